```python
import functools
import jax, jax.numpy as jnp
from jax import lax
import numpy as np

D_MODEL = 1024
BATCH = 2
SEQ = 8192
DEPTH = 1
DEC_BATCH = 128
DEC_SEQ = 1
PAST_LEN = 16384
PAGE_SIZE = 128

PLE_DIM = 256
D_FF = 2816
MLA_HEADS = 8
MLA_NOPE = 64
MLA_ROPE = 32
MLA_V = 64
Q_LORA = 256
KV_LORA = 128
RET_HEADS = 4
RET_DK = 128
RET_DV = 128
RET_CHUNK = 128
Q_BLOCK = 128
ROPE_THETA = 10000.0
EPS = 1e-6
IN_SPLITS = (Q_LORA, KV_LORA, MLA_ROPE, RET_HEADS * RET_DK, RET_HEADS * RET_DK,
             RET_HEADS * RET_DV, RET_HEADS * RET_DV, D_MODEL, D_MODEL)
IN_COLS = Q_LORA + KV_LORA + MLA_ROPE + 2 * RET_HEADS * RET_DK + 2 * RET_HEADS * RET_DV + 2 * D_MODEL

kernel_name = 'mla_retention_gated_hybrid_step'


def _split_points():
    pts, acc = [], 0
    for s in IN_SPLITS[:-1]:
        acc += s
        pts.append(acc)
    return pts


def rmsnorm(x, g):
    xf = x.astype(jnp.float32)
    y = xf * lax.rsqrt(jnp.mean(xf * xf, axis=-1, keepdims=True) + EPS)
    return (y * g.astype(jnp.float32)).astype(x.dtype)


def head_layernorm(o, g):
    mu = jnp.mean(o, axis=-1, keepdims=True)
    d = o - mu
    var = jnp.mean(d * d, axis=-1, keepdims=True)
    return d * lax.rsqrt(var + EPS) * g.astype(jnp.float32)[None, None]


def rope(x, pos):
    half = x.shape[-1] // 2
    inv = ROPE_THETA ** (-jnp.arange(half, dtype=jnp.float32) / half)
    ang = pos.astype(jnp.float32)[:, None] * inv[None, :]
    cos = jnp.cos(ang)[None, :, None, :]
    sin = jnp.sin(ang)[None, :, None, :]
    xf = x.astype(jnp.float32)
    x1, x2 = xf[..., :half], xf[..., half:]
    return jnp.concatenate([x1 * cos - x2 * sin, x1 * sin + x2 * cos], axis=-1).astype(x.dtype)


def swiglu_half(x, g, w_gate, w_up, w_down):
    u = rmsnorm(x, g)
    return x + 0.5 * ((jax.nn.silu(u @ w_gate) * (u @ w_up)) @ w_down)


def retention_log_gamma():
    return jnp.log1p(-jnp.exp2(-5.0 - jnp.arange(RET_HEADS, dtype=jnp.float32)))


def retention_chunk(s0, q, k, v, log_gamma):
    q = q.astype(jnp.float32)
    k = k.astype(jnp.float32)
    v = v.astype(jnp.float32)
    c = q.shape[2]
    idx = jnp.arange(c, dtype=jnp.float32)
    diff = idx[:, None] - idx[None, :]
    decay = jnp.where(diff[None] >= 0, jnp.exp(log_gamma[:, None, None] * jnp.maximum(diff, 0.0)[None]), 0.0)
    scores = jnp.einsum('bhid,bhjd->bhij', q, k) * decay[None]
    inner = jnp.einsum('bhij,bhjv->bhiv', scores, v)
    q_decay = jnp.exp(log_gamma[:, None] * (idx + 1.0)[None, :])
    cross = jnp.einsum('bhid,bhdv->bhiv', q, s0) * q_decay[None, :, :, None]
    k_decay = jnp.exp(log_gamma[:, None] * (c - 1.0 - idx)[None, :])
    s_new = jnp.exp(log_gamma * c)[None, :, None, None] * s0 + jnp.einsum('bhjd,bhjv->bhdv', k * k_decay[None, :, :, None], v)
    return inner + cross, s_new


def retention_prompt(q, k, v, log_gamma):
    b, t = q.shape[:2]
    nc = t // RET_CHUNK

    def to_chunks(a):
        a = a.transpose(0, 2, 1, 3)
        a = a.reshape(b, RET_HEADS, nc, RET_CHUNK, a.shape[-1])
        return jnp.moveaxis(a, 2, 0)

    def step(s, blk):
        o, s = retention_chunk(s, blk[0], blk[1], blk[2], log_gamma)
        return s, o

    s0 = jnp.zeros((b, RET_HEADS, RET_DK, RET_DV), jnp.float32)
    s_fin, o = lax.scan(step, s0, (to_chunks(q), to_chunks(k), to_chunks(v)))
    o = jnp.moveaxis(o, 0, 2).reshape(b, RET_HEADS, t, RET_DV).transpose(0, 2, 1, 3)
    return o, s_fin


def retention_sample(q, k, v, state, log_gamma):
    o, s_new = retention_chunk(state.astype(jnp.float32), q.transpose(0, 2, 1, 3),
                               k.transpose(0, 2, 1, 3), v.transpose(0, 2, 1, 3), log_gamma)
    return o.transpose(0, 2, 1, 3), s_new


def mla_attend_prompt(q_nope, q_pe, c_kv, k_pe, w_kv_b):
    b, t = q_nope.shape[:2]
    kv = (c_kv @ w_kv_b).reshape(b, t, MLA_HEADS, MLA_NOPE + MLA_V)
    k_nope, v = kv[..., :MLA_NOPE], kv[..., MLA_NOPE:]
    nqb = t // Q_BLOCK
    qn = q_nope.reshape(b, nqb, Q_BLOCK, MLA_HEADS, MLA_NOPE).swapaxes(0, 1)
    qp = q_pe.reshape(b, nqb, Q_BLOCK, MLA_HEADS, MLA_ROPE).swapaxes(0, 1)
    kpos = jnp.arange(t)
    scale = (MLA_NOPE + MLA_ROPE) ** -0.5

    def one_block(args):
        qn_b, qp_b, bi = args
        s = (jnp.einsum('bqhd,bkhd->bhqk', qn_b, k_nope)
             + jnp.einsum('bqhr,bkr->bhqk', qp_b, k_pe)).astype(jnp.float32) * scale
        qpos = bi * Q_BLOCK + jnp.arange(Q_BLOCK)
        s = jnp.where(kpos[None, :] <= qpos[:, None], s, -jnp.inf)
        p = jax.nn.softmax(s, axis=-1).astype(v.dtype)
        return jnp.einsum('bhqk,bkhd->bqhd', p, v)

    o = lax.map(one_block, (qn, qp, jnp.arange(nqb)))
    return o.swapaxes(0, 1).reshape(b, t, MLA_HEADS * MLA_V)


def mla_attend_sample(q_nope, q_pe, c_kv, k_pe, w_kv_b, cache_ckv, cache_kpe, page_table, layer):
    b, t = q_nope.shape[:2]
    n_past = page_table.shape[1] * PAGE_SIZE
    w = w_kv_b.reshape(KV_LORA, MLA_HEADS, MLA_NOPE + MLA_V)
    w_uk, w_uv = w[..., :MLA_NOPE], w[..., MLA_NOPE:]
    q_lat = jnp.einsum('bthd,chd->bthc', q_nope, w_uk)
    past_ckv = cache_ckv[layer, page_table].reshape(b, n_past, KV_LORA).astype(c_kv.dtype)
    past_kpe = cache_kpe[layer, page_table].reshape(b, n_past, MLA_ROPE).astype(k_pe.dtype)
    scale = (MLA_NOPE + MLA_ROPE) ** -0.5
    s_past = (jnp.einsum('bthc,bkc->bhtk', q_lat, past_ckv)
              + jnp.einsum('bthr,bkr->bhtk', q_pe, past_kpe)).astype(jnp.float32) * scale
    s_new = (jnp.einsum('bthc,bkc->bhtk', q_lat, c_kv)
             + jnp.einsum('bthr,bkr->bhtk', q_pe, k_pe)).astype(jnp.float32) * scale
    tri = jnp.arange(t)[None, :] <= jnp.arange(t)[:, None]
    s_new = jnp.where(tri[None, None], s_new, -jnp.inf)
    p = jax.nn.softmax(jnp.concatenate([s_past, s_new], axis=-1), axis=-1).astype(c_kv.dtype)
    o_lat = (jnp.einsum('bhtk,bkc->bthc', p[..., :n_past], past_ckv)
             + jnp.einsum('bhtk,bkc->bthc', p[..., n_past:], c_kv))
    o = jnp.einsum('bthc,chd->bthd', o_lat, w_uv)
    return o.reshape(b, t, MLA_HEADS * MLA_V)


def hybrid_layer(x, p_emb, pos, lp, attend, retain):
    b, t = x.shape[:2]
    h = swiglu_half(x, lp['ffn1_norm'], lp['ffn1_w_gate'], lp['ffn1_w_up'], lp['ffn1_w_down'])
    u = rmsnorm(h, lp['mix_norm'])
    z = u @ lp['w_in']
    cq, ckv_raw, kpe_raw, rq, rk, rv, rg, ga, gr = jnp.split(z, _split_points(), axis=-1)
    q = (rmsnorm(cq, lp['q_a_norm']) @ lp['w_q_b']).reshape(b, t, MLA_HEADS, MLA_NOPE + MLA_ROPE)
    q_nope = q[..., :MLA_NOPE]
    q_pe = rope(q[..., MLA_NOPE:], pos)
    c_kv = rmsnorm(ckv_raw, lp['kv_a_norm'])
    k_pe = rope(kpe_raw[:, :, None, :], pos)[:, :, 0, :]
    o_att = attend(q_nope, q_pe, c_kv, k_pe)
    qr = rope(rq.reshape(b, t, RET_HEADS, RET_DK), pos)
    kr = rope(rk.reshape(b, t, RET_HEADS, RET_DK), pos) * (RET_DK ** -0.5)
    vr = rv.reshape(b, t, RET_HEADS, RET_DV)
    o_ret, s_new = retain(qr, kr, vr)
    o_ret = head_layernorm(o_ret, lp['ret_norm']).astype(x.dtype).reshape(b, t, RET_HEADS * RET_DV)
    o_ret = jax.nn.silu(rg) * o_ret
    merged = (jax.nn.sigmoid(ga) * (o_att @ lp['w_branch_att'])
              + jax.nn.sigmoid(gr) * (o_ret @ lp['w_branch_ret']))
    h = h + merged @ lp['w_out']
    h = swiglu_half(h, lp['ffn2_norm'], lp['ffn2_w_gate'], lp['ffn2_w_up'], lp['ffn2_w_down'])
    gate = jax.nn.sigmoid(rmsnorm(h, lp['ple_norm']) @ lp['w_ple_gate'])
    h = h + gate * (p_emb @ lp['w_ple_proj'])
    return h, c_kv, k_pe, s_new


def setup_inputs(seed: int = 0) -> dict:
    key = jax.random.key(seed)
    ks = iter(jax.random.split(key, 40))
    n_pages = PAST_LEN // PAGE_SIZE
    n_used = DEC_BATCH * n_pages
    n_pool = n_used + n_used // 4

    def w(shape, fan_in):
        return jax.random.normal(next(ks), shape, jnp.float32) * (fan_in ** -0.5)

    def gain(shape):
        return 1.0 + 0.02 * jax.random.normal(next(ks), shape, jnp.float32)

    d = {}
    d['x_prompt'] = jax.random.normal(next(ks), (BATCH, SEQ, D_MODEL), jnp.float32)
    d['x_sample'] = jax.random.normal(next(ks), (DEC_BATCH, DEC_SEQ, D_MODEL), jnp.float32)
    d['cache_ckv'] = jax.random.normal(next(ks), (DEPTH, n_pool, PAGE_SIZE, KV_LORA), jnp.float32)
    d['cache_kpe'] = jax.random.normal(next(ks), (DEPTH, n_pool, PAGE_SIZE, MLA_ROPE), jnp.float32)
    d['state_ret'] = 0.5 * jax.random.normal(next(ks), (DEPTH, DEC_BATCH, RET_HEADS, RET_DK, RET_DV), jnp.float32)
    d['page_table'] = jax.random.permutation(next(ks), n_pool)[:n_used].reshape(DEC_BATCH, n_pages).astype(jnp.int32)
    d['p_prompt'] = jax.random.normal(next(ks), (DEPTH, BATCH, SEQ, PLE_DIM), jnp.float32)
    d['p_sample'] = jax.random.normal(next(ks), (DEPTH, DEC_BATCH, DEC_SEQ, PLE_DIM), jnp.float32)
    d['ffn1_norm'] = gain((DEPTH, D_MODEL))
    d['ffn1_w_gate'] = w((DEPTH, D_MODEL, D_FF), D_MODEL)
    d['ffn1_w_up'] = w((DEPTH, D_MODEL, D_FF), D_MODEL)
    d['ffn1_w_down'] = w((DEPTH, D_FF, D_MODEL), D_FF)
    d['mix_norm'] = gain((DEPTH, D_MODEL))
    d['w_in'] = w((DEPTH, D_MODEL, IN_COLS), D_MODEL)
    d['q_a_norm'] = gain((DEPTH, Q_LORA))
    d['w_q_b'] = w((DEPTH, Q_LORA, MLA_HEADS * (MLA_NOPE + MLA_ROPE)), Q_LORA)
    d['kv_a_norm'] = gain((DEPTH, KV_LORA))
    d['w_kv_b'] = w((DEPTH, KV_LORA, MLA_HEADS * (MLA_NOPE + MLA_V)), KV_LORA)
    d['ret_norm'] = gain((DEPTH, RET_HEADS, RET_DV))
    d['w_branch_att'] = w((DEPTH, MLA_HEADS * MLA_V, D_MODEL), MLA_HEADS * MLA_V)
    d['w_branch_ret'] = w((DEPTH, RET_HEADS * RET_DV, D_MODEL), RET_HEADS * RET_DV)
    d['w_out'] = w((DEPTH, D_MODEL, D_MODEL), D_MODEL)
    d['ffn2_norm'] = gain((DEPTH, D_MODEL))
    d['ffn2_w_gate'] = w((DEPTH, D_MODEL, D_FF), D_MODEL)
    d['ffn2_w_up'] = w((DEPTH, D_MODEL, D_FF), D_MODEL)
    d['ffn2_w_down'] = w((DEPTH, D_FF, D_MODEL), D_FF)
    d['ple_norm'] = gain((DEPTH, D_MODEL))
    d['w_ple_gate'] = w((DEPTH, D_MODEL, D_MODEL), D_MODEL)
    d['w_ple_proj'] = w((DEPTH, PLE_DIM, D_MODEL), PLE_DIM)
    d['final_norm'] = gain((D_MODEL,))
    return d


def reference(x_prompt, x_sample, cache_ckv, cache_kpe, state_ret, page_table, p_prompt, p_sample,
              ffn1_norm, ffn1_w_gate, ffn1_w_up, ffn1_w_down, mix_norm, w_in, q_a_norm, w_q_b,
              kv_a_norm, w_kv_b, ret_norm, w_branch_att, w_branch_ret, w_out,
              ffn2_norm, ffn2_w_gate, ffn2_w_up, ffn2_w_down, ple_norm, w_ple_gate, w_ple_proj,
              final_norm):
    n_past = page_table.shape[1] * PAGE_SIZE
    pos_p = jnp.arange(x_prompt.shape[1], dtype=jnp.int32)
    pos_s = n_past + jnp.arange(x_sample.shape[1], dtype=jnp.int32)
    log_gamma = retention_log_gamma()
    hp, hs = x_prompt, x_sample
    ckv_p, kpe_p, ret_p, ckv_s, kpe_s, ret_s = [], [], [], [], [], []
    for i in range(DEPTH):
        lp = {
            'ffn1_norm': ffn1_norm[i], 'ffn1_w_gate': ffn1_w_gate[i], 'ffn1_w_up': ffn1_w_up[i],
            'ffn1_w_down': ffn1_w_down[i], 'mix_norm': mix_norm[i], 'w_in': w_in[i],
            'q_a_norm': q_a_norm[i], 'w_q_b': w_q_b[i], 'kv_a_norm': kv_a_norm[i],
            'ret_norm': ret_norm[i], 'w_branch_att': w_branch_att[i], 'w_branch_ret': w_branch_ret[i],
            'w_out': w_out[i], 'ffn2_norm': ffn2_norm[i], 'ffn2_w_gate': ffn2_w_gate[i],
            'ffn2_w_up': ffn2_w_up[i], 'ffn2_w_down': ffn2_w_down[i], 'ple_norm': ple_norm[i],
            'w_ple_gate': w_ple_gate[i], 'w_ple_proj': w_ple_proj[i],
        }
        attend_p = functools.partial(mla_attend_prompt, w_kv_b=w_kv_b[i])
        retain_p = functools.partial(retention_prompt, log_gamma=log_gamma)
        hp, c1, k1, s1 = hybrid_layer(hp, p_prompt[i], pos_p, lp, attend_p, retain_p)
        attend_s = functools.partial(mla_attend_sample, w_kv_b=w_kv_b[i], cache_ckv=cache_ckv,
                                     cache_kpe=cache_kpe, page_table=page_table, layer=i)
        retain_s = functools.partial(retention_sample, state=state_ret[i], log_gamma=log_gamma)
        hs, c2, k2, s2 = hybrid_layer(hs, p_sample[i], pos_s, lp, attend_s, retain_s)
        ckv_p.append(c1); kpe_p.append(k1); ret_p.append(s1)
        ckv_s.append(c2); kpe_s.append(k2); ret_s.append(s2)
    y_prompt = rmsnorm(hp, final_norm)
    y_sample = rmsnorm(hs, final_norm)
    new_ckv_prompt = jnp.stack(ckv_p, axis=0)
    new_kpe_prompt = jnp.stack(kpe_p, axis=0)
    new_ret_prompt = jnp.stack(ret_p, axis=0)
    new_ckv_sample = jnp.stack(ckv_s, axis=0)
    new_kpe_sample = jnp.stack(kpe_s, axis=0)
    new_ret_sample = jnp.stack(ret_s, axis=0)
    return (y_prompt, y_sample, new_ckv_prompt, new_kpe_prompt, new_ret_prompt,
            new_ckv_sample, new_kpe_sample, new_ret_sample)
```

```python
import functools

import jax
import jax.numpy as jnp
import numpy as np
from jax import lax
from jax.experimental import pallas as pl
from jax.experimental.pallas import tpu as pltpu

F32 = jnp.float32
BF16 = jnp.bfloat16

D_MODEL = 1024
D_FF = 2816
PLE_DIM = 256
MLA_HEADS = 8
MLA_NOPE = 64
MLA_ROPE = 32
MLA_V = 64
Q_LORA = 256
KV_LORA = 128
RET_HEADS = 4
RET_DK = 128
RET_DV = 128
PAGE_SIZE = 128
ROPE_THETA = 10000.0
EPS = 1e-6

LANES = 128
HEAD_PAD = LANES
ROPE_OFF = MLA_NOPE
ROPE_HALF = MLA_ROPE // 2
SOFTMAX_SCALE = (MLA_NOPE + MLA_ROPE) ** -0.5
NEG_BIG = -1e30

TOKEN_TILE = 256
ATTN_TILE = 512
RET_CHUNK = 128
PAGES_PER_STEP = 16
KEY_W = 2 * LANES
VMEM_LIMIT = 60 * 1024 * 1024

_OFF_CQ = 0
_OFF_CKV = _OFF_CQ + Q_LORA
_OFF_KPE = _OFF_CKV + KV_LORA
_OFF_RQ = _OFF_KPE + MLA_ROPE
_OFF_RK = _OFF_RQ + RET_HEADS * RET_DK
_OFF_RV = _OFF_RK + RET_HEADS * RET_DK
_OFF_RG = _OFF_RV + RET_HEADS * RET_DV
_RET_W = RET_HEADS * RET_DK
_FRONT_COLS = Q_LORA + KV_LORA + LANES + 3 * _RET_W


def _rms(x, g):
    return x * lax.rsqrt(jnp.mean(x * x, axis=-1, keepdims=True) + EPS) * g


def _dot(a, b):
    return jnp.dot(a, b, preferred_element_type=F32)


def _dot_nt(a, b):
    return lax.dot_general(a, b, (((1,), (1,)), ((), ())), preferred_element_type=F32)


def _swiglu_half(x, g, wg, wu, wd):
    xn = _rms(x, g).astype(BF16)
    gate = _dot(xn, wg)
    up = _dot(xn, wu)
    act = (gate * jax.nn.sigmoid(gate) * up).astype(BF16)
    return x + 0.5 * _dot(act, wd)


def _const_spec(shape):
    n = len(shape)
    return pl.BlockSpec(shape, lambda *_: (0,) * n, pipeline_mode=pl.Buffered(1))


def _front_kernel(x_ref, cm_ref, s1_ref, s2_ref, cr_ref, sr_ref,
                  g1_ref, wg_ref, wu_ref, wd_ref, gmix_ref, win_ref,
                  gqa_ref, wqb_ref, gkv_ref, wk_ref, wv_ref,
                  h_ref, q_ref, k_ref, v_ref, ckv_ref, kpe_ref, qr_ref, kr_ref, rv_ref):
    x = x_ref[...]
    h = _swiglu_half(x, g1_ref[...], wg_ref[...], wu_ref[...], wd_ref[...])
    h_ref[...] = h
    un = _rms(h, gmix_ref[...]).astype(BF16)
    z = _dot(un, win_ref[...])

    cm, s1, s2 = cm_ref[...], s1_ref[...], s2_ref[...]

    def mla_rope(t):
        return (t * cm + pltpu.roll(t, LANES - ROPE_HALF, 1) * s1
                + pltpu.roll(t, ROPE_HALF, 1) * s2)

    cq = z[:, _OFF_CQ:_OFF_CQ + Q_LORA]
    q = _dot(_rms(cq, gqa_ref[...]).astype(BF16), wqb_ref[...])
    c_kv = _rms(z[:, Q_LORA:Q_LORA + KV_LORA], gkv_ref[...])
    ckv_ref[...] = c_kv
    c_kv_b = c_kv.astype(BF16)
    k_nope = _dot(c_kv_b, wk_ref[...])
    v_ref[...] = _dot(c_kv_b, wv_ref[...]).astype(BF16)
    k_pe = mla_rope(z[:, Q_LORA + KV_LORA:Q_LORA + KV_LORA + LANES])
    kpe_ref[...] = k_pe
    for hd in range(MLA_HEADS):
        sl = slice(hd * HEAD_PAD, (hd + 1) * HEAD_PAD)
        q_ref[:, sl] = (mla_rope(q[:, sl]) * SOFTMAX_SCALE).astype(BF16)
        k_ref[:, sl] = (k_nope[:, sl] + k_pe).astype(BF16)

    cr, sr = cr_ref[...], sr_ref[...]
    base = Q_LORA + KV_LORA + LANES
    for hd in range(RET_HEADS):
        sl = slice(hd * RET_DK, (hd + 1) * RET_DK)
        rq = z[:, base + hd * RET_DK:base + (hd + 1) * RET_DK]
        rk = z[:, base + _RET_W + hd * RET_DK:base + _RET_W + (hd + 1) * RET_DK]
        qr_ref[:, sl] = rq * cr + pltpu.roll(rq, RET_DK // 2, 1) * sr
        kr_ref[:, sl] = (rk * cr + pltpu.roll(rk, RET_DK // 2, 1) * sr) * (RET_DK ** -0.5)
    rv_ref[...] = z[:, base + 2 * _RET_W:base + 3 * _RET_W]


def _front_stage(x, tables, w, tile, table_tiles):
    n = x.shape[0]
    row = lambda width: pl.BlockSpec((tile, width), lambda i: (i, 0))
    tab = pl.BlockSpec((tile, LANES), lambda i: (i % table_tiles, 0))
    weights = [w['ffn1_norm'], w['ffn1_w_gate'], w['ffn1_w_up'], w['ffn1_w_down'], w['mix_norm'],
               w['w_in_front'], w['q_a_norm'], w['w_q_b'], w['kv_a_norm'], w['w_k'], w['w_v']]
    out_widths = [(D_MODEL, F32), (MLA_HEADS * HEAD_PAD, BF16), (MLA_HEADS * HEAD_PAD, BF16),
                  (MLA_HEADS * MLA_V, BF16), (KV_LORA, F32), (LANES, F32),
                  (_RET_W, F32), (_RET_W, F32), (_RET_W, F32)]
    return pl.pallas_call(
        _front_kernel,
        grid=(n // tile,),
        in_specs=[row(D_MODEL)] + [tab] * 5 + [_const_spec(a.shape) for a in weights],
        out_specs=[row(wd) for wd, _ in out_widths],
        out_shape=[jax.ShapeDtypeStruct((n, wd), dt) for wd, dt in out_widths],
        compiler_params=pltpu.CompilerParams(dimension_semantics=("arbitrary",),
                                             vmem_limit_bytes=VMEM_LIMIT),
        name="front_stage",
    )(x, *tables, *weights)


def _attn_kernel(qi_ref, ki_ref, q_ref, k_ref, v_ref, o_ref, m_sc, l_sc, acc_sc):
    t = pl.program_id(1)
    qi = qi_ref[t]
    ki = ki_ref[t]

    @pl.when(ki == 0)
    def _():
        m_sc[...] = jnp.full(m_sc.shape, NEG_BIG, F32)
        l_sc[...] = jnp.zeros(l_sc.shape, F32)
        acc_sc[...] = jnp.zeros(acc_sc.shape, F32)

    def update(masked):
        for hd in range(MLA_HEADS):
            qh = q_ref[0, :, hd * HEAD_PAD:(hd + 1) * HEAD_PAD]
            kh = k_ref[0, :, hd * HEAD_PAD:(hd + 1) * HEAD_PAD]
            s = _dot_nt(qh, kh)
            if masked:
                rows = lax.broadcasted_iota(jnp.int32, s.shape, 0)
                cols = lax.broadcasted_iota(jnp.int32, s.shape, 1)
                s = jnp.where(cols <= rows, s, NEG_BIG)
            m_prev = m_sc[hd]
            m_new = jnp.maximum(m_prev, jnp.max(s, axis=-1, keepdims=True))
            alpha = jnp.exp(m_prev - m_new)
            p = jnp.exp(s - m_new)
            l_sc[hd] = alpha * l_sc[hd] + jnp.sum(p, axis=-1, keepdims=True)
            acc_sc[hd] = alpha * acc_sc[hd] + _dot(p.astype(BF16),
                                                    v_ref[0, :, hd * MLA_V:(hd + 1) * MLA_V])
            m_sc[hd] = m_new

    @pl.when(ki < qi)
    def _():
        update(False)

    @pl.when(ki == qi)
    def _():
        update(True)
        for hd in range(MLA_HEADS):
            o_ref[0, :, hd * MLA_V:(hd + 1) * MLA_V] = (acc_sc[hd] / l_sc[hd]).astype(o_ref.dtype)


def _prompt_attention(q, k, v):
    b, t, _ = q.shape
    nt = t // ATTN_TILE
    qi_list = np.array([i for i in range(nt) for _ in range(i + 1)], np.int32)
    ki_list = np.array([j for i in range(nt) for j in range(i + 1)], np.int32)
    grid_spec = pltpu.PrefetchScalarGridSpec(
        num_scalar_prefetch=2,
        grid=(b, len(qi_list)),
        in_specs=[
            pl.BlockSpec((1, ATTN_TILE, MLA_HEADS * HEAD_PAD), lambda bb, s, qi, ki: (bb, qi[s], 0)),
            pl.BlockSpec((1, ATTN_TILE, MLA_HEADS * HEAD_PAD), lambda bb, s, qi, ki: (bb, ki[s], 0)),
            pl.BlockSpec((1, ATTN_TILE, MLA_HEADS * MLA_V), lambda bb, s, qi, ki: (bb, ki[s], 0)),
        ],
        out_specs=pl.BlockSpec((1, ATTN_TILE, MLA_HEADS * MLA_V), lambda bb, s, qi, ki: (bb, qi[s], 0)),
        scratch_shapes=[pltpu.VMEM((MLA_HEADS, ATTN_TILE, 1), F32),
                        pltpu.VMEM((MLA_HEADS, ATTN_TILE, 1), F32),
                        pltpu.VMEM((MLA_HEADS, ATTN_TILE, MLA_V), F32)],
    )
    return pl.pallas_call(
        _attn_kernel,
        grid_spec=grid_spec,
        out_shape=jax.ShapeDtypeStruct((b, t, MLA_HEADS * MLA_V), BF16),
        compiler_params=pltpu.CompilerParams(dimension_semantics=("arbitrary", "arbitrary"),
                                             vmem_limit_bytes=VMEM_LIMIT),
        name="prompt_attention",
    )(jnp.asarray(qi_list), jnp.asarray(ki_list), q, k, v)


def _head_layernorm(o, g):
    mu = jnp.mean(o, axis=-1, keepdims=True)
    d = o - mu
    var = jnp.mean(d * d, axis=-1, keepdims=True)
    return d * lax.rsqrt(var + EPS) * g


def _ret_prompt_kernel(q_ref, k_ref, v_ref, dec_ref, qd_ref, kd_ref, gc_ref, g_ref,
                       o_ref, s_out_ref, s_sc):
    c = pl.program_id(1)

    @pl.when(c == 0)
    def _():
        s_sc[...] = jnp.zeros(s_sc.shape, F32)

    for hd in range(RET_HEADS):
        sl = slice(hd * RET_DK, (hd + 1) * RET_DK)
        q = q_ref[0, :, sl]
        k = k_ref[0, :, sl]
        vb = v_ref[0, :, sl].astype(BF16)
        qb = q.astype(BF16)
        state = s_sc[hd]
        scores = _dot_nt(qb, k.astype(BF16)) * dec_ref[hd]
        inner = _dot(scores.astype(BF16), vb)
        cross = _dot(qb, state.astype(BF16)) * qd_ref[hd]
        o_ref[0, :, sl] = _head_layernorm(inner + cross, g_ref[hd:hd + 1, :])
        k_dec_t = jnp.transpose(k * kd_ref[hd]).astype(BF16)
        s_sc[hd] = gc_ref[hd:hd + 1, :] * state + _dot(k_dec_t, vb)

    @pl.when(c == pl.num_programs(1) - 1)
    def _():
        s_out_ref[0] = s_sc[...]


def _retention_consts(chunk):
    log_gamma = jnp.log1p(-jnp.exp2(-5.0 - jnp.arange(RET_HEADS, dtype=F32)))
    idx = jnp.arange(chunk, dtype=F32)
    diff = idx[:, None] - idx[None, :]
    decay = jnp.where(diff[None] >= 0,
                      jnp.exp(log_gamma[:, None, None] * jnp.maximum(diff, 0.0)[None]), 0.0)
    q_decay = jnp.exp(log_gamma[:, None] * (idx + 1.0)[None, :])
    k_decay = jnp.exp(log_gamma[:, None] * (chunk - 1.0 - idx)[None, :])
    chunk_decay = jnp.exp(log_gamma * chunk)
    lane = lambda a: jnp.broadcast_to(a[..., None], a.shape + (LANES,))
    return decay, lane(q_decay), lane(k_decay), lane(chunk_decay)


def _prompt_retention(qr, kr, rv, ret_norm):
    b, t, _ = qr.shape
    decay, q_decay, k_decay, chunk_decay = _retention_consts(RET_CHUNK)
    seq = pl.BlockSpec((1, RET_CHUNK, _RET_W), lambda bb, c: (bb, c, 0))
    return pl.pallas_call(
        _ret_prompt_kernel,
        grid=(b, t // RET_CHUNK),
        in_specs=[seq, seq, seq, _const_spec(decay.shape), _const_spec(q_decay.shape),
                  _const_spec(k_decay.shape), _const_spec(chunk_decay.shape),
                  _const_spec(ret_norm.shape)],
        out_specs=[seq, pl.BlockSpec((1, RET_HEADS, RET_DK, RET_DV), lambda bb, c: (bb, 0, 0, 0))],
        out_shape=[jax.ShapeDtypeStruct((b, t, _RET_W), F32),
                   jax.ShapeDtypeStruct((b, RET_HEADS, RET_DK, RET_DV), F32)],
        scratch_shapes=[pltpu.VMEM((RET_HEADS, RET_DK, RET_DV), F32)],
        compiler_params=pltpu.CompilerParams(dimension_semantics=("arbitrary", "arbitrary")),
        name="prompt_retention",
    )(qr, kr, rv, decay, q_decay, k_decay, chunk_decay, ret_norm)


def _ret_sample_kernel(q_ref, k_ref, v_ref, s_ref, gam_ref, g_ref, o_ref, s_out_ref):
    rows = lax.broadcasted_iota(jnp.int32, (RET_DK, RET_DK), 0)
    cols = lax.broadcasted_iota(jnp.int32, (RET_DK, RET_DK), 1)
    eye = rows == cols

    def column(r):
        return jnp.sum(jnp.where(eye, jnp.broadcast_to(r, (RET_DK, RET_DK)), 0.0),
                       axis=1, keepdims=True)

    for hd in range(RET_HEADS):
        sl = slice(hd * RET_DK, (hd + 1) * RET_DK)
        q = q_ref[0, :, sl]
        k = k_ref[0, :, sl]
        v = v_ref[0, :, sl]
        gam = gam_ref[hd:hd + 1, :]
        state = s_ref[0, hd]
        cross = jnp.sum(column(q) * state, axis=0, keepdims=True) * gam
        inner = jnp.sum(q * k, axis=-1, keepdims=True) * v
        o_ref[0, :, sl] = _head_layernorm(inner + cross, g_ref[hd:hd + 1, :])
        s_out_ref[0, hd] = gam * state + column(k) * v


def _sample_retention(qr, kr, rv, state, ret_norm):
    n = qr.shape[0]
    log_gamma = jnp.log1p(-jnp.exp2(-5.0 - jnp.arange(RET_HEADS, dtype=F32)))
    gam = jnp.broadcast_to(jnp.exp(log_gamma * 1.0)[:, None], (RET_HEADS, LANES))
    tok = pl.BlockSpec((1, 1, _RET_W), lambda i: (i, 0, 0))
    st = pl.BlockSpec((1, RET_HEADS, RET_DK, RET_DV), lambda i: (i, 0, 0, 0))
    r3 = lambda a: a.reshape(n, 1, _RET_W)
    o, s_new = pl.pallas_call(
        _ret_sample_kernel,
        grid=(n,),
        in_specs=[tok, tok, tok, st, _const_spec(gam.shape), _const_spec(ret_norm.shape)],
        out_specs=[tok, st],
        out_shape=[jax.ShapeDtypeStruct((n, 1, _RET_W), F32),
                   jax.ShapeDtypeStruct(state.shape, F32)],
        compiler_params=pltpu.CompilerParams(dimension_semantics=("arbitrary",)),
        name="sample_retention",
    )(r3(qr), r3(kr), r3(rv), state, gam, ret_norm)
    return o.reshape(n, _RET_W), s_new


def _absorb_q_kernel(q_ref, w_ref, o_ref):
    for hd in range(MLA_HEADS):
        o_ref[hd] = _dot(q_ref[:, hd * HEAD_PAD:(hd + 1) * HEAD_PAD], w_ref[hd])


def _absorb_q(q, w_absorb):
    n = q.shape[0]
    return pl.pallas_call(
        _absorb_q_kernel,
        out_shape=jax.ShapeDtypeStruct((MLA_HEADS, n, KEY_W), F32),
        name="sample_absorb_q",
    )(q, w_absorb)


def _unabsorb_kernel(o_ref, w_ref, out_ref):
    acc = _dot(o_ref[0].astype(BF16), w_ref[0])
    for hd in range(1, MLA_HEADS):
        acc = acc + _dot(o_ref[hd].astype(BF16), w_ref[hd])
    out_ref[...] = acc.astype(out_ref.dtype)


def _unabsorb(o_lat, w_unabsorb):
    n = o_lat.shape[1]
    return pl.pallas_call(
        _unabsorb_kernel,
        out_shape=jax.ShapeDtypeStruct((n, MLA_HEADS * MLA_V), BF16),
        name="sample_unabsorb",
    )(o_lat, w_unabsorb)


def _paged_kernel(pt_ref, q_ref, knew_ref, ckv_hbm, kpe_hbm, o_ref, buf, kpe_buf, sem,
                  m_sc, l_sc, acc_sc):
    b = pl.program_id(0)
    c = pl.program_id(1)
    n_chunks = pl.num_programs(1)
    step = b * n_chunks + c
    total = pl.num_programs(0) * n_chunks
    slot = step % 2
    chunk_keys = PAGES_PER_STEP * PAGE_SIZE

    def page_copies(bb, cc, sl):
        out = []
        for j in range(PAGES_PER_STEP):
            pg = pt_ref[bb, cc * PAGES_PER_STEP + j]
            rows = pl.ds(j * PAGE_SIZE, PAGE_SIZE)
            out.append(pltpu.make_async_copy(ckv_hbm.at[pg], buf.at[sl, rows, pl.ds(0, KV_LORA)],
                                             sem.at[sl, 0]))
            out.append(pltpu.make_async_copy(kpe_hbm.at[pg], kpe_buf.at[sl, rows], sem.at[sl, 1]))
        return out

    @pl.when(step == 0)
    def _():
        buf[:, :, KV_LORA:] = jnp.zeros((2, chunk_keys, KEY_W - KV_LORA), F32)
        for cp in page_copies(0, 0, 0):
            cp.start()

    @pl.when(step + 1 < total)
    def _():
        nxt = step + 1
        for cp in page_copies(nxt // n_chunks, nxt % n_chunks, 1 - slot):
            cp.start()

    for cp in page_copies(b, c, slot):
        cp.wait()
    buf[slot, :, KV_LORA:KV_LORA + MLA_ROPE] = kpe_buf[slot]

    @pl.when(c == 0)
    def _():
        m_sc[...] = jnp.full(m_sc.shape, NEG_BIG, F32)
        l_sc[...] = jnp.zeros(l_sc.shape, F32)
        acc_sc[...] = jnp.zeros(acc_sc.shape, F32)

    q = q_ref[0]
    s = _dot_nt(q, buf[slot])
    m_prev = m_sc[...]
    m_new = jnp.maximum(m_prev, jnp.max(s, axis=-1, keepdims=True))
    alpha = jnp.exp(m_prev - m_new)
    p = jnp.exp(s - m_new)
    l_new = alpha * l_sc[...] + jnp.sum(p, axis=-1, keepdims=True)
    acc_new = alpha * acc_sc[...] + _dot(p, buf[slot, :, 0:KV_LORA])
    m_sc[...] = m_new
    l_sc[...] = l_new
    acc_sc[...] = acc_new

    @pl.when(c == n_chunks - 1)
    def _():
        kn = knew_ref[0]
        s_self = jnp.sum(q * kn, axis=-1, keepdims=True)
        m_fin = jnp.maximum(m_new, s_self)
        a = jnp.exp(m_new - m_fin)
        p_self = jnp.exp(s_self - m_fin)
        l_fin = a * l_new + p_self
        o_ref[0] = (a * acc_new + p_self * kn[:, 0:KV_LORA]) / l_fin


def _paged_attention(page_table, q_abs, k_new, cache_ckv, cache_kpe):
    n, n_pages = page_table.shape
    n_chunks = n_pages // PAGES_PER_STEP
    chunk_keys = PAGES_PER_STEP * PAGE_SIZE
    grid_spec = pltpu.PrefetchScalarGridSpec(
        num_scalar_prefetch=1,
        grid=(n, n_chunks),
        in_specs=[
            pl.BlockSpec((1, MLA_HEADS, KEY_W), lambda b, c, pt: (b, 0, 0)),
            pl.BlockSpec((1, 1, KEY_W), lambda b, c, pt: (b, 0, 0)),
            pl.BlockSpec(memory_space=pl.ANY),
            pl.BlockSpec(memory_space=pl.ANY),
        ],
        out_specs=pl.BlockSpec((1, MLA_HEADS, KV_LORA), lambda b, c, pt: (b, 0, 0)),
        scratch_shapes=[pltpu.VMEM((2, chunk_keys, KEY_W), F32),
                        pltpu.VMEM((2, chunk_keys, MLA_ROPE), F32),
                        pltpu.SemaphoreType.DMA((2, 2)),
                        pltpu.VMEM((MLA_HEADS, 1), F32),
                        pltpu.VMEM((MLA_HEADS, 1), F32),
                        pltpu.VMEM((MLA_HEADS, KV_LORA), F32)],
    )
    return pl.pallas_call(
        _paged_kernel,
        grid_spec=grid_spec,
        out_shape=jax.ShapeDtypeStruct((n, MLA_HEADS, KV_LORA), F32),
        compiler_params=pltpu.CompilerParams(dimension_semantics=("arbitrary", "arbitrary")),
        name="sample_paged_attention",
    )(page_table, q_abs, k_new, cache_ckv, cache_kpe)


def _back_kernel(h_ref, oatt_ref, oret_ref, p_ref,
                 gmix_ref, wgate_ref, wba_ref, wbr_ref, wout_ref,
                 g2_ref, wg_ref, wu_ref, wd_ref, gple_ref, wpg_ref, wpp_ref, gfin_ref, y_ref):
    h = h_ref[...]
    un = _rms(h, gmix_ref[...]).astype(BF16)
    gates = _dot(un, wgate_ref[...])
    rg = gates[:, :_RET_W]
    ga = gates[:, _RET_W:_RET_W + D_MODEL]
    gr = gates[:, _RET_W + D_MODEL:]
    o_ret = (rg * jax.nn.sigmoid(rg) * oret_ref[...]).astype(BF16)
    merged = (jax.nn.sigmoid(ga) * _dot(oatt_ref[...], wba_ref[...])
              + jax.nn.sigmoid(gr) * _dot(o_ret, wbr_ref[...]))
    h = h + _dot(merged.astype(BF16), wout_ref[...])
    h = _swiglu_half(h, g2_ref[...], wg_ref[...], wu_ref[...], wd_ref[...])
    gate = jax.nn.sigmoid(_dot(_rms(h, gple_ref[...]).astype(BF16), wpg_ref[...]))
    h = h + gate * _dot(p_ref[...].astype(BF16), wpp_ref[...])
    y_ref[...] = _rms(h, gfin_ref[...])


def _back_stage(h, o_att, o_ret, p_emb, w, tile):
    n = h.shape[0]
    row = lambda width: pl.BlockSpec((tile, width), lambda i: (i, 0))
    weights = [w['mix_norm'], w['w_in_gates'], w['w_branch_att'], w['w_branch_ret'], w['w_out'],
               w['ffn2_norm'], w['ffn2_w_gate'], w['ffn2_w_up'], w['ffn2_w_down'],
               w['ple_norm'], w['w_ple_gate'], w['w_ple_proj'], w['final_norm']]
    return pl.pallas_call(
        _back_kernel,
        grid=(n // tile,),
        in_specs=[row(D_MODEL), row(MLA_HEADS * MLA_V), row(_RET_W), row(PLE_DIM)]
        + [_const_spec(a.shape) for a in weights],
        out_specs=row(D_MODEL),
        out_shape=jax.ShapeDtypeStruct((n, D_MODEL), F32),
        compiler_params=pltpu.CompilerParams(dimension_semantics=("arbitrary",),
                                             vmem_limit_bytes=VMEM_LIMIT),
        name="back_stage",
    )(h, o_att, o_ret, p_emb, *weights)


def _rope_tables(pos):
    pos = pos.astype(F32)[:, None]
    n = pos.shape[0]
    inv_m = ROPE_THETA ** (-jnp.arange(ROPE_HALF, dtype=F32) / ROPE_HALF)
    cos_m, sin_m = jnp.cos(pos * inv_m[None, :]), jnp.sin(pos * inv_m[None, :])
    z = lambda width: jnp.zeros((n, width), F32)
    tail = HEAD_PAD - ROPE_OFF - MLA_ROPE
    cm = jnp.concatenate([jnp.ones((n, ROPE_OFF), F32), cos_m, cos_m, z(tail)], axis=1)
    s1 = jnp.concatenate([z(ROPE_OFF), -sin_m, z(ROPE_HALF), z(tail)], axis=1)
    s2 = jnp.concatenate([z(ROPE_OFF), z(ROPE_HALF), sin_m, z(tail)], axis=1)
    half = RET_DK // 2
    inv_r = ROPE_THETA ** (-jnp.arange(half, dtype=F32) / half)
    cos_r, sin_r = jnp.cos(pos * inv_r[None, :]), jnp.sin(pos * inv_r[None, :])
    cr = jnp.concatenate([cos_r, cos_r], axis=1)
    sr = jnp.concatenate([-sin_r, sin_r], axis=1)
    return cm, s1, s2, cr, sr


def _layer_weights(i, ffn1_norm, ffn1_w_gate, ffn1_w_up, ffn1_w_down, mix_norm, w_in, q_a_norm, w_q_b,
                   kv_a_norm, w_kv_b, ret_norm, w_branch_att, w_branch_ret, w_out,
                   ffn2_norm, ffn2_w_gate, ffn2_w_up, ffn2_w_down, ple_norm, w_ple_gate, w_ple_proj,
                   final_norm):
    vec = lambda a: a.reshape(1, -1)
    bf = lambda a: a.astype(BF16)
    win = w_in[i]
    zcols = lambda width: jnp.zeros((D_MODEL, width), F32)
    w_in_front = jnp.concatenate(
        [win[:, :_OFF_KPE], zcols(ROPE_OFF), win[:, _OFF_KPE:_OFF_RQ],
         zcols(HEAD_PAD - ROPE_OFF - MLA_ROPE), win[:, _OFF_RQ:_OFF_RG]], axis=1)
    wqb = w_q_b[i].reshape(Q_LORA, MLA_HEADS, MLA_NOPE + MLA_ROPE)
    wqb = jnp.pad(wqb, ((0, 0), (0, 0), (0, HEAD_PAD - MLA_NOPE - MLA_ROPE)))
    wkv = w_kv_b[i].reshape(KV_LORA, MLA_HEADS, MLA_NOPE + MLA_V)
    w_uk, w_uv = wkv[..., :MLA_NOPE], wkv[..., MLA_NOPE:]
    w_k = jnp.pad(w_uk, ((0, 0), (0, 0), (0, HEAD_PAD - MLA_NOPE)))
    w_absorb = jnp.zeros((MLA_HEADS, HEAD_PAD, KEY_W), F32)
    w_absorb = w_absorb.at[:, :MLA_NOPE, :KV_LORA].set(jnp.transpose(w_uk, (1, 2, 0)))
    w_absorb = w_absorb.at[:, MLA_NOPE:MLA_NOPE + MLA_ROPE, KV_LORA:KV_LORA + MLA_ROPE].set(
        jnp.eye(MLA_ROPE, dtype=F32)[None])
    w_unabsorb = jnp.zeros((MLA_HEADS, KV_LORA, MLA_HEADS * MLA_V), F32)
    for hd in range(MLA_HEADS):
        w_unabsorb = w_unabsorb.at[hd, :, hd * MLA_V:(hd + 1) * MLA_V].set(w_uv[:, hd, :])
    return {
        'ffn1_norm': vec(ffn1_norm[i]), 'ffn1_w_gate': bf(ffn1_w_gate[i]), 'ffn1_w_up': bf(ffn1_w_up[i]),
        'ffn1_w_down': bf(ffn1_w_down[i]), 'mix_norm': vec(mix_norm[i]),
        'w_in_front': bf(w_in_front), 'w_in_gates': bf(win[:, _OFF_RG:]),
        'q_a_norm': vec(q_a_norm[i]), 'w_q_b': bf(wqb.reshape(Q_LORA, MLA_HEADS * HEAD_PAD)),
        'kv_a_norm': vec(kv_a_norm[i]), 'w_k': bf(w_k.reshape(KV_LORA, MLA_HEADS * HEAD_PAD)),
        'w_v': bf(w_uv.reshape(KV_LORA, MLA_HEADS * MLA_V)),
        'w_absorb': bf(w_absorb), 'w_unabsorb': bf(w_unabsorb),
        'ret_norm': ret_norm[i],
        'w_branch_att': bf(w_branch_att[i]), 'w_branch_ret': bf(w_branch_ret[i]), 'w_out': bf(w_out[i]),
        'ffn2_norm': vec(ffn2_norm[i]), 'ffn2_w_gate': bf(ffn2_w_gate[i]), 'ffn2_w_up': bf(ffn2_w_up[i]),
        'ffn2_w_down': bf(ffn2_w_down[i]), 'ple_norm': vec(ple_norm[i]),
        'w_ple_gate': bf(w_ple_gate[i]), 'w_ple_proj': bf(w_ple_proj[i]),
        'final_norm': vec(final_norm),
    }


def kernel(x_prompt, x_sample, cache_ckv, cache_kpe, state_ret, page_table, p_prompt, p_sample, ffn1_norm, ffn1_w_gate, ffn1_w_up, ffn1_w_down, mix_norm, w_in, q_a_norm, w_q_b, kv_a_norm, w_kv_b, ret_norm, w_branch_att, w_branch_ret, w_out, ffn2_norm, ffn2_w_gate, ffn2_w_up, ffn2_w_down, ple_norm, w_ple_gate, w_ple_proj, final_norm):
    batch, seq, _ = x_prompt.shape
    n_dec, dec_seq, _ = x_sample.shape
    depth = w_in.shape[0]
    assert dec_seq == 1 and depth == 1
    n_past = page_table.shape[1] * PAGE_SIZE

    w = _layer_weights(0, ffn1_norm, ffn1_w_gate, ffn1_w_up, ffn1_w_down, mix_norm, w_in, q_a_norm,
                       w_q_b, kv_a_norm, w_kv_b, ret_norm, w_branch_att, w_branch_ret, w_out,
                       ffn2_norm, ffn2_w_gate, ffn2_w_up, ffn2_w_down, ple_norm, w_ple_gate,
                       w_ple_proj, final_norm)

    tabs_p = _rope_tables(jnp.arange(seq, dtype=jnp.int32))
    (h_p, q_p, k_p, v_p, ckv_p, kpe_p, qr_p, kr_p, rv_p) = _front_stage(
        x_prompt.reshape(batch * seq, D_MODEL), tabs_p, w, TOKEN_TILE, seq // TOKEN_TILE)
    bt = lambda a: a.reshape(batch, seq, a.shape[-1])
    o_att_p = _prompt_attention(bt(q_p), bt(k_p), bt(v_p))
    o_ret_p, ret_p = _prompt_retention(bt(qr_p), bt(kr_p), bt(rv_p), w['ret_norm'])
    y_p = _back_stage(h_p, o_att_p.reshape(batch * seq, -1), o_ret_p.reshape(batch * seq, -1),
                      p_prompt[0].reshape(batch * seq, PLE_DIM), w, TOKEN_TILE)

    tabs_s = tuple(jnp.broadcast_to(t, (n_dec, LANES))
                   for t in _rope_tables(jnp.full((1,), n_past, jnp.int32)))
    (h_s, q_s, _, _, ckv_s, kpe_s, qr_s, kr_s, rv_s) = _front_stage(
        x_sample.reshape(n_dec, D_MODEL), tabs_s, w, n_dec, 1)
    q_abs = jnp.transpose(_absorb_q(q_s, w['w_absorb']), (1, 0, 2))
    k_new = jnp.concatenate([ckv_s, kpe_s[:, ROPE_OFF:ROPE_OFF + MLA_ROPE],
                             jnp.zeros((n_dec, KEY_W - KV_LORA - MLA_ROPE), F32)], axis=1)
    o_lat = _paged_attention(page_table, q_abs, k_new.reshape(n_dec, 1, KEY_W),
                             cache_ckv[0], cache_kpe[0])
    o_att_s = _unabsorb(jnp.transpose(o_lat, (1, 0, 2)), w['w_unabsorb'])
    o_ret_s, ret_s = _sample_retention(qr_s, kr_s, rv_s, state_ret[0], w['ret_norm'])
    y_s = _back_stage(h_s, o_att_s, o_ret_s, p_sample[0].reshape(n_dec, PLE_DIM), w, n_dec)

    kpe_cols = slice(ROPE_OFF, ROPE_OFF + MLA_ROPE)
    return (y_p.reshape(batch, seq, D_MODEL),
            y_s.reshape(n_dec, 1, D_MODEL),
            ckv_p.reshape(1, batch, seq, KV_LORA),
            kpe_p[:, kpe_cols].reshape(1, batch, seq, MLA_ROPE),
            ret_p[None],
            ckv_s.reshape(1, n_dec, 1, KV_LORA),
            kpe_s[:, kpe_cols].reshape(1, n_dec, 1, MLA_ROPE),
            ret_s[None])
```

```python
import jax
import jax.numpy as jnp
import numpy as np
from jax import lax
from jax.experimental import pallas as pl
from jax.experimental.pallas import tpu as pltpu

F32 = jnp.float32
BF16 = jnp.bfloat16

D_MODEL = 1024
D_FF = 2816
PLE_DIM = 256
MLA_HEADS = 8
MLA_NOPE = 64
MLA_ROPE = 32
MLA_V = 64
Q_LORA = 256
KV_LORA = 128
RET_HEADS = 4
RET_DK = 128
RET_DV = 128
PAGE_SIZE = 128
ROPE_THETA = 10000.0
EPS = 1e-6

LANES = 128
HEAD_PAD = LANES
ROPE_OFF = MLA_NOPE
ROPE_HALF = MLA_ROPE // 2
SOFTMAX_SCALE = (MLA_NOPE + MLA_ROPE) ** -0.5
LOG2E = 1.4426950408889634
Q_SCALE = SOFTMAX_SCALE * LOG2E
NEG_BIG = -1e30

TOKEN_TILE = 256
ATTN_TILE = 512
RET_CHUNK = 128
PAGES_PER_STEP = 32
PAGE_GROUPS = 4
RET_SEQS_PER_STEP = 8
KEY_W = 2 * LANES
VMEM_LIMIT = 60 * 1024 * 1024

_OFF_CQ = 0
_OFF_CKV = _OFF_CQ + Q_LORA
_OFF_KPE = _OFF_CKV + KV_LORA
_OFF_RQ = _OFF_KPE + MLA_ROPE
_OFF_RK = _OFF_RQ + RET_HEADS * RET_DK
_OFF_RV = _OFF_RK + RET_HEADS * RET_DK
_OFF_RG = _OFF_RV + RET_HEADS * RET_DV
_RET_W = RET_HEADS * RET_DK
_ATT_W = MLA_HEADS * MLA_V
V_ROWS = MLA_V + 16
_VT_ROWS = MLA_HEADS * V_ROWS


def _rms(x, g):
    return x * lax.rsqrt(jnp.mean(x * x, axis=-1, keepdims=True) + EPS) * g


def _dot(a, b):
    return jnp.dot(a, b, preferred_element_type=F32)


def _dot_nt(a, b):
    return lax.dot_general(a, b, (((1,), (1,)), ((), ())), preferred_element_type=F32)


def _swiglu_half(x, g, wg, wu, wd):
    xn = _rms(x, g).astype(BF16)
    gate = _dot(xn, wg)
    up = _dot(xn, wu)
    act = (gate * jax.nn.sigmoid(gate) * up).astype(BF16)
    return x + 0.5 * _dot(act, wd)


def _const_spec(shape):
    n = len(shape)
    return pl.BlockSpec(shape, lambda *_: (0,) * n, pipeline_mode=pl.Buffered(1))


def _front_kernel(x_ref, cm_ref, s1_ref, s2_ref, cr_ref, sr_ref,
                  g1_ref, wg_ref, wu_ref, wd_ref, gmix_ref, win_ref,
                  gqa_ref, wqb_ref, gkv_ref, wk_ref, wvt_ref,
                  h_ref, q_ref, k_ref, vt_ref, ckv_ref, kpe_ref, qr_ref, kr_ref, rv_ref):
    x = x_ref[...]
    h = _swiglu_half(x, g1_ref[...], wg_ref[...], wu_ref[...], wd_ref[...])
    h_ref[...] = h
    un = _rms(h, gmix_ref[...]).astype(BF16)
    z = _dot(un, win_ref[...])

    cm, s1, s2 = cm_ref[...], s1_ref[...], s2_ref[...]

    def mla_rope(t):
        return (t * cm + pltpu.roll(t, LANES - ROPE_HALF, 1) * s1
                + pltpu.roll(t, ROPE_HALF, 1) * s2)

    cq = z[:, _OFF_CQ:_OFF_CQ + Q_LORA]
    q = _dot(_rms(cq, gqa_ref[...]).astype(BF16), wqb_ref[...])
    c_kv = _rms(z[:, Q_LORA:Q_LORA + KV_LORA], gkv_ref[...])
    ckv_ref[...] = c_kv
    c_kv_b = c_kv.astype(BF16)
    k_nope = _dot(c_kv_b, wk_ref[...])
    vt = _dot_nt(wvt_ref[...], c_kv_b)
    sum_row = lax.broadcasted_iota(jnp.int32, vt.shape, 0) % V_ROWS >= MLA_V
    vt_ref[...] = jnp.where(sum_row, 1.0, vt).astype(BF16)
    k_pe = mla_rope(z[:, Q_LORA + KV_LORA:Q_LORA + KV_LORA + LANES])
    kpe_ref[...] = k_pe
    for hd in range(MLA_HEADS):
        sl = slice(hd * HEAD_PAD, (hd + 1) * HEAD_PAD)
        q_ref[:, sl] = (mla_rope(q[:, sl]) * Q_SCALE).astype(BF16)
        k_ref[:, sl] = (k_nope[:, sl] + k_pe).astype(BF16)

    cr, sr = cr_ref[...], sr_ref[...]
    base = Q_LORA + KV_LORA + LANES
    for hd in range(RET_HEADS):
        sl = slice(hd * RET_DK, (hd + 1) * RET_DK)
        rq = z[:, base + hd * RET_DK:base + (hd + 1) * RET_DK]
        rk = z[:, base + _RET_W + hd * RET_DK:base + _RET_W + (hd + 1) * RET_DK]
        qr_ref[:, sl] = rq * cr + pltpu.roll(rq, RET_DK // 2, 1) * sr
        kr_ref[:, sl] = (rk * cr + pltpu.roll(rk, RET_DK // 2, 1) * sr) * (RET_DK ** -0.5)
    rv_ref[...] = z[:, base + 2 * _RET_W:base + 3 * _RET_W]


def _front_stage(x, tables, w, tile, table_tiles):
    n = x.shape[0]
    row = lambda width: pl.BlockSpec((tile, width), lambda i: (i, 0))
    tab = pl.BlockSpec((tile, LANES), lambda i: (i % table_tiles, 0))
    weights = [w['ffn1_norm'], w['ffn1_w_gate'], w['ffn1_w_up'], w['ffn1_w_down'], w['mix_norm'],
               w['w_in_front'], w['q_a_norm'], w['w_q_b'], w['kv_a_norm'], w['w_k'], w['w_vt']]
    outs = [((n, D_MODEL), F32, row(D_MODEL)),
            ((n, MLA_HEADS * HEAD_PAD), BF16, row(MLA_HEADS * HEAD_PAD)),
            ((n, MLA_HEADS * HEAD_PAD), BF16, row(MLA_HEADS * HEAD_PAD)),
            ((_VT_ROWS, n), BF16, pl.BlockSpec((_VT_ROWS, tile), lambda i: (0, i))),
            ((n, KV_LORA), F32, row(KV_LORA)),
            ((n, LANES), F32, row(LANES)),
            ((n, _RET_W), F32, row(_RET_W)),
            ((n, _RET_W), F32, row(_RET_W)),
            ((n, _RET_W), F32, row(_RET_W))]
    return pl.pallas_call(
        _front_kernel,
        grid=(n // tile,),
        in_specs=[row(D_MODEL)] + [tab] * 5 + [_const_spec(a.shape) for a in weights],
        out_specs=[spec for _, _, spec in outs],
        out_shape=[jax.ShapeDtypeStruct(shape, dt) for shape, dt, _ in outs],
        compiler_params=pltpu.CompilerParams(dimension_semantics=("arbitrary",),
                                             vmem_limit_bytes=VMEM_LIMIT),
        name="front_stage",
    )(x, *tables, *weights)


def _attn_kernel(qi_ref, ki_ref, q_ref, k_ref, vt_ref, o_ref, m_sc, acc_sc):
    t = pl.program_id(1)
    qi = qi_ref[t]
    ki = ki_ref[t]

    @pl.when(ki == 0)
    def _():
        m_sc[...] = jnp.full(m_sc.shape, NEG_BIG, F32)
        acc_sc[...] = jnp.zeros(acc_sc.shape, F32)

    def scores(hd):
        qh = q_ref[:, hd * HEAD_PAD:(hd + 1) * HEAD_PAD]
        kh = k_ref[:, hd * HEAD_PAD:(hd + 1) * HEAD_PAD]
        return _dot_nt(kh, qh)

    def update(masked):
        st_next = scores(0)
        for hd in range(MLA_HEADS):
            st = st_next
            if hd + 1 < MLA_HEADS:
                st_next = scores(hd + 1)
            if masked:
                key = lax.broadcasted_iota(jnp.int32, st.shape, 0)
                qry = lax.broadcasted_iota(jnp.int32, st.shape, 1)
                st = jnp.where(key <= qry, st, NEG_BIG)
            m_prev = m_sc[hd]
            m_new = jnp.maximum(m_prev, jnp.max(st, axis=0, keepdims=True))
            alpha = jnp.exp2(m_prev - m_new)
            p = jnp.exp2(st - m_new)
            acc_sc[hd] = alpha * acc_sc[hd] + _dot(vt_ref[hd * V_ROWS:(hd + 1) * V_ROWS, :],
                                                    p.astype(BF16))
            m_sc[hd] = m_new

    @pl.when(ki < qi)
    def _():
        update(False)

    @pl.when(ki == qi)
    def _():
        update(True)
        acc = acc_sc[...]
        o_t = (acc[:, :MLA_V, :] / acc[:, MLA_V:MLA_V + 1, :]).reshape(_ATT_W, ATTN_TILE)
        o_ref[...] = jnp.transpose(o_t).astype(o_ref.dtype)


def _prompt_attention(q, k, vt, batch, seq):
    nt = seq // ATTN_TILE
    qi_list = np.array([i for i in range(nt) for _ in range(i + 1)], np.int32)
    ki_list = np.array([j for i in range(nt) for j in range(i + 1)], np.int32)
    qk_w = MLA_HEADS * HEAD_PAD
    grid_spec = pltpu.PrefetchScalarGridSpec(
        num_scalar_prefetch=2,
        grid=(batch, len(qi_list)),
        in_specs=[
            pl.BlockSpec((ATTN_TILE, qk_w), lambda bb, s, qi, ki: (bb * nt + qi[s], 0)),
            pl.BlockSpec((ATTN_TILE, qk_w), lambda bb, s, qi, ki: (bb * nt + ki[s], 0)),
            pl.BlockSpec((_VT_ROWS, ATTN_TILE), lambda bb, s, qi, ki: (0, bb * nt + ki[s])),
        ],
        out_specs=pl.BlockSpec((ATTN_TILE, _ATT_W), lambda bb, s, qi, ki: (bb * nt + qi[s], 0)),
        scratch_shapes=[pltpu.VMEM((MLA_HEADS, 1, ATTN_TILE), F32),
                        pltpu.VMEM((MLA_HEADS, V_ROWS, ATTN_TILE), F32)],
    )
    return pl.pallas_call(
        _attn_kernel,
        grid_spec=grid_spec,
        out_shape=jax.ShapeDtypeStruct((batch * seq, _ATT_W), BF16),
        compiler_params=pltpu.CompilerParams(dimension_semantics=("arbitrary", "arbitrary"),
                                             vmem_limit_bytes=VMEM_LIMIT),
        name="prompt_attention",
    )(jnp.asarray(qi_list), jnp.asarray(ki_list), q, k, vt)


def _head_layernorm(o, g):
    mu = jnp.mean(o, axis=-1, keepdims=True)
    d = o - mu
    var = jnp.mean(d * d, axis=-1, keepdims=True)
    return d * lax.rsqrt(var + EPS) * g


def _ret_prompt_kernel(q_ref, k_ref, v_ref, dec_ref, qd_ref, kd_ref, gc_ref, g_ref,
                       o_ref, s_out_ref, s_sc):
    c = pl.program_id(1)

    @pl.when(c == 0)
    def _():
        s_sc[...] = jnp.zeros(s_sc.shape, F32)

    for hd in range(RET_HEADS):
        sl = slice(hd * RET_DK, (hd + 1) * RET_DK)
        q = q_ref[0, :, sl]
        k = k_ref[0, :, sl]
        vb = v_ref[0, :, sl].astype(BF16)
        qb = q.astype(BF16)
        state = s_sc[hd]
        scores = _dot_nt(qb, k.astype(BF16)) * dec_ref[hd]
        inner = _dot(scores.astype(BF16), vb)
        cross = _dot(qb, state.astype(BF16)) * qd_ref[hd]
        o_ref[0, :, sl] = _head_layernorm(inner + cross, g_ref[hd:hd + 1, :])
        k_dec_t = jnp.transpose(k * kd_ref[hd]).astype(BF16)
        s_sc[hd] = gc_ref[hd:hd + 1, :] * state + _dot(k_dec_t, vb)

    @pl.when(c == pl.num_programs(1) - 1)
    def _():
        s_out_ref[0] = s_sc[...]


def _log_gamma():
    return jnp.log1p(-jnp.exp2(-5.0 - jnp.arange(RET_HEADS, dtype=F32)))


def _retention_consts(chunk):
    log_gamma = _log_gamma()
    idx = jnp.arange(chunk, dtype=F32)
    diff = idx[:, None] - idx[None, :]
    decay = jnp.where(diff[None] >= 0,
                      jnp.exp(log_gamma[:, None, None] * jnp.maximum(diff, 0.0)[None]), 0.0)
    q_decay = jnp.exp(log_gamma[:, None] * (idx + 1.0)[None, :])
    k_decay = jnp.exp(log_gamma[:, None] * (chunk - 1.0 - idx)[None, :])
    chunk_decay = jnp.exp(log_gamma * chunk)
    lane = lambda a: jnp.broadcast_to(a[..., None], a.shape + (LANES,))
    return decay, lane(q_decay), lane(k_decay), lane(chunk_decay)


def _prompt_retention(qr, kr, rv, ret_norm):
    b, t, _ = qr.shape
    decay, q_decay, k_decay, chunk_decay = _retention_consts(RET_CHUNK)
    seq = pl.BlockSpec((1, RET_CHUNK, _RET_W), lambda bb, c: (bb, c, 0))
    return pl.pallas_call(
        _ret_prompt_kernel,
        grid=(b, t // RET_CHUNK),
        in_specs=[seq, seq, seq, _const_spec(decay.shape), _const_spec(q_decay.shape),
                  _const_spec(k_decay.shape), _const_spec(chunk_decay.shape),
                  _const_spec(ret_norm.shape)],
        out_specs=[seq, pl.BlockSpec((1, RET_HEADS, RET_DK, RET_DV), lambda bb, c: (bb, 0, 0, 0))],
        out_shape=[jax.ShapeDtypeStruct((b, t, _RET_W), F32),
                   jax.ShapeDtypeStruct((b, RET_HEADS, RET_DK, RET_DV), F32)],
        scratch_shapes=[pltpu.VMEM((RET_HEADS, RET_DK, RET_DV), F32)],
        compiler_params=pltpu.CompilerParams(dimension_semantics=("arbitrary", "arbitrary")),
        name="prompt_retention",
    )(qr, kr, rv, decay, q_decay, k_decay, chunk_decay, ret_norm)


def _ret_sample_kernel(q_ref, k_ref, v_ref, s_ref, gam_ref, g_ref, o_ref, s_out_ref):
    rows = lax.broadcasted_iota(jnp.int32, (RET_DK, RET_DK), 0)
    cols = lax.broadcasted_iota(jnp.int32, (RET_DK, RET_DK), 1)
    eye = rows == cols

    def column(r):
        return jnp.sum(jnp.where(eye, jnp.broadcast_to(r, (RET_DK, RET_DK)), 0.0),
                       axis=1, keepdims=True)

    for i in range(RET_SEQS_PER_STEP):
        for hd in range(RET_HEADS):
            sl = slice(hd * RET_DK, (hd + 1) * RET_DK)
            q = q_ref[i, :, sl]
            k = k_ref[i, :, sl]
            v = v_ref[i, :, sl]
            gam = gam_ref[hd:hd + 1, :]
            state = s_ref[0, i, hd]
            cross = jnp.sum(column(q) * state, axis=0, keepdims=True) * gam
            inner = jnp.sum(q * k, axis=-1, keepdims=True) * v
            o_ref[i, :, sl] = _head_layernorm(inner + cross, g_ref[hd:hd + 1, :])
            s_out_ref[0, i, hd] = gam * state + column(k) * v


def _sample_retention(qr, kr, rv, state, ret_norm):
    n = qr.shape[0]
    gam = jnp.broadcast_to(jnp.exp(_log_gamma() * 1.0)[:, None], (RET_HEADS, LANES))
    g = RET_SEQS_PER_STEP
    tok = pl.BlockSpec((g, 1, _RET_W), lambda i: (i, 0, 0))
    st = pl.BlockSpec((1, g, RET_HEADS, RET_DK, RET_DV), lambda i: (0, i, 0, 0, 0))
    r3 = lambda a: a.reshape(n, 1, _RET_W)
    o, s_new = pl.pallas_call(
        _ret_sample_kernel,
        grid=(n // g,),
        in_specs=[tok, tok, tok, st, _const_spec(gam.shape), _const_spec(ret_norm.shape)],
        out_specs=[tok, st],
        out_shape=[jax.ShapeDtypeStruct((n, 1, _RET_W), F32),
                   jax.ShapeDtypeStruct(state.shape, F32)],
        compiler_params=pltpu.CompilerParams(dimension_semantics=("arbitrary",)),
        name="sample_retention",
    )(r3(qr), r3(kr), r3(rv), state, gam, ret_norm)
    return o.reshape(n, _RET_W), s_new


def _absorb_q_kernel(q_ref, w_ref, o_ref):
    for hd in range(MLA_HEADS):
        o_ref[hd] = _dot(q_ref[:, hd * HEAD_PAD:(hd + 1) * HEAD_PAD], w_ref[hd])


def _absorb_q(q, w_absorb):
    n = q.shape[0]
    return pl.pallas_call(
        _absorb_q_kernel,
        out_shape=jax.ShapeDtypeStruct((MLA_HEADS, n, KEY_W), F32),
        name="sample_absorb_q",
    )(q, w_absorb)


def _unabsorb_kernel(o_ref, w_ref, out_ref):
    acc = _dot(o_ref[0].astype(BF16), w_ref[0])
    for hd in range(1, MLA_HEADS):
        acc = acc + _dot(o_ref[hd].astype(BF16), w_ref[hd])
    out_ref[...] = acc.astype(out_ref.dtype)


def _unabsorb(o_lat, w_unabsorb):
    n = o_lat.shape[1]
    return pl.pallas_call(
        _unabsorb_kernel,
        out_shape=jax.ShapeDtypeStruct((n, _ATT_W), BF16),
        name="sample_unabsorb",
    )(o_lat, w_unabsorb)


def _paged_kernel(pt_ref, q_ref, knew_ref, ckv_hbm, kpe_hbm, o_ref, buf, kpe_buf, sem,
                  m_sc, l_sc, acc_sc):
    b = pl.program_id(0)
    c = pl.program_id(1)
    n_chunks = pl.num_programs(1)
    step = b * n_chunks + c
    total = pl.num_programs(0) * n_chunks
    slot = step % 2
    chunk_keys = PAGES_PER_STEP * PAGE_SIZE
    group_keys = chunk_keys // PAGE_GROUPS

    def page_copies(bb, cc, sl):
        out = []
        for j in range(PAGES_PER_STEP):
            pg = pt_ref[bb, cc * PAGES_PER_STEP + j]
            rows = pl.ds(j * PAGE_SIZE, PAGE_SIZE)
            out.append(pltpu.make_async_copy(ckv_hbm.at[0, pg], buf.at[sl, rows, pl.ds(0, KV_LORA)],
                                             sem.at[sl, 0]))
            out.append(pltpu.make_async_copy(kpe_hbm.at[0, pg], kpe_buf.at[sl, rows], sem.at[sl, 1]))
        return out

    @pl.when(step == 0)
    def _():
        buf[:, :, KV_LORA:] = jnp.zeros((2, chunk_keys, KEY_W - KV_LORA), F32)
        for cp in page_copies(0, 0, 0):
            cp.start()

    @pl.when(step + 1 < total)
    def _():
        nxt = step + 1
        for cp in page_copies(nxt // n_chunks, nxt % n_chunks, 1 - slot):
            cp.start()

    for cp in page_copies(b, c, slot):
        cp.wait()

    @pl.when(c == 0)
    def _():
        m_sc[...] = jnp.full(m_sc.shape, NEG_BIG, F32)
        l_sc[...] = jnp.zeros(l_sc.shape, F32)
        acc_sc[...] = jnp.zeros(acc_sc.shape, F32)

    q = q_ref[0]
    buf[slot, :, KV_LORA:KV_LORA + MLA_ROPE] = kpe_buf[slot]
    parts = []
    for g in range(PAGE_GROUPS):
        rows = slice(g * group_keys, (g + 1) * group_keys)
        s = _dot_nt(q, buf[slot, rows, :])
        m_g = jnp.max(s, axis=-1, keepdims=True)
        p = jnp.exp2(s - m_g)
        l_g = jnp.sum(p, axis=-1, keepdims=True)
        parts.append((m_g, l_g, _dot(p, buf[slot, rows, 0:KV_LORA])))

    m_prev = m_sc[...]
    m_new = m_prev
    for m_g, _, _ in parts:
        m_new = jnp.maximum(m_new, m_g)
    a_prev = jnp.exp2(m_prev - m_new)
    l_new = a_prev * l_sc[...]
    acc_new = a_prev * acc_sc[...]
    for m_g, l_g, o_g in parts:
        a_g = jnp.exp2(m_g - m_new)
        l_new = l_new + a_g * l_g
        acc_new = acc_new + a_g * o_g
    m_sc[...] = m_new
    l_sc[...] = l_new
    acc_sc[...] = acc_new

    @pl.when(c == n_chunks - 1)
    def _():
        kn = knew_ref[0]
        s_self = jnp.sum(q * kn, axis=-1, keepdims=True)
        m_fin = jnp.maximum(m_new, s_self)
        a = jnp.exp2(m_new - m_fin)
        p_self = jnp.exp2(s_self - m_fin)
        l_fin = a * l_new + p_self
        o_ref[0] = (a * acc_new + p_self * kn[:, 0:KV_LORA]) / l_fin


def _paged_attention(page_table, q_abs, k_new, cache_ckv, cache_kpe):
    n, n_pages = page_table.shape
    n_chunks = n_pages // PAGES_PER_STEP
    chunk_keys = PAGES_PER_STEP * PAGE_SIZE
    grid_spec = pltpu.PrefetchScalarGridSpec(
        num_scalar_prefetch=1,
        grid=(n, n_chunks),
        in_specs=[
            pl.BlockSpec((1, MLA_HEADS, KEY_W), lambda b, c, pt: (b, 0, 0)),
            pl.BlockSpec((1, 1, KEY_W), lambda b, c, pt: (b, 0, 0)),
            pl.BlockSpec(memory_space=pl.ANY),
            pl.BlockSpec(memory_space=pl.ANY),
        ],
        out_specs=pl.BlockSpec((1, MLA_HEADS, KV_LORA), lambda b, c, pt: (b, 0, 0)),
        scratch_shapes=[pltpu.VMEM((2, chunk_keys, KEY_W), F32),
                        pltpu.VMEM((2, chunk_keys, MLA_ROPE), F32),
                        pltpu.SemaphoreType.DMA((2, 2)),
                        pltpu.VMEM((MLA_HEADS, 1), F32),
                        pltpu.VMEM((MLA_HEADS, 1), F32),
                        pltpu.VMEM((MLA_HEADS, KV_LORA), F32)],
    )
    return pl.pallas_call(
        _paged_kernel,
        grid_spec=grid_spec,
        out_shape=jax.ShapeDtypeStruct((n, MLA_HEADS, KV_LORA), F32),
        compiler_params=pltpu.CompilerParams(dimension_semantics=("arbitrary", "arbitrary"),
                                             vmem_limit_bytes=VMEM_LIMIT),
        name="sample_paged_attention",
    )(page_table, q_abs, k_new, cache_ckv, cache_kpe)


def _back_kernel(h_ref, oatt_ref, oret_ref, p_ref,
                 gmix_ref, wgate_ref, wba_ref, wbr_ref, wout_ref,
                 g2_ref, wg_ref, wu_ref, wd_ref, gple_ref, wpg_ref, wpp_ref, gfin_ref, y_ref):
    h = h_ref[...]
    un = _rms(h, gmix_ref[...]).astype(BF16)
    gates = _dot(un, wgate_ref[...])
    rg = gates[:, :_RET_W]
    ga = gates[:, _RET_W:_RET_W + D_MODEL]
    gr = gates[:, _RET_W + D_MODEL:]
    o_ret = (rg * jax.nn.sigmoid(rg) * oret_ref[...]).astype(BF16)
    merged = (jax.nn.sigmoid(ga) * _dot(oatt_ref[...], wba_ref[...])
              + jax.nn.sigmoid(gr) * _dot(o_ret, wbr_ref[...]))
    h = h + _dot(merged.astype(BF16), wout_ref[...])
    h = _swiglu_half(h, g2_ref[...], wg_ref[...], wu_ref[...], wd_ref[...])
    gate = jax.nn.sigmoid(_dot(_rms(h, gple_ref[...]).astype(BF16), wpg_ref[...]))
    h = h + gate * _dot(p_ref[...].astype(BF16), wpp_ref[...])
    y_ref[...] = _rms(h, gfin_ref[...])


def _back_stage(h, o_att, o_ret, p_emb, w, tile):
    n = h.shape[0]
    row = lambda width: pl.BlockSpec((tile, width), lambda i: (i, 0))
    weights = [w['mix_norm'], w['w_in_gates'], w['w_branch_att'], w['w_branch_ret'], w['w_out'],
               w['ffn2_norm'], w['ffn2_w_gate'], w['ffn2_w_up'], w['ffn2_w_down'],
               w['ple_norm'], w['w_ple_gate'], w['w_ple_proj'], w['final_norm']]
    return pl.pallas_call(
        _back_kernel,
        grid=(n // tile,),
        in_specs=[row(D_MODEL), row(_ATT_W), row(_RET_W), row(PLE_DIM)]
        + [_const_spec(a.shape) for a in weights],
        out_specs=row(D_MODEL),
        out_shape=jax.ShapeDtypeStruct((n, D_MODEL), F32),
        compiler_params=pltpu.CompilerParams(dimension_semantics=("arbitrary",),
                                             vmem_limit_bytes=VMEM_LIMIT),
        name="back_stage",
    )(h, o_att, o_ret, p_emb, *weights)


def _rope_tables(pos):
    pos = pos.astype(F32)[:, None]
    n = pos.shape[0]
    inv_m = ROPE_THETA ** (-jnp.arange(ROPE_HALF, dtype=F32) / ROPE_HALF)
    cos_m, sin_m = jnp.cos(pos * inv_m[None, :]), jnp.sin(pos * inv_m[None, :])
    z = lambda width: jnp.zeros((n, width), F32)
    tail = HEAD_PAD - ROPE_OFF - MLA_ROPE
    cm = jnp.concatenate([jnp.ones((n, ROPE_OFF), F32), cos_m, cos_m, z(tail)], axis=1)
    s1 = jnp.concatenate([z(ROPE_OFF), -sin_m, z(ROPE_HALF), z(tail)], axis=1)
    s2 = jnp.concatenate([z(ROPE_OFF), z(ROPE_HALF), sin_m, z(tail)], axis=1)
    half = RET_DK // 2
    inv_r = ROPE_THETA ** (-jnp.arange(half, dtype=F32) / half)
    cos_r, sin_r = jnp.cos(pos * inv_r[None, :]), jnp.sin(pos * inv_r[None, :])
    cr = jnp.concatenate([cos_r, cos_r], axis=1)
    sr = jnp.concatenate([-sin_r, sin_r], axis=1)
    return cm, s1, s2, cr, sr


def _layer_weights(i, ffn1_norm, ffn1_w_gate, ffn1_w_up, ffn1_w_down, mix_norm, w_in, q_a_norm, w_q_b,
                   kv_a_norm, w_kv_b, ret_norm, w_branch_att, w_branch_ret, w_out,
                   ffn2_norm, ffn2_w_gate, ffn2_w_up, ffn2_w_down, ple_norm, w_ple_gate, w_ple_proj,
                   final_norm):
    vec = lambda a: a.reshape(1, -1)
    bf = lambda a: a.astype(BF16)
    win = w_in[i]
    zcols = lambda width: jnp.zeros((D_MODEL, width), F32)
    w_in_front = jnp.concatenate(
        [win[:, :_OFF_KPE], zcols(ROPE_OFF), win[:, _OFF_KPE:_OFF_RQ],
         zcols(HEAD_PAD - ROPE_OFF - MLA_ROPE), win[:, _OFF_RQ:_OFF_RG]], axis=1)
    wqb = w_q_b[i].reshape(Q_LORA, MLA_HEADS, MLA_NOPE + MLA_ROPE)
    wqb = jnp.pad(wqb, ((0, 0), (0, 0), (0, HEAD_PAD - MLA_NOPE - MLA_ROPE)))
    wkv = w_kv_b[i].reshape(KV_LORA, MLA_HEADS, MLA_NOPE + MLA_V)
    w_uk, w_uv = wkv[..., :MLA_NOPE], wkv[..., MLA_NOPE:]
    w_k = jnp.pad(w_uk, ((0, 0), (0, 0), (0, HEAD_PAD - MLA_NOPE)))
    w_absorb = jnp.zeros((MLA_HEADS, HEAD_PAD, KEY_W), F32)
    w_absorb = w_absorb.at[:, :MLA_NOPE, :KV_LORA].set(jnp.transpose(w_uk, (1, 2, 0)))
    w_absorb = w_absorb.at[:, MLA_NOPE:MLA_NOPE + MLA_ROPE, KV_LORA:KV_LORA + MLA_ROPE].set(
        jnp.eye(MLA_ROPE, dtype=F32)[None])
    w_unabsorb = jnp.zeros((MLA_HEADS, KV_LORA, _ATT_W), F32)
    for hd in range(MLA_HEADS):
        w_unabsorb = w_unabsorb.at[hd, :, hd * MLA_V:(hd + 1) * MLA_V].set(w_uv[:, hd, :])
    return {
        'ffn1_norm': vec(ffn1_norm[i]), 'ffn1_w_gate': bf(ffn1_w_gate[i]), 'ffn1_w_up': bf(ffn1_w_up[i]),
        'ffn1_w_down': bf(ffn1_w_down[i]), 'mix_norm': vec(mix_norm[i]),
        'w_in_front': bf(w_in_front), 'w_in_gates': bf(win[:, _OFF_RG:]),
        'q_a_norm': vec(q_a_norm[i]), 'w_q_b': bf(wqb.reshape(Q_LORA, MLA_HEADS * HEAD_PAD)),
        'kv_a_norm': vec(kv_a_norm[i]), 'w_k': bf(w_k.reshape(KV_LORA, MLA_HEADS * HEAD_PAD)),
        'w_vt': bf(jnp.pad(jnp.transpose(w_uv, (1, 2, 0)), ((0, 0), (0, V_ROWS - MLA_V), (0, 0)))
                   .reshape(_VT_ROWS, KV_LORA)),
        'w_absorb': bf(w_absorb), 'w_unabsorb': bf(w_unabsorb),
        'ret_norm': ret_norm[i],
        'w_branch_att': bf(w_branch_att[i]), 'w_branch_ret': bf(w_branch_ret[i]), 'w_out': bf(w_out[i]),
        'ffn2_norm': vec(ffn2_norm[i]), 'ffn2_w_gate': bf(ffn2_w_gate[i]), 'ffn2_w_up': bf(ffn2_w_up[i]),
        'ffn2_w_down': bf(ffn2_w_down[i]), 'ple_norm': vec(ple_norm[i]),
        'w_ple_gate': bf(w_ple_gate[i]), 'w_ple_proj': bf(w_ple_proj[i]),
        'final_norm': vec(final_norm),
    }


def kernel(x_prompt, x_sample, cache_ckv, cache_kpe, state_ret, page_table, p_prompt, p_sample, ffn1_norm, ffn1_w_gate, ffn1_w_up, ffn1_w_down, mix_norm, w_in, q_a_norm, w_q_b, kv_a_norm, w_kv_b, ret_norm, w_branch_att, w_branch_ret, w_out, ffn2_norm, ffn2_w_gate, ffn2_w_up, ffn2_w_down, ple_norm, w_ple_gate, w_ple_proj, final_norm):
    batch, seq, _ = x_prompt.shape
    n_dec, dec_seq, _ = x_sample.shape
    depth = w_in.shape[0]
    assert dec_seq == 1 and depth == 1
    n_past = page_table.shape[1] * PAGE_SIZE

    w = _layer_weights(0, ffn1_norm, ffn1_w_gate, ffn1_w_up, ffn1_w_down, mix_norm, w_in, q_a_norm,
                       w_q_b, kv_a_norm, w_kv_b, ret_norm, w_branch_att, w_branch_ret, w_out,
                       ffn2_norm, ffn2_w_gate, ffn2_w_up, ffn2_w_down, ple_norm, w_ple_gate,
                       w_ple_proj, final_norm)

    tabs_p = _rope_tables(jnp.arange(seq, dtype=jnp.int32))
    (h_p, q_p, k_p, vt_p, ckv_p, kpe_p, qr_p, kr_p, rv_p) = _front_stage(
        x_prompt.reshape(batch * seq, D_MODEL), tabs_p, w, TOKEN_TILE, seq // TOKEN_TILE)
    bt = lambda a: a.reshape(batch, seq, a.shape[-1])
    o_att_p = _prompt_attention(q_p, k_p, vt_p, batch, seq)
    o_ret_p, ret_p = _prompt_retention(bt(qr_p), bt(kr_p), bt(rv_p), w['ret_norm'])
    y_p = _back_stage(h_p, o_att_p, o_ret_p.reshape(batch * seq, -1),
                      p_prompt.reshape(batch * seq, PLE_DIM), w, TOKEN_TILE)

    tabs_s = tuple(jnp.broadcast_to(t, (n_dec, LANES))
                   for t in _rope_tables(jnp.full((1,), n_past, jnp.int32)))
    (h_s, q_s, _, _, ckv_s, kpe_s, qr_s, kr_s, rv_s) = _front_stage(
        x_sample.reshape(n_dec, D_MODEL), tabs_s, w, n_dec, 1)
    q_abs = jnp.transpose(_absorb_q(q_s, w['w_absorb']), (1, 0, 2))
    k_new = jnp.concatenate([ckv_s, kpe_s[:, ROPE_OFF:ROPE_OFF + MLA_ROPE],
                             jnp.zeros((n_dec, KEY_W - KV_LORA - MLA_ROPE), F32)], axis=1)
    o_lat = _paged_attention(page_table, q_abs, k_new.reshape(n_dec, 1, KEY_W), cache_ckv, cache_kpe)
    o_att_s = _unabsorb(jnp.transpose(o_lat, (1, 0, 2)), w['w_unabsorb'])
    o_ret_s, ret_s = _sample_retention(qr_s, kr_s, rv_s, state_ret, w['ret_norm'])
    y_s = _back_stage(h_s, o_att_s, o_ret_s, p_sample.reshape(n_dec, PLE_DIM), w, n_dec)

    kpe_cols = slice(ROPE_OFF, ROPE_OFF + MLA_ROPE)
    return (y_p.reshape(batch, seq, D_MODEL),
            y_s.reshape(n_dec, 1, D_MODEL),
            ckv_p.reshape(1, batch, seq, KV_LORA),
            kpe_p[:, kpe_cols].reshape(1, batch, seq, MLA_ROPE),
            ret_p[None],
            ckv_s.reshape(1, n_dec, 1, KV_LORA),
            kpe_s[:, kpe_cols].reshape(1, n_dec, 1, MLA_ROPE),
            ret_s)
```

```python
import jax
import jax.numpy as jnp
import numpy as np
from jax import lax
from jax.experimental import pallas as pl
from jax.experimental.pallas import tpu as pltpu

F32 = jnp.float32
BF16 = jnp.bfloat16

D_MODEL = 1024
D_FF = 2816
PLE_DIM = 256
MLA_HEADS = 8
MLA_NOPE = 64
MLA_ROPE = 32
MLA_V = 64
Q_LORA = 256
KV_LORA = 128
RET_HEADS = 4
RET_DK = 128
RET_DV = 128
PAGE_SIZE = 128
ROPE_THETA = 10000.0
EPS = 1e-6

LANES = 128
HEAD_PAD = LANES
ROPE_OFF = MLA_NOPE
ROPE_HALF = MLA_ROPE // 2
SOFTMAX_SCALE = (MLA_NOPE + MLA_ROPE) ** -0.5
LOG2E = 1.4426950408889634
Q_SCALE = SOFTMAX_SCALE * LOG2E
NEG_BIG = -1e30

TOKEN_TILE = 256
ATTN_TILE = 512
RET_CHUNK = 512
PAGES_PER_STEP = 32
PAGE_GROUPS = 4
RET_SEQS_PER_STEP = 8
KEY_W = 2 * LANES
VMEM_LIMIT = 60 * 1024 * 1024

_OFF_CQ = 0
_OFF_CKV = _OFF_CQ + Q_LORA
_OFF_KPE = _OFF_CKV + KV_LORA
_OFF_RQ = _OFF_KPE + MLA_ROPE
_OFF_RK = _OFF_RQ + RET_HEADS * RET_DK
_OFF_RV = _OFF_RK + RET_HEADS * RET_DK
_OFF_RG = _OFF_RV + RET_HEADS * RET_DV
_RET_W = RET_HEADS * RET_DK
_ATT_W = MLA_HEADS * MLA_V
V_ROWS = MLA_V + 16
_VT_ROWS = MLA_HEADS * V_ROWS


def _rms(x, g):
    return x * lax.rsqrt(jnp.mean(x * x, axis=-1, keepdims=True) + EPS) * g


def _dot(a, b):
    return jnp.dot(a, b, preferred_element_type=F32)


def _dot_nt(a, b):
    return lax.dot_general(a, b, (((1,), (1,)), ((), ())), preferred_element_type=F32)


def _swiglu_half(x, g, wg, wu, wd):
    xn = _rms(x, g).astype(BF16)
    gate = _dot(xn, wg)
    up = _dot(xn, wu)
    act = (gate * jax.nn.sigmoid(gate) * up).astype(BF16)
    return x + 0.5 * _dot(act, wd)


def _const_spec(shape):
    n = len(shape)
    return pl.BlockSpec(shape, lambda *_: (0,) * n, pipeline_mode=pl.Buffered(1))


def _front_kernel(x_ref, cm_ref, s1_ref, s2_ref, cr_ref, sr_ref,
                  g1_ref, wg_ref, wu_ref, wd_ref, gmix_ref, win_ref,
                  gqa_ref, wqb_ref, gkv_ref, wk_ref, wvt_ref,
                  h_ref, q_ref, k_ref, vt_ref, ckv_ref, kpe_ref, qr_ref, kr_ref, rv_ref):
    x = x_ref[...]
    h = _swiglu_half(x, g1_ref[...], wg_ref[...], wu_ref[...], wd_ref[...])
    h_ref[...] = h
    un = _rms(h, gmix_ref[...]).astype(BF16)
    z = _dot(un, win_ref[...])

    cm, s1, s2 = cm_ref[...], s1_ref[...], s2_ref[...]

    def mla_rope(t):
        return (t * cm + pltpu.roll(t, LANES - ROPE_HALF, 1) * s1
                + pltpu.roll(t, ROPE_HALF, 1) * s2)

    cq = z[:, _OFF_CQ:_OFF_CQ + Q_LORA]
    q = _dot(_rms(cq, gqa_ref[...]).astype(BF16), wqb_ref[...])
    c_kv = _rms(z[:, Q_LORA:Q_LORA + KV_LORA], gkv_ref[...])
    ckv_ref[...] = c_kv
    c_kv_b = c_kv.astype(BF16)
    k_nope = _dot(c_kv_b, wk_ref[...])
    vt = _dot_nt(wvt_ref[...], c_kv_b)
    sum_row = lax.broadcasted_iota(jnp.int32, vt.shape, 0) % V_ROWS >= MLA_V
    vt_ref[...] = jnp.where(sum_row, 1.0, vt).astype(BF16)
    k_pe = mla_rope(z[:, Q_LORA + KV_LORA:Q_LORA + KV_LORA + LANES])
    kpe_ref[...] = k_pe
    for hd in range(MLA_HEADS):
        sl = slice(hd * HEAD_PAD, (hd + 1) * HEAD_PAD)
        q_ref[:, sl] = (mla_rope(q[:, sl]) * Q_SCALE).astype(BF16)
        k_ref[:, sl] = (k_nope[:, sl] + k_pe).astype(BF16)

    cr, sr = cr_ref[...], sr_ref[...]
    base = Q_LORA + KV_LORA + LANES
    for hd in range(RET_HEADS):
        sl = slice(hd * RET_DK, (hd + 1) * RET_DK)
        rq = z[:, base + hd * RET_DK:base + (hd + 1) * RET_DK]
        rk = z[:, base + _RET_W + hd * RET_DK:base + _RET_W + (hd + 1) * RET_DK]
        qr_ref[:, sl] = rq * cr + pltpu.roll(rq, RET_DK // 2, 1) * sr
        kr_ref[:, sl] = (rk * cr + pltpu.roll(rk, RET_DK // 2, 1) * sr) * (RET_DK ** -0.5)
    rv_ref[...] = z[:, base + 2 * _RET_W:base + 3 * _RET_W]


def _front_stage(x, tables, w, tile, table_tiles):
    n = x.shape[0]
    row = lambda width: pl.BlockSpec((tile, width), lambda i: (i, 0))
    tab = pl.BlockSpec((tile, LANES), lambda i: (i % table_tiles, 0))
    weights = [w['ffn1_norm'], w['ffn1_w_gate'], w['ffn1_w_up'], w['ffn1_w_down'], w['mix_norm'],
               w['w_in_front'], w['q_a_norm'], w['w_q_b'], w['kv_a_norm'], w['w_k'], w['w_vt']]
    outs = [((n, D_MODEL), F32, row(D_MODEL)),
            ((n, MLA_HEADS * HEAD_PAD), BF16, row(MLA_HEADS * HEAD_PAD)),
            ((n, MLA_HEADS * HEAD_PAD), BF16, row(MLA_HEADS * HEAD_PAD)),
            ((_VT_ROWS, n), BF16, pl.BlockSpec((_VT_ROWS, tile), lambda i: (0, i))),
            ((n, KV_LORA), F32, row(KV_LORA)),
            ((n, LANES), F32, row(LANES)),
            ((n, _RET_W), F32, row(_RET_W)),
            ((n, _RET_W), F32, row(_RET_W)),
            ((n, _RET_W), F32, row(_RET_W))]
    return pl.pallas_call(
        _front_kernel,
        grid=(n // tile,),
        in_specs=[row(D_MODEL)] + [tab] * 5 + [_const_spec(a.shape) for a in weights],
        out_specs=[spec for _, _, spec in outs],
        out_shape=[jax.ShapeDtypeStruct(shape, dt) for shape, dt, _ in outs],
        compiler_params=pltpu.CompilerParams(dimension_semantics=("arbitrary",),
                                             vmem_limit_bytes=VMEM_LIMIT),
        name="front_stage",
    )(x, *tables, *weights)


def _attn_kernel(qi_ref, ki_ref, q_ref, k_ref, vt_ref, o_ref, m_sc, acc_sc):
    t = pl.program_id(1)
    qi = qi_ref[t]
    ki = ki_ref[t]

    @pl.when(ki == 0)
    def _():
        m_sc[...] = jnp.full(m_sc.shape, NEG_BIG, F32)
        acc_sc[...] = jnp.zeros(acc_sc.shape, F32)

    def scores(hd):
        qh = q_ref[:, hd * HEAD_PAD:(hd + 1) * HEAD_PAD]
        kh = k_ref[:, hd * HEAD_PAD:(hd + 1) * HEAD_PAD]
        return _dot_nt(kh, qh)

    def update(masked):
        st_next = scores(0)
        for hd in range(MLA_HEADS):
            st = st_next
            if hd + 1 < MLA_HEADS:
                st_next = scores(hd + 1)
            if masked:
                key = lax.broadcasted_iota(jnp.int32, st.shape, 0)
                qry = lax.broadcasted_iota(jnp.int32, st.shape, 1)
                st = jnp.where(key <= qry, st, NEG_BIG)
            m_prev = m_sc[hd]
            m_new = jnp.maximum(m_prev, jnp.max(st, axis=0, keepdims=True))
            alpha = jnp.exp2(m_prev - m_new)
            p = jnp.exp2(st - m_new)
            acc_sc[hd] = alpha * acc_sc[hd] + _dot(vt_ref[hd * V_ROWS:(hd + 1) * V_ROWS, :],
                                                    p.astype(BF16))
            m_sc[hd] = m_new

    @pl.when(ki < qi)
    def _():
        update(False)

    @pl.when(ki == qi)
    def _():
        update(True)
        acc = acc_sc[...]
        o_t = (acc[:, :MLA_V, :] / acc[:, MLA_V:MLA_V + 1, :]).reshape(_ATT_W, ATTN_TILE)
        o_ref[...] = jnp.transpose(o_t).astype(o_ref.dtype)


def _prompt_attention(q, k, vt, batch, seq):
    nt = seq // ATTN_TILE
    qi_list = np.array([i for i in range(nt) for _ in range(i + 1)], np.int32)
    ki_list = np.array([j for i in range(nt) for j in range(i + 1)], np.int32)
    qk_w = MLA_HEADS * HEAD_PAD
    grid_spec = pltpu.PrefetchScalarGridSpec(
        num_scalar_prefetch=2,
        grid=(batch, len(qi_list)),
        in_specs=[
            pl.BlockSpec((ATTN_TILE, qk_w), lambda bb, s, qi, ki: (bb * nt + qi[s], 0)),
            pl.BlockSpec((ATTN_TILE, qk_w), lambda bb, s, qi, ki: (bb * nt + ki[s], 0)),
            pl.BlockSpec((_VT_ROWS, ATTN_TILE), lambda bb, s, qi, ki: (0, bb * nt + ki[s])),
        ],
        out_specs=pl.BlockSpec((ATTN_TILE, _ATT_W), lambda bb, s, qi, ki: (bb * nt + qi[s], 0)),
        scratch_shapes=[pltpu.VMEM((MLA_HEADS, 1, ATTN_TILE), F32),
                        pltpu.VMEM((MLA_HEADS, V_ROWS, ATTN_TILE), F32)],
    )
    return pl.pallas_call(
        _attn_kernel,
        grid_spec=grid_spec,
        out_shape=jax.ShapeDtypeStruct((batch * seq, _ATT_W), BF16),
        compiler_params=pltpu.CompilerParams(dimension_semantics=("arbitrary", "arbitrary"),
                                             vmem_limit_bytes=VMEM_LIMIT),
        name="prompt_attention",
    )(jnp.asarray(qi_list), jnp.asarray(ki_list), q, k, vt)


def _head_layernorm(o, g):
    mu = jnp.mean(o, axis=-1, keepdims=True)
    d = o - mu
    var = jnp.mean(d * d, axis=-1, keepdims=True)
    return d * lax.rsqrt(var + EPS) * g


def _ret_prompt_kernel(q_ref, k_ref, v_ref, dec_ref, qd_ref, kd_ref, gc_ref, g_ref,
                       o_ref, s_out_ref, s_sc):
    c = pl.program_id(1)

    @pl.when(c == 0)
    def _():
        s_sc[...] = jnp.zeros(s_sc.shape, F32)

    for hd in range(RET_HEADS):
        sl = slice(hd * RET_DK, (hd + 1) * RET_DK)
        q = q_ref[0, :, sl]
        k = k_ref[0, :, sl]
        vb = v_ref[0, :, sl].astype(BF16)
        qb = q.astype(BF16)
        state = s_sc[hd]
        scores = _dot_nt(qb, k.astype(BF16)) * dec_ref[hd]
        inner = _dot(scores.astype(BF16), vb)
        cross = _dot(qb, state.astype(BF16)) * qd_ref[hd]
        o_ref[0, :, sl] = _head_layernorm(inner + cross, g_ref[hd:hd + 1, :])
        k_dec_t = jnp.transpose(k * kd_ref[hd]).astype(BF16)
        s_sc[hd] = gc_ref[hd:hd + 1, :] * state + _dot(k_dec_t, vb)

    @pl.when(c == pl.num_programs(1) - 1)
    def _():
        s_out_ref[0] = s_sc[...]


def _log_gamma():
    return jnp.log1p(-jnp.exp2(-5.0 - jnp.arange(RET_HEADS, dtype=F32)))


def _retention_consts(chunk):
    log_gamma = _log_gamma()
    idx = jnp.arange(chunk, dtype=F32)
    diff = idx[:, None] - idx[None, :]
    decay = jnp.where(diff[None] >= 0,
                      jnp.exp(log_gamma[:, None, None] * jnp.maximum(diff, 0.0)[None]), 0.0)
    q_decay = jnp.exp(log_gamma[:, None] * (idx + 1.0)[None, :])
    k_decay = jnp.exp(log_gamma[:, None] * (chunk - 1.0 - idx)[None, :])
    chunk_decay = jnp.exp(log_gamma * chunk)
    lane = lambda a: jnp.broadcast_to(a[..., None], a.shape + (LANES,))
    return decay, lane(q_decay), lane(k_decay), lane(chunk_decay)


def _prompt_retention(qr, kr, rv, ret_norm):
    b, t, _ = qr.shape
    decay, q_decay, k_decay, chunk_decay = _retention_consts(RET_CHUNK)
    seq = pl.BlockSpec((1, RET_CHUNK, _RET_W), lambda bb, c: (bb, c, 0))
    return pl.pallas_call(
        _ret_prompt_kernel,
        grid=(b, t // RET_CHUNK),
        in_specs=[seq, seq, seq, _const_spec(decay.shape), _const_spec(q_decay.shape),
                  _const_spec(k_decay.shape), _const_spec(chunk_decay.shape),
                  _const_spec(ret_norm.shape)],
        out_specs=[seq, pl.BlockSpec((1, RET_HEADS, RET_DK, RET_DV), lambda bb, c: (bb, 0, 0, 0))],
        out_shape=[jax.ShapeDtypeStruct((b, t, _RET_W), F32),
                   jax.ShapeDtypeStruct((b, RET_HEADS, RET_DK, RET_DV), F32)],
        scratch_shapes=[pltpu.VMEM((RET_HEADS, RET_DK, RET_DV), F32)],
        compiler_params=pltpu.CompilerParams(dimension_semantics=("arbitrary", "arbitrary")),
        name="prompt_retention",
    )(qr, kr, rv, decay, q_decay, k_decay, chunk_decay, ret_norm)


def _ret_sample_kernel(q_ref, k_ref, v_ref, s_ref, gam_ref, g_ref, o_ref, s_out_ref):
    rows = lax.broadcasted_iota(jnp.int32, (RET_DK, RET_DK), 0)
    cols = lax.broadcasted_iota(jnp.int32, (RET_DK, RET_DK), 1)
    eye = rows == cols

    def column(r):
        return jnp.sum(jnp.where(eye, jnp.broadcast_to(r, (RET_DK, RET_DK)), 0.0),
                       axis=1, keepdims=True)

    heads = range(RET_HEADS)
    lanes = [slice(hd * RET_DK, (hd + 1) * RET_DK) for hd in heads]
    gams = [gam_ref[hd:hd + 1, :] for hd in heads]
    for i in range(RET_SEQS_PER_STEP):
        qs = [q_ref[i, :, sl] for sl in lanes]
        ks = [k_ref[i, :, sl] for sl in lanes]
        vs = [v_ref[i, :, sl] for sl in lanes]
        q_cols = [column(q) for q in qs]
        k_cols = [column(k) for k in ks]
        qk = [jnp.sum(q * k, axis=-1, keepdims=True) for q, k in zip(qs, ks)]
        states = [s_ref[0, i, hd] for hd in heads]
        outs = [qk[hd] * vs[hd] + jnp.sum(q_cols[hd] * states[hd], axis=0, keepdims=True) * gams[hd]
                for hd in heads]
        for hd in heads:
            s_out_ref[0, i, hd] = gams[hd] * states[hd] + k_cols[hd] * vs[hd]
        mus = [jnp.mean(o, axis=-1, keepdims=True) for o in outs]
        devs = [o - mu for o, mu in zip(outs, mus)]
        variances = [jnp.mean(d * d, axis=-1, keepdims=True) for d in devs]
        for hd in heads:
            o_ref[i, :, lanes[hd]] = devs[hd] * lax.rsqrt(variances[hd] + EPS) * g_ref[hd:hd + 1, :]


def _sample_retention(qr, kr, rv, state, ret_norm):
    n = qr.shape[0]
    gam = jnp.broadcast_to(jnp.exp(_log_gamma() * 1.0)[:, None], (RET_HEADS, LANES))
    g = RET_SEQS_PER_STEP
    tok = pl.BlockSpec((g, 1, _RET_W), lambda i: (i, 0, 0))
    st = pl.BlockSpec((1, g, RET_HEADS, RET_DK, RET_DV), lambda i: (0, i, 0, 0, 0))
    r3 = lambda a: a.reshape(n, 1, _RET_W)
    o, s_new = pl.pallas_call(
        _ret_sample_kernel,
        grid=(n // g,),
        in_specs=[tok, tok, tok, st, _const_spec(gam.shape), _const_spec(ret_norm.shape)],
        out_specs=[tok, st],
        out_shape=[jax.ShapeDtypeStruct((n, 1, _RET_W), F32),
                   jax.ShapeDtypeStruct(state.shape, F32)],
        compiler_params=pltpu.CompilerParams(dimension_semantics=("arbitrary",)),
        name="sample_retention",
    )(r3(qr), r3(kr), r3(rv), state, gam, ret_norm)
    return o.reshape(n, _RET_W), s_new


def _absorb_q_kernel(q_ref, w_ref, o_ref):
    for hd in range(MLA_HEADS):
        o_ref[hd] = _dot(q_ref[:, hd * HEAD_PAD:(hd + 1) * HEAD_PAD], w_ref[hd])


def _absorb_q(q, w_absorb):
    n = q.shape[0]
    return pl.pallas_call(
        _absorb_q_kernel,
        out_shape=jax.ShapeDtypeStruct((MLA_HEADS, n, KEY_W), F32),
        name="sample_absorb_q",
    )(q, w_absorb)


def _unabsorb_kernel(o_ref, w_ref, out_ref):
    acc = _dot(o_ref[0].astype(BF16), w_ref[0])
    for hd in range(1, MLA_HEADS):
        acc = acc + _dot(o_ref[hd].astype(BF16), w_ref[hd])
    out_ref[...] = acc.astype(out_ref.dtype)


def _unabsorb(o_lat, w_unabsorb):
    n = o_lat.shape[1]
    return pl.pallas_call(
        _unabsorb_kernel,
        out_shape=jax.ShapeDtypeStruct((n, _ATT_W), BF16),
        name="sample_unabsorb",
    )(o_lat, w_unabsorb)


def _paged_kernel(pt_ref, q_ref, knew_ref, ckv_hbm, kpet_hbm, o_ref, buf, kpe_buf, sem):
    b = pl.program_id(0)
    n_seq = pl.num_programs(0)
    n_chunks = pt_ref.shape[1] // PAGES_PER_STEP
    assert n_chunks % 2 == 0
    chunk_keys = PAGES_PER_STEP * PAGE_SIZE
    group_keys = chunk_keys // PAGE_GROUPS

    def page_copies(bb, cc, sl):
        out = []
        for j in range(PAGES_PER_STEP):
            pg = pt_ref[bb, cc * PAGES_PER_STEP + j]
            keys = pl.ds(j * PAGE_SIZE, PAGE_SIZE)
            out.append(pltpu.make_async_copy(ckv_hbm.at[0, pg], buf.at[sl, keys], sem.at[sl, 0]))
            out.append(pltpu.make_async_copy(kpet_hbm.at[0, pg], kpe_buf.at[sl, :, keys],
                                             sem.at[sl, 1]))
        return out

    @pl.when(b == 0)
    def _():
        for cp in page_copies(0, 0, 0):
            cp.start()

    q = q_ref[0]
    q_lat = q[:, 0:KV_LORA]
    q_pe = q[:, KV_LORA:KV_LORA + MLA_ROPE]

    def scores(sl, g):
        keys = slice(g * group_keys, (g + 1) * group_keys)
        return _dot_nt(q_lat, buf[sl, keys, :]) + _dot(q_pe, kpe_buf[sl, :, keys])

    m_run = jnp.full((MLA_HEADS, 1), NEG_BIG, F32)
    l_run = jnp.zeros((MLA_HEADS, 1), F32)
    acc_run = jnp.zeros((MLA_HEADS, KV_LORA), F32)
    for c in range(n_chunks):
        sl = c % 2
        if c + 1 < n_chunks:
            prefetch = page_copies(b, c + 1, 1 - sl)
        else:
            prefetch = []

            @pl.when(b + 1 < n_seq)
            def _():
                for cp in page_copies(b + 1, 0, 1 - sl):
                    cp.start()
        for cp in page_copies(b, c, sl):
            cp.wait()

        parts = []
        per_group = len(prefetch) // PAGE_GROUPS
        s_next = scores(sl, 0)
        for g in range(PAGE_GROUPS):
            s = s_next
            if g + 1 < PAGE_GROUPS:
                s_next = scores(sl, g + 1)
            for cp in prefetch[g * per_group:(g + 1) * per_group]:
                cp.start()
            m_g = jnp.max(s, axis=-1, keepdims=True)
            p = jnp.exp2(s - m_g)
            l_g = jnp.sum(p, axis=-1, keepdims=True)
            keys = slice(g * group_keys, (g + 1) * group_keys)
            parts.append((m_g, l_g, _dot(p, buf[sl, keys, :])))

        m_new = m_run
        for m_g, _, _ in parts:
            m_new = jnp.maximum(m_new, m_g)
        a_run = jnp.exp2(m_run - m_new)
        l_run = a_run * l_run
        acc_run = a_run * acc_run
        for m_g, l_g, o_g in parts:
            a_g = jnp.exp2(m_g - m_new)
            l_run = l_run + a_g * l_g
            acc_run = acc_run + a_g * o_g
        m_run = m_new

    kn = knew_ref[0]
    s_self = jnp.sum(q * kn, axis=-1, keepdims=True)
    m_fin = jnp.maximum(m_run, s_self)
    a = jnp.exp2(m_run - m_fin)
    p_self = jnp.exp2(s_self - m_fin)
    o_ref[0] = (a * acc_run + p_self * kn[:, 0:KV_LORA]) / (a * l_run + p_self)


def _paged_attention(page_table, q_abs, k_new, cache_ckv, cache_kpe_t):
    n, n_pages = page_table.shape
    chunk_keys = PAGES_PER_STEP * PAGE_SIZE
    grid_spec = pltpu.PrefetchScalarGridSpec(
        num_scalar_prefetch=1,
        grid=(n,),
        in_specs=[
            pl.BlockSpec((1, MLA_HEADS, KEY_W), lambda b, pt: (b, 0, 0)),
            pl.BlockSpec((1, 1, KEY_W), lambda b, pt: (b, 0, 0)),
            pl.BlockSpec(memory_space=pl.ANY),
            pl.BlockSpec(memory_space=pl.ANY),
        ],
        out_specs=pl.BlockSpec((1, MLA_HEADS, KV_LORA), lambda b, pt: (b, 0, 0)),
        scratch_shapes=[pltpu.VMEM((2, chunk_keys, KV_LORA), F32),
                        pltpu.VMEM((2, MLA_ROPE, chunk_keys), F32),
                        pltpu.SemaphoreType.DMA((2, 2))],
    )
    return pl.pallas_call(
        _paged_kernel,
        grid_spec=grid_spec,
        out_shape=jax.ShapeDtypeStruct((n, MLA_HEADS, KV_LORA), F32),
        compiler_params=pltpu.CompilerParams(dimension_semantics=("arbitrary",),
                                             vmem_limit_bytes=VMEM_LIMIT),
        name="sample_paged_attention",
    )(page_table, q_abs, k_new, cache_ckv, cache_kpe_t)


def _back_kernel(h_ref, oatt_ref, oret_ref, p_ref,
                 gmix_ref, wgate_ref, wba_ref, wbr_ref, wout_ref,
                 g2_ref, wg_ref, wu_ref, wd_ref, gple_ref, wpg_ref, wpp_ref, gfin_ref, y_ref):
    h = h_ref[...]
    un = _rms(h, gmix_ref[...]).astype(BF16)
    gates = _dot(un, wgate_ref[...])
    rg = gates[:, :_RET_W]
    ga = gates[:, _RET_W:_RET_W + D_MODEL]
    gr = gates[:, _RET_W + D_MODEL:]
    o_ret = (rg * jax.nn.sigmoid(rg) * oret_ref[...]).astype(BF16)
    merged = (jax.nn.sigmoid(ga) * _dot(oatt_ref[...], wba_ref[...])
              + jax.nn.sigmoid(gr) * _dot(o_ret, wbr_ref[...]))
    h = h + _dot(merged.astype(BF16), wout_ref[...])
    h = _swiglu_half(h, g2_ref[...], wg_ref[...], wu_ref[...], wd_ref[...])
    gate = jax.nn.sigmoid(_dot(_rms(h, gple_ref[...]).astype(BF16), wpg_ref[...]))
    h = h + gate * _dot(p_ref[...].astype(BF16), wpp_ref[...])
    y_ref[...] = _rms(h, gfin_ref[...])


def _back_stage(h, o_att, o_ret, p_emb, w, tile):
    n = h.shape[0]
    row = lambda width: pl.BlockSpec((tile, width), lambda i: (i, 0))
    weights = [w['mix_norm'], w['w_in_gates'], w['w_branch_att'], w['w_branch_ret'], w['w_out'],
               w['ffn2_norm'], w['ffn2_w_gate'], w['ffn2_w_up'], w['ffn2_w_down'],
               w['ple_norm'], w['w_ple_gate'], w['w_ple_proj'], w['final_norm']]
    return pl.pallas_call(
        _back_kernel,
        grid=(n // tile,),
        in_specs=[row(D_MODEL), row(_ATT_W), row(_RET_W), row(PLE_DIM)]
        + [_const_spec(a.shape) for a in weights],
        out_specs=row(D_MODEL),
        out_shape=jax.ShapeDtypeStruct((n, D_MODEL), F32),
        compiler_params=pltpu.CompilerParams(dimension_semantics=("arbitrary",),
                                             vmem_limit_bytes=VMEM_LIMIT),
        name="back_stage",
    )(h, o_att, o_ret, p_emb, *weights)


def _rope_tables(pos):
    pos = pos.astype(F32)[:, None]
    n = pos.shape[0]
    inv_m = ROPE_THETA ** (-jnp.arange(ROPE_HALF, dtype=F32) / ROPE_HALF)
    cos_m, sin_m = jnp.cos(pos * inv_m[None, :]), jnp.sin(pos * inv_m[None, :])
    z = lambda width: jnp.zeros((n, width), F32)
    tail = HEAD_PAD - ROPE_OFF - MLA_ROPE
    cm = jnp.concatenate([jnp.ones((n, ROPE_OFF), F32), cos_m, cos_m, z(tail)], axis=1)
    s1 = jnp.concatenate([z(ROPE_OFF), -sin_m, z(ROPE_HALF), z(tail)], axis=1)
    s2 = jnp.concatenate([z(ROPE_OFF), z(ROPE_HALF), sin_m, z(tail)], axis=1)
    half = RET_DK // 2
    inv_r = ROPE_THETA ** (-jnp.arange(half, dtype=F32) / half)
    cos_r, sin_r = jnp.cos(pos * inv_r[None, :]), jnp.sin(pos * inv_r[None, :])
    cr = jnp.concatenate([cos_r, cos_r], axis=1)
    sr = jnp.concatenate([-sin_r, sin_r], axis=1)
    return cm, s1, s2, cr, sr


def _layer_weights(i, ffn1_norm, ffn1_w_gate, ffn1_w_up, ffn1_w_down, mix_norm, w_in, q_a_norm, w_q_b,
                   kv_a_norm, w_kv_b, ret_norm, w_branch_att, w_branch_ret, w_out,
                   ffn2_norm, ffn2_w_gate, ffn2_w_up, ffn2_w_down, ple_norm, w_ple_gate, w_ple_proj,
                   final_norm):
    vec = lambda a: a.reshape(1, -1)
    bf = lambda a: a.astype(BF16)
    win = w_in[i]
    zcols = lambda width: jnp.zeros((D_MODEL, width), F32)
    w_in_front = jnp.concatenate(
        [win[:, :_OFF_KPE], zcols(ROPE_OFF), win[:, _OFF_KPE:_OFF_RQ],
         zcols(HEAD_PAD - ROPE_OFF - MLA_ROPE), win[:, _OFF_RQ:_OFF_RG]], axis=1)
    wqb = w_q_b[i].reshape(Q_LORA, MLA_HEADS, MLA_NOPE + MLA_ROPE)
    wqb = jnp.pad(wqb, ((0, 0), (0, 0), (0, HEAD_PAD - MLA_NOPE - MLA_ROPE)))
    wkv = w_kv_b[i].reshape(KV_LORA, MLA_HEADS, MLA_NOPE + MLA_V)
    w_uk, w_uv = wkv[..., :MLA_NOPE], wkv[..., MLA_NOPE:]
    w_k = jnp.pad(w_uk, ((0, 0), (0, 0), (0, HEAD_PAD - MLA_NOPE)))
    w_absorb = jnp.zeros((MLA_HEADS, HEAD_PAD, KEY_W), F32)
    w_absorb = w_absorb.at[:, :MLA_NOPE, :KV_LORA].set(jnp.transpose(w_uk, (1, 2, 0)))
    w_absorb = w_absorb.at[:, MLA_NOPE:MLA_NOPE + MLA_ROPE, KV_LORA:KV_LORA + MLA_ROPE].set(
        jnp.eye(MLA_ROPE, dtype=F32)[None])
    w_unabsorb = jnp.zeros((MLA_HEADS, KV_LORA, _ATT_W), F32)
    for hd in range(MLA_HEADS):
        w_unabsorb = w_unabsorb.at[hd, :, hd * MLA_V:(hd + 1) * MLA_V].set(w_uv[:, hd, :])
    return {
        'ffn1_norm': vec(ffn1_norm[i]), 'ffn1_w_gate': bf(ffn1_w_gate[i]), 'ffn1_w_up': bf(ffn1_w_up[i]),
        'ffn1_w_down': bf(ffn1_w_down[i]), 'mix_norm': vec(mix_norm[i]),
        'w_in_front': bf(w_in_front), 'w_in_gates': bf(win[:, _OFF_RG:]),
        'q_a_norm': vec(q_a_norm[i]), 'w_q_b': bf(wqb.reshape(Q_LORA, MLA_HEADS * HEAD_PAD)),
        'kv_a_norm': vec(kv_a_norm[i]), 'w_k': bf(w_k.reshape(KV_LORA, MLA_HEADS * HEAD_PAD)),
        'w_vt': bf(jnp.pad(jnp.transpose(w_uv, (1, 2, 0)), ((0, 0), (0, V_ROWS - MLA_V), (0, 0)))
                   .reshape(_VT_ROWS, KV_LORA)),
        'w_absorb': bf(w_absorb), 'w_unabsorb': bf(w_unabsorb),
        'ret_norm': ret_norm[i],
        'w_branch_att': bf(w_branch_att[i]), 'w_branch_ret': bf(w_branch_ret[i]), 'w_out': bf(w_out[i]),
        'ffn2_norm': vec(ffn2_norm[i]), 'ffn2_w_gate': bf(ffn2_w_gate[i]), 'ffn2_w_up': bf(ffn2_w_up[i]),
        'ffn2_w_down': bf(ffn2_w_down[i]), 'ple_norm': vec(ple_norm[i]),
        'w_ple_gate': bf(w_ple_gate[i]), 'w_ple_proj': bf(w_ple_proj[i]),
        'final_norm': vec(final_norm),
    }


def kernel(x_prompt, x_sample, cache_ckv, cache_kpe, state_ret, page_table, p_prompt, p_sample, ffn1_norm, ffn1_w_gate, ffn1_w_up, ffn1_w_down, mix_norm, w_in, q_a_norm, w_q_b, kv_a_norm, w_kv_b, ret_norm, w_branch_att, w_branch_ret, w_out, ffn2_norm, ffn2_w_gate, ffn2_w_up, ffn2_w_down, ple_norm, w_ple_gate, w_ple_proj, final_norm):
    batch, seq, _ = x_prompt.shape
    n_dec, dec_seq, _ = x_sample.shape
    depth = w_in.shape[0]
    assert dec_seq == 1 and depth == 1
    n_past = page_table.shape[1] * PAGE_SIZE

    w = _layer_weights(0, ffn1_norm, ffn1_w_gate, ffn1_w_up, ffn1_w_down, mix_norm, w_in, q_a_norm,
                       w_q_b, kv_a_norm, w_kv_b, ret_norm, w_branch_att, w_branch_ret, w_out,
                       ffn2_norm, ffn2_w_gate, ffn2_w_up, ffn2_w_down, ple_norm, w_ple_gate,
                       w_ple_proj, final_norm)

    tabs_p = _rope_tables(jnp.arange(seq, dtype=jnp.int32))
    (h_p, q_p, k_p, vt_p, ckv_p, kpe_p, qr_p, kr_p, rv_p) = _front_stage(
        x_prompt.reshape(batch * seq, D_MODEL), tabs_p, w, TOKEN_TILE, seq // TOKEN_TILE)
    bt = lambda a: a.reshape(batch, seq, a.shape[-1])
    o_att_p = _prompt_attention(q_p, k_p, vt_p, batch, seq)
    o_ret_p, ret_p = _prompt_retention(bt(qr_p), bt(kr_p), bt(rv_p), w['ret_norm'])
    y_p = _back_stage(h_p, o_att_p, o_ret_p.reshape(batch * seq, -1),
                      p_prompt.reshape(batch * seq, PLE_DIM), w, TOKEN_TILE)

    tabs_s = tuple(jnp.broadcast_to(t, (n_dec, LANES))
                   for t in _rope_tables(jnp.full((1,), n_past, jnp.int32)))
    (h_s, q_s, _, _, ckv_s, kpe_s, qr_s, kr_s, rv_s) = _front_stage(
        x_sample.reshape(n_dec, D_MODEL), tabs_s, w, n_dec, 1)
    q_abs = jnp.transpose(_absorb_q(q_s, w['w_absorb']), (1, 0, 2))
    k_new = jnp.concatenate([ckv_s, kpe_s[:, ROPE_OFF:ROPE_OFF + MLA_ROPE],
                             jnp.zeros((n_dec, KEY_W - KV_LORA - MLA_ROPE), F32)], axis=1)
    o_lat = _paged_attention(page_table, q_abs, k_new.reshape(n_dec, 1, KEY_W), cache_ckv,
                             jnp.swapaxes(cache_kpe, 2, 3))
    o_att_s = _unabsorb(jnp.transpose(o_lat, (1, 0, 2)), w['w_unabsorb'])
    o_ret_s, ret_s = _sample_retention(qr_s, kr_s, rv_s, state_ret, w['ret_norm'])
    y_s = _back_stage(h_s, o_att_s, o_ret_s, p_sample.reshape(n_dec, PLE_DIM), w, n_dec)

    kpe_cols = slice(ROPE_OFF, ROPE_OFF + MLA_ROPE)
    return (y_p.reshape(batch, seq, D_MODEL),
            y_s.reshape(n_dec, 1, D_MODEL),
            ckv_p.reshape(1, batch, seq, KV_LORA),
            kpe_p[:, kpe_cols].reshape(1, batch, seq, MLA_ROPE),
            ret_p[None],
            ckv_s.reshape(1, n_dec, 1, KV_LORA),
            kpe_s[:, kpe_cols].reshape(1, n_dec, 1, MLA_ROPE),
            ret_s)
```

```python
import jax
import jax.numpy as jnp
import numpy as np
from jax import lax
from jax.experimental import pallas as pl
from jax.experimental.pallas import tpu as pltpu

F32 = jnp.float32
BF16 = jnp.bfloat16

D_MODEL = 1024
D_FF = 2816
PLE_DIM = 256
MLA_HEADS = 8
MLA_NOPE = 64
MLA_ROPE = 32
MLA_V = 64
Q_LORA = 256
KV_LORA = 128
RET_HEADS = 4
RET_DK = 128
RET_DV = 128
PAGE_SIZE = 128
ROPE_THETA = 10000.0
EPS = 1e-6

LANES = 128
HEAD_PAD = LANES
ROPE_OFF = MLA_NOPE
ROPE_HALF = MLA_ROPE // 2
SOFTMAX_SCALE = (MLA_NOPE + MLA_ROPE) ** -0.5
LOG2E = 1.4426950408889634
Q_SCALE = SOFTMAX_SCALE * LOG2E
NEG_BIG = -1e30

TOKEN_TILE = 256
ATTN_TILE = 512
RET_CHUNK = 512
PAGES_PER_STEP = 32
PAGE_GROUPS = 4
PAGE_SLOTS = 4
RET_SEQS_PER_STEP = 8
KEY_W = 2 * LANES
VMEM_LIMIT = 60 * 1024 * 1024

_OFF_CQ = 0
_OFF_CKV = _OFF_CQ + Q_LORA
_OFF_KPE = _OFF_CKV + KV_LORA
_OFF_RQ = _OFF_KPE + MLA_ROPE
_OFF_RK = _OFF_RQ + RET_HEADS * RET_DK
_OFF_RV = _OFF_RK + RET_HEADS * RET_DK
_OFF_RG = _OFF_RV + RET_HEADS * RET_DV
_RET_W = RET_HEADS * RET_DK
_ATT_W = MLA_HEADS * MLA_V
V_ROWS = MLA_V + 16
_VT_ROWS = MLA_HEADS * V_ROWS


def _rms(x, g):
    return x * lax.rsqrt(jnp.mean(x * x, axis=-1, keepdims=True) + EPS) * g


def _dot(a, b):
    return jnp.dot(a, b, preferred_element_type=F32)


def _dot_nt(a, b):
    return lax.dot_general(a, b, (((1,), (1,)), ((), ())), preferred_element_type=F32)


def _swiglu_half(x, g, wg, wu, wd):
    xn = _rms(x, g).astype(BF16)
    gate = _dot(xn, wg)
    up = _dot(xn, wu)
    act = (gate * jax.nn.sigmoid(gate) * up).astype(BF16)
    return x + 0.5 * _dot(act, wd)


def _const_spec(shape):
    n = len(shape)
    return pl.BlockSpec(shape, lambda *_: (0,) * n, pipeline_mode=pl.Buffered(1))


def _front_kernel(x_ref, cm_ref, s1_ref, s2_ref, cr_ref, sr_ref, ct_ref, st_ref,
                  g1_ref, wg_ref, wu_ref, wd_ref, gmix_ref, win_ref,
                  gqa_ref, wqbt_ref, gkv_ref, wk_ref, wvt_ref,
                  h_ref, qt_ref, k_ref, vt_ref, ckv_ref, kpe_ref, qr_ref, kr_ref, rv_ref):
    x = x_ref[...]
    h = _swiglu_half(x, g1_ref[...], wg_ref[...], wu_ref[...], wd_ref[...])
    h_ref[...] = h
    un = _rms(h, gmix_ref[...]).astype(BF16)
    z = _dot(un, win_ref[...])

    cm, s1, s2 = cm_ref[...], s1_ref[...], s2_ref[...]

    def mla_rope(t):
        return (t * cm + pltpu.roll(t, LANES - ROPE_HALF, 1) * s1
                + pltpu.roll(t, ROPE_HALF, 1) * s2)

    cq = z[:, _OFF_CQ:_OFF_CQ + Q_LORA]
    qt = _dot_nt(wqbt_ref[...], _rms(cq, gqa_ref[...]).astype(BF16))
    ct, st = ct_ref[...], st_ref[...]
    c_kv = _rms(z[:, Q_LORA:Q_LORA + KV_LORA], gkv_ref[...])
    ckv_ref[...] = c_kv
    c_kv_b = c_kv.astype(BF16)
    k_nope = _dot(c_kv_b, wk_ref[...])
    vt = _dot_nt(wvt_ref[...], c_kv_b)
    sum_row = lax.broadcasted_iota(jnp.int32, vt.shape, 0) % V_ROWS >= MLA_V
    vt_ref[...] = jnp.where(sum_row, 1.0, vt).astype(BF16)
    k_pe = mla_rope(z[:, Q_LORA + KV_LORA:Q_LORA + KV_LORA + LANES])
    kpe_ref[...] = k_pe
    for hd in range(MLA_HEADS):
        sl = slice(hd * HEAD_PAD, (hd + 1) * HEAD_PAD)
        k_ref[:, sl] = (k_nope[:, sl] + k_pe).astype(BF16)
        r0 = hd * HEAD_PAD + ROPE_OFF
        x1 = qt[r0:r0 + ROPE_HALF]
        x2 = qt[r0 + ROPE_HALF:r0 + MLA_ROPE]
        qt_ref[hd * HEAD_PAD:r0, :] = (qt[hd * HEAD_PAD:r0] * Q_SCALE).astype(BF16)
        qt_ref[r0:r0 + ROPE_HALF, :] = ((x1 * ct - x2 * st) * Q_SCALE).astype(BF16)
        qt_ref[r0 + ROPE_HALF:r0 + MLA_ROPE, :] = ((x1 * st + x2 * ct) * Q_SCALE).astype(BF16)
        qt_ref[r0 + MLA_ROPE:(hd + 1) * HEAD_PAD, :] = jnp.zeros(
            (HEAD_PAD - ROPE_OFF - MLA_ROPE, qt.shape[1]), BF16)

    cr, sr = cr_ref[...], sr_ref[...]
    base = Q_LORA + KV_LORA + LANES
    for hd in range(RET_HEADS):
        sl = slice(hd * RET_DK, (hd + 1) * RET_DK)
        rq = z[:, base + hd * RET_DK:base + (hd + 1) * RET_DK]
        rk = z[:, base + _RET_W + hd * RET_DK:base + _RET_W + (hd + 1) * RET_DK]
        qr_ref[:, sl] = rq * cr + pltpu.roll(rq, RET_DK // 2, 1) * sr
        kr_ref[:, sl] = (rk * cr + pltpu.roll(rk, RET_DK // 2, 1) * sr) * (RET_DK ** -0.5)
    rv_ref[...] = z[:, base + 2 * _RET_W:base + 3 * _RET_W]


def _front_stage(x, tables, w, tile, table_tiles):
    n = x.shape[0]
    row = lambda width: pl.BlockSpec((tile, width), lambda i: (i, 0))
    col = lambda height: pl.BlockSpec((height, tile), lambda i: (0, i))
    tab = pl.BlockSpec((tile, LANES), lambda i: (i % table_tiles, 0))
    tab_t = pl.BlockSpec((ROPE_HALF, tile), lambda i: (0, i % table_tiles))
    weights = [w['ffn1_norm'], w['ffn1_w_gate'], w['ffn1_w_up'], w['ffn1_w_down'], w['mix_norm'],
               w['w_in_front'], w['q_a_norm'], w['w_q_bt'], w['kv_a_norm'], w['w_k'], w['w_vt']]
    outs = [((n, D_MODEL), F32, row(D_MODEL)),
            ((MLA_HEADS * HEAD_PAD, n), BF16, col(MLA_HEADS * HEAD_PAD)),
            ((n, MLA_HEADS * HEAD_PAD), BF16, row(MLA_HEADS * HEAD_PAD)),
            ((_VT_ROWS, n), BF16, col(_VT_ROWS)),
            ((n, KV_LORA), F32, row(KV_LORA)),
            ((n, LANES), F32, row(LANES)),
            ((n, _RET_W), F32, row(_RET_W)),
            ((n, _RET_W), F32, row(_RET_W)),
            ((n, _RET_W), F32, row(_RET_W))]
    return pl.pallas_call(
        _front_kernel,
        grid=(n // tile,),
        in_specs=[row(D_MODEL)] + [tab] * 5 + [tab_t] * 2 + [_const_spec(a.shape) for a in weights],
        out_specs=[spec for _, _, spec in outs],
        out_shape=[jax.ShapeDtypeStruct(shape, dt) for shape, dt, _ in outs],
        compiler_params=pltpu.CompilerParams(dimension_semantics=("arbitrary",),
                                             vmem_limit_bytes=VMEM_LIMIT),
        name="front_stage",
    )(x, *tables, *weights)


def _attn_kernel(qi_ref, ki_ref, qt_ref, k_ref, vt_ref, o_ref, m_sc, acc_sc):
    t = pl.program_id(1)
    qi = qi_ref[t]
    ki = ki_ref[t]

    @pl.when(ki == 0)
    def _():
        m_sc[...] = jnp.full(m_sc.shape, NEG_BIG, F32)
        acc_sc[...] = jnp.zeros(acc_sc.shape, F32)

    def scores(hd):
        kh = k_ref[:, hd * HEAD_PAD:(hd + 1) * HEAD_PAD]
        return _dot(kh, qt_ref[hd * HEAD_PAD:(hd + 1) * HEAD_PAD, :])

    def update(masked):
        st_next = scores(0)
        for hd in range(MLA_HEADS):
            st = st_next
            if hd + 1 < MLA_HEADS:
                st_next = scores(hd + 1)
            if masked:
                key = lax.broadcasted_iota(jnp.int32, st.shape, 0)
                qry = lax.broadcasted_iota(jnp.int32, st.shape, 1)
                st = jnp.where(key <= qry, st, NEG_BIG)
            m_prev = m_sc[hd]
            m_new = jnp.maximum(m_prev, jnp.max(st, axis=0, keepdims=True))
            alpha = jnp.exp2(m_prev - m_new)
            p = jnp.exp2(st - m_new)
            acc_sc[hd] = alpha * acc_sc[hd] + _dot(vt_ref[hd * V_ROWS:(hd + 1) * V_ROWS, :],
                                                    p.astype(BF16))
            m_sc[hd] = m_new

    @pl.when(ki < qi)
    def _():
        update(False)

    @pl.when(ki == qi)
    def _():
        update(True)
        acc = acc_sc[...]
        o_t = (acc[:, :MLA_V, :] / acc[:, MLA_V:MLA_V + 1, :]).reshape(_ATT_W, ATTN_TILE)
        o_ref[...] = jnp.transpose(o_t).astype(o_ref.dtype)


def _prompt_attention(qt, k, vt, batch, seq):
    nt = seq // ATTN_TILE
    qi_list = np.array([i for i in range(nt) for _ in range(i + 1)], np.int32)
    ki_list = np.array([j for i in range(nt) for j in range(i + 1)], np.int32)
    qk_w = MLA_HEADS * HEAD_PAD
    grid_spec = pltpu.PrefetchScalarGridSpec(
        num_scalar_prefetch=2,
        grid=(batch, len(qi_list)),
        in_specs=[
            pl.BlockSpec((qk_w, ATTN_TILE), lambda bb, s, qi, ki: (0, bb * nt + qi[s])),
            pl.BlockSpec((ATTN_TILE, qk_w), lambda bb, s, qi, ki: (bb * nt + ki[s], 0)),
            pl.BlockSpec((_VT_ROWS, ATTN_TILE), lambda bb, s, qi, ki: (0, bb * nt + ki[s])),
        ],
        out_specs=pl.BlockSpec((ATTN_TILE, _ATT_W), lambda bb, s, qi, ki: (bb * nt + qi[s], 0)),
        scratch_shapes=[pltpu.VMEM((MLA_HEADS, 1, ATTN_TILE), F32),
                        pltpu.VMEM((MLA_HEADS, V_ROWS, ATTN_TILE), F32)],
    )
    return pl.pallas_call(
        _attn_kernel,
        grid_spec=grid_spec,
        out_shape=jax.ShapeDtypeStruct((batch * seq, _ATT_W), BF16),
        compiler_params=pltpu.CompilerParams(dimension_semantics=("arbitrary", "arbitrary"),
                                             vmem_limit_bytes=VMEM_LIMIT),
        name="prompt_attention",
    )(jnp.asarray(qi_list), jnp.asarray(ki_list), qt, k, vt)


def _head_layernorm(o, g):
    mu = jnp.mean(o, axis=-1, keepdims=True)
    d = o - mu
    var = jnp.mean(d * d, axis=-1, keepdims=True)
    return d * lax.rsqrt(var + EPS) * g


def _ret_prompt_kernel(q_ref, k_ref, v_ref, dec_ref, qd_ref, kd_ref, gc_ref, g_ref,
                       o_ref, s_out_ref, s_sc):
    c = pl.program_id(1)

    @pl.when(c == 0)
    def _():
        s_sc[...] = jnp.zeros(s_sc.shape, F32)

    for hd in range(RET_HEADS):
        sl = slice(hd * RET_DK, (hd + 1) * RET_DK)
        q = q_ref[0, :, sl]
        k = k_ref[0, :, sl]
        vb = v_ref[0, :, sl].astype(BF16)
        qb = q.astype(BF16)
        state = s_sc[hd]
        scores = _dot_nt(qb, k.astype(BF16)) * dec_ref[hd]
        inner = _dot(scores.astype(BF16), vb)
        cross = _dot(qb, state.astype(BF16)) * qd_ref[hd]
        o_ref[0, :, sl] = _head_layernorm(inner + cross, g_ref[hd:hd + 1, :])
        k_dec_t = jnp.transpose(k * kd_ref[hd]).astype(BF16)
        s_sc[hd] = gc_ref[hd:hd + 1, :] * state + _dot(k_dec_t, vb)

    @pl.when(c == pl.num_programs(1) - 1)
    def _():
        s_out_ref[0] = s_sc[...]


def _log_gamma():
    return jnp.log1p(-jnp.exp2(-5.0 - jnp.arange(RET_HEADS, dtype=F32)))


def _retention_consts(chunk):
    log_gamma = _log_gamma()
    idx = jnp.arange(chunk, dtype=F32)
    diff = idx[:, None] - idx[None, :]
    decay = jnp.where(diff[None] >= 0,
                      jnp.exp(log_gamma[:, None, None] * jnp.maximum(diff, 0.0)[None]), 0.0)
    q_decay = jnp.exp(log_gamma[:, None] * (idx + 1.0)[None, :])
    k_decay = jnp.exp(log_gamma[:, None] * (chunk - 1.0 - idx)[None, :])
    chunk_decay = jnp.exp(log_gamma * chunk)
    lane = lambda a: jnp.broadcast_to(a[..., None], a.shape + (LANES,))
    return decay, lane(q_decay), lane(k_decay), lane(chunk_decay)


def _prompt_retention(qr, kr, rv, ret_norm):
    b, t, _ = qr.shape
    decay, q_decay, k_decay, chunk_decay = _retention_consts(RET_CHUNK)
    seq = pl.BlockSpec((1, RET_CHUNK, _RET_W), lambda bb, c: (bb, c, 0))
    return pl.pallas_call(
        _ret_prompt_kernel,
        grid=(b, t // RET_CHUNK),
        in_specs=[seq, seq, seq, _const_spec(decay.shape), _const_spec(q_decay.shape),
                  _const_spec(k_decay.shape), _const_spec(chunk_decay.shape),
                  _const_spec(ret_norm.shape)],
        out_specs=[seq, pl.BlockSpec((1, RET_HEADS, RET_DK, RET_DV), lambda bb, c: (bb, 0, 0, 0))],
        out_shape=[jax.ShapeDtypeStruct((b, t, _RET_W), F32),
                   jax.ShapeDtypeStruct((b, RET_HEADS, RET_DK, RET_DV), F32)],
        scratch_shapes=[pltpu.VMEM((RET_HEADS, RET_DK, RET_DV), F32)],
        compiler_params=pltpu.CompilerParams(dimension_semantics=("arbitrary", "arbitrary")),
        name="prompt_retention",
    )(qr, kr, rv, decay, q_decay, k_decay, chunk_decay, ret_norm)


def _ret_sample_kernel(q_ref, k_ref, v_ref, s_ref, gam_ref, g_ref, o_ref, s_out_ref):
    rows = lax.broadcasted_iota(jnp.int32, (RET_DK, RET_DK), 0)
    cols = lax.broadcasted_iota(jnp.int32, (RET_DK, RET_DK), 1)
    eye = rows == cols

    def column(r):
        return jnp.sum(jnp.where(eye, jnp.broadcast_to(r, (RET_DK, RET_DK)), 0.0),
                       axis=1, keepdims=True)

    heads = range(RET_HEADS)
    lanes = [slice(hd * RET_DK, (hd + 1) * RET_DK) for hd in heads]
    gams = [gam_ref[hd:hd + 1, :] for hd in heads]
    for i in range(RET_SEQS_PER_STEP):
        qs = [q_ref[i, :, sl] for sl in lanes]
        ks = [k_ref[i, :, sl] for sl in lanes]
        vs = [v_ref[i, :, sl] for sl in lanes]
        q_cols = [column(q) for q in qs]
        k_cols = [column(k) for k in ks]
        qk = [jnp.sum(q * k, axis=-1, keepdims=True) for q, k in zip(qs, ks)]
        states = [s_ref[0, i, hd] for hd in heads]
        outs = [qk[hd] * vs[hd] + jnp.sum(q_cols[hd] * states[hd], axis=0, keepdims=True) * gams[hd]
                for hd in heads]
        for hd in heads:
            s_out_ref[0, i, hd] = gams[hd] * states[hd] + k_cols[hd] * vs[hd]
        mus = [jnp.mean(o, axis=-1, keepdims=True) for o in outs]
        devs = [o - mu for o, mu in zip(outs, mus)]
        variances = [jnp.mean(d * d, axis=-1, keepdims=True) for d in devs]
        for hd in heads:
            o_ref[i, :, lanes[hd]] = devs[hd] * lax.rsqrt(variances[hd] + EPS) * g_ref[hd:hd + 1, :]


def _sample_retention(qr, kr, rv, state, ret_norm):
    n = qr.shape[0]
    gam = jnp.broadcast_to(jnp.exp(_log_gamma() * 1.0)[:, None], (RET_HEADS, LANES))
    g = RET_SEQS_PER_STEP
    tok = pl.BlockSpec((g, 1, _RET_W), lambda i: (i, 0, 0))
    st = pl.BlockSpec((1, g, RET_HEADS, RET_DK, RET_DV), lambda i: (0, i, 0, 0, 0))
    r3 = lambda a: a.reshape(n, 1, _RET_W)
    o, s_new = pl.pallas_call(
        _ret_sample_kernel,
        grid=(n // g,),
        in_specs=[tok, tok, tok, st, _const_spec(gam.shape), _const_spec(ret_norm.shape)],
        out_specs=[tok, st],
        out_shape=[jax.ShapeDtypeStruct((n, 1, _RET_W), F32),
                   jax.ShapeDtypeStruct(state.shape, F32)],
        compiler_params=pltpu.CompilerParams(dimension_semantics=("arbitrary",)),
        name="sample_retention",
    )(r3(qr), r3(kr), r3(rv), state, gam, ret_norm)
    return o.reshape(n, _RET_W), s_new


def _absorb_q_kernel(qt_ref, wt_ref, o_ref):
    for hd in range(MLA_HEADS):
        o_ref[hd] = _dot(wt_ref[hd], qt_ref[hd * HEAD_PAD:(hd + 1) * HEAD_PAD, :])


def _absorb_q(qt, w_absorb_t):
    n = qt.shape[1]
    return pl.pallas_call(
        _absorb_q_kernel,
        out_shape=jax.ShapeDtypeStruct((MLA_HEADS, KEY_W, n), F32),
        name="sample_absorb_q",
    )(qt, w_absorb_t)


def _unabsorb_kernel(o_ref, w_ref, out_ref):
    acc = _dot(o_ref[0].astype(BF16), w_ref[0])
    for hd in range(1, MLA_HEADS):
        acc = acc + _dot(o_ref[hd].astype(BF16), w_ref[hd])
    out_ref[...] = acc.astype(out_ref.dtype)


def _unabsorb(o_lat, w_unabsorb):
    n = o_lat.shape[1]
    return pl.pallas_call(
        _unabsorb_kernel,
        out_shape=jax.ShapeDtypeStruct((n, _ATT_W), BF16),
        name="sample_unabsorb",
    )(o_lat, w_unabsorb)


def _paged_kernel(pt_ref, q_ref, knew_ref, ckv_hbm, kpet_hbm, o_ref, buf, kpe_buf, sem):
    b = pl.program_id(0)
    n_seq = pl.num_programs(0)
    n_chunks = pt_ref.shape[1] // PAGES_PER_STEP
    assert n_chunks == PAGE_SLOTS
    ahead = PAGE_SLOTS - 1
    chunk_keys = PAGES_PER_STEP * PAGE_SIZE
    group_keys = chunk_keys // PAGE_GROUPS

    def page_copies(bb, cc, sl):
        out = []
        for j in range(PAGES_PER_STEP):
            pg = pt_ref[bb, cc * PAGES_PER_STEP + j]
            keys = pl.ds(j * PAGE_SIZE, PAGE_SIZE)
            out.append(pltpu.make_async_copy(ckv_hbm.at[0, pg], buf.at[sl, keys], sem.at[sl, 0]))
            out.append(pltpu.make_async_copy(kpet_hbm.at[0, pg], kpe_buf.at[sl, :, keys],
                                             sem.at[sl, 1]))
        return out

    @pl.when(b == 0)
    def _():
        for c in range(ahead):
            for cp in page_copies(0, c, c):
                cp.start()

    b_next = jnp.minimum(b + 1, n_seq - 1)

    q = q_ref[0]
    q_lat = q[:, 0:KV_LORA]
    q_pe = q[:, KV_LORA:KV_LORA + MLA_ROPE]

    def scores(sl, g):
        keys = slice(g * group_keys, (g + 1) * group_keys)
        return _dot_nt(q_lat, buf[sl, keys, :]) + _dot(q_pe, kpe_buf[sl, :, keys])

    m_run = jnp.full((MLA_HEADS, 1), NEG_BIG, F32)
    l_run = jnp.zeros((MLA_HEADS, 1), F32)
    acc_run = jnp.zeros((MLA_HEADS, KV_LORA), F32)
    for c in range(n_chunks):
        sl = c
        nxt = c + ahead
        prefetch = (page_copies(b, nxt, nxt) if nxt < n_chunks
                    else page_copies(b_next, nxt - n_chunks, nxt - n_chunks))
        for cp in page_copies(b, c, sl):
            cp.wait()

        parts = []
        per_group = len(prefetch) // PAGE_GROUPS
        s_next = scores(sl, 0)
        for g in range(PAGE_GROUPS):
            s = s_next
            if g + 1 < PAGE_GROUPS:
                s_next = scores(sl, g + 1)
            for cp in prefetch[g * per_group:(g + 1) * per_group]:
                cp.start()
            m_g = jnp.max(s, axis=-1, keepdims=True)
            p = jnp.exp2(s - m_g)
            l_g = jnp.sum(p, axis=-1, keepdims=True)
            keys = slice(g * group_keys, (g + 1) * group_keys)
            parts.append((m_g, l_g, _dot(p, buf[sl, keys, :])))

        m_new = m_run
        for m_g, _, _ in parts:
            m_new = jnp.maximum(m_new, m_g)
        a_run = jnp.exp2(m_run - m_new)
        l_run = a_run * l_run
        acc_run = a_run * acc_run
        for m_g, l_g, o_g in parts:
            a_g = jnp.exp2(m_g - m_new)
            l_run = l_run + a_g * l_g
            acc_run = acc_run + a_g * o_g
        m_run = m_new

    kn = knew_ref[0]
    s_self = jnp.sum(q * kn, axis=-1, keepdims=True)
    m_fin = jnp.maximum(m_run, s_self)
    a = jnp.exp2(m_run - m_fin)
    p_self = jnp.exp2(s_self - m_fin)
    o_ref[0] = (a * acc_run + p_self * kn[:, 0:KV_LORA]) / (a * l_run + p_self)

    @pl.when(b == n_seq - 1)
    def _():
        for c in range(ahead):
            for cp in page_copies(b, c, c):
                cp.wait()


def _paged_attention(page_table, q_abs, k_new, cache_ckv, cache_kpe_t):
    n, n_pages = page_table.shape
    chunk_keys = PAGES_PER_STEP * PAGE_SIZE
    grid_spec = pltpu.PrefetchScalarGridSpec(
        num_scalar_prefetch=1,
        grid=(n,),
        in_specs=[
            pl.BlockSpec((1, MLA_HEADS, KEY_W), lambda b, pt: (b, 0, 0)),
            pl.BlockSpec((1, 1, KEY_W), lambda b, pt: (b, 0, 0)),
            pl.BlockSpec(memory_space=pl.ANY),
            pl.BlockSpec(memory_space=pl.ANY),
        ],
        out_specs=pl.BlockSpec((1, MLA_HEADS, KV_LORA), lambda b, pt: (b, 0, 0)),
        scratch_shapes=[pltpu.VMEM((PAGE_SLOTS, chunk_keys, KV_LORA), F32),
                        pltpu.VMEM((PAGE_SLOTS, MLA_ROPE, chunk_keys), F32),
                        pltpu.SemaphoreType.DMA((PAGE_SLOTS, 2))],
    )
    return pl.pallas_call(
        _paged_kernel,
        grid_spec=grid_spec,
        out_shape=jax.ShapeDtypeStruct((n, MLA_HEADS, KV_LORA), F32),
        compiler_params=pltpu.CompilerParams(dimension_semantics=("arbitrary",),
                                             vmem_limit_bytes=VMEM_LIMIT),
        name="sample_paged_attention",
    )(page_table, q_abs, k_new, cache_ckv, cache_kpe_t)


def _back_kernel(h_ref, oatt_ref, oret_ref, p_ref,
                 gmix_ref, wgate_ref, wba_ref, wbr_ref, wout_ref,
                 g2_ref, wg_ref, wu_ref, wd_ref, gple_ref, wpg_ref, wpp_ref, gfin_ref, y_ref):
    h = h_ref[...]
    un = _rms(h, gmix_ref[...]).astype(BF16)
    gates = _dot(un, wgate_ref[...])
    rg = gates[:, :_RET_W]
    ga = gates[:, _RET_W:_RET_W + D_MODEL]
    gr = gates[:, _RET_W + D_MODEL:]
    o_ret = (rg * jax.nn.sigmoid(rg) * oret_ref[...]).astype(BF16)
    merged = (jax.nn.sigmoid(ga) * _dot(oatt_ref[...], wba_ref[...])
              + jax.nn.sigmoid(gr) * _dot(o_ret, wbr_ref[...]))
    h = h + _dot(merged.astype(BF16), wout_ref[...])
    h = _swiglu_half(h, g2_ref[...], wg_ref[...], wu_ref[...], wd_ref[...])
    gate = jax.nn.sigmoid(_dot(_rms(h, gple_ref[...]).astype(BF16), wpg_ref[...]))
    h = h + gate * _dot(p_ref[...].astype(BF16), wpp_ref[...])
    y_ref[...] = _rms(h, gfin_ref[...])


def _back_stage(h, o_att, o_ret, p_emb, w, tile):
    n = h.shape[0]
    row = lambda width: pl.BlockSpec((tile, width), lambda i: (i, 0))
    weights = [w['mix_norm'], w['w_in_gates'], w['w_branch_att'], w['w_branch_ret'], w['w_out'],
               w['ffn2_norm'], w['ffn2_w_gate'], w['ffn2_w_up'], w['ffn2_w_down'],
               w['ple_norm'], w['w_ple_gate'], w['w_ple_proj'], w['final_norm']]
    return pl.pallas_call(
        _back_kernel,
        grid=(n // tile,),
        in_specs=[row(D_MODEL), row(_ATT_W), row(_RET_W), row(PLE_DIM)]
        + [_const_spec(a.shape) for a in weights],
        out_specs=row(D_MODEL),
        out_shape=jax.ShapeDtypeStruct((n, D_MODEL), F32),
        compiler_params=pltpu.CompilerParams(dimension_semantics=("arbitrary",),
                                             vmem_limit_bytes=VMEM_LIMIT),
        name="back_stage",
    )(h, o_att, o_ret, p_emb, *weights)


def _rope_tables(pos):
    pos = pos.astype(F32)[:, None]
    n = pos.shape[0]
    inv_m = ROPE_THETA ** (-jnp.arange(ROPE_HALF, dtype=F32) / ROPE_HALF)
    cos_m, sin_m = jnp.cos(pos * inv_m[None, :]), jnp.sin(pos * inv_m[None, :])
    z = lambda width: jnp.zeros((n, width), F32)
    tail = HEAD_PAD - ROPE_OFF - MLA_ROPE
    cm = jnp.concatenate([jnp.ones((n, ROPE_OFF), F32), cos_m, cos_m, z(tail)], axis=1)
    s1 = jnp.concatenate([z(ROPE_OFF), -sin_m, z(ROPE_HALF), z(tail)], axis=1)
    s2 = jnp.concatenate([z(ROPE_OFF), z(ROPE_HALF), sin_m, z(tail)], axis=1)
    half = RET_DK // 2
    inv_r = ROPE_THETA ** (-jnp.arange(half, dtype=F32) / half)
    cos_r, sin_r = jnp.cos(pos * inv_r[None, :]), jnp.sin(pos * inv_r[None, :])
    cr = jnp.concatenate([cos_r, cos_r], axis=1)
    sr = jnp.concatenate([-sin_r, sin_r], axis=1)
    return cm, s1, s2, cr, sr, jnp.transpose(cos_m), jnp.transpose(sin_m)


def _layer_weights(i, ffn1_norm, ffn1_w_gate, ffn1_w_up, ffn1_w_down, mix_norm, w_in, q_a_norm, w_q_b,
                   kv_a_norm, w_kv_b, ret_norm, w_branch_att, w_branch_ret, w_out,
                   ffn2_norm, ffn2_w_gate, ffn2_w_up, ffn2_w_down, ple_norm, w_ple_gate, w_ple_proj,
                   final_norm):
    vec = lambda a: a.reshape(1, -1)
    bf = lambda a: a.astype(BF16)
    win = w_in[i]
    zcols = lambda width: jnp.zeros((D_MODEL, width), F32)
    w_in_front = jnp.concatenate(
        [win[:, :_OFF_KPE], zcols(ROPE_OFF), win[:, _OFF_KPE:_OFF_RQ],
         zcols(HEAD_PAD - ROPE_OFF - MLA_ROPE), win[:, _OFF_RQ:_OFF_RG]], axis=1)
    wqb = w_q_b[i].reshape(Q_LORA, MLA_HEADS, MLA_NOPE + MLA_ROPE)
    wqb = jnp.pad(wqb, ((0, 0), (0, 0), (0, HEAD_PAD - MLA_NOPE - MLA_ROPE)))
    wkv = w_kv_b[i].reshape(KV_LORA, MLA_HEADS, MLA_NOPE + MLA_V)
    w_uk, w_uv = wkv[..., :MLA_NOPE], wkv[..., MLA_NOPE:]
    w_k = jnp.pad(w_uk, ((0, 0), (0, 0), (0, HEAD_PAD - MLA_NOPE)))
    w_absorb = jnp.zeros((MLA_HEADS, HEAD_PAD, KEY_W), F32)
    w_absorb = w_absorb.at[:, :MLA_NOPE, :KV_LORA].set(jnp.transpose(w_uk, (1, 2, 0)))
    w_absorb = w_absorb.at[:, MLA_NOPE:MLA_NOPE + MLA_ROPE, KV_LORA:KV_LORA + MLA_ROPE].set(
        jnp.eye(MLA_ROPE, dtype=F32)[None])
    w_unabsorb = jnp.zeros((MLA_HEADS, KV_LORA, _ATT_W), F32)
    for hd in range(MLA_HEADS):
        w_unabsorb = w_unabsorb.at[hd, :, hd * MLA_V:(hd + 1) * MLA_V].set(w_uv[:, hd, :])
    return {
        'ffn1_norm': vec(ffn1_norm[i]), 'ffn1_w_gate': bf(ffn1_w_gate[i]), 'ffn1_w_up': bf(ffn1_w_up[i]),
        'ffn1_w_down': bf(ffn1_w_down[i]), 'mix_norm': vec(mix_norm[i]),
        'w_in_front': bf(w_in_front), 'w_in_gates': bf(win[:, _OFF_RG:]),
        'q_a_norm': vec(q_a_norm[i]),
        'w_q_bt': bf(jnp.transpose(wqb.reshape(Q_LORA, MLA_HEADS * HEAD_PAD))),
        'kv_a_norm': vec(kv_a_norm[i]), 'w_k': bf(w_k.reshape(KV_LORA, MLA_HEADS * HEAD_PAD)),
        'w_vt': bf(jnp.pad(jnp.transpose(w_uv, (1, 2, 0)), ((0, 0), (0, V_ROWS - MLA_V), (0, 0)))
                   .reshape(_VT_ROWS, KV_LORA)),
        'w_absorb_t': bf(jnp.transpose(w_absorb, (0, 2, 1))), 'w_unabsorb': bf(w_unabsorb),
        'ret_norm': ret_norm[i],
        'w_branch_att': bf(w_branch_att[i]), 'w_branch_ret': bf(w_branch_ret[i]), 'w_out': bf(w_out[i]),
        'ffn2_norm': vec(ffn2_norm[i]), 'ffn2_w_gate': bf(ffn2_w_gate[i]), 'ffn2_w_up': bf(ffn2_w_up[i]),
        'ffn2_w_down': bf(ffn2_w_down[i]), 'ple_norm': vec(ple_norm[i]),
        'w_ple_gate': bf(w_ple_gate[i]), 'w_ple_proj': bf(w_ple_proj[i]),
        'final_norm': vec(final_norm),
    }


def kernel(x_prompt, x_sample, cache_ckv, cache_kpe, state_ret, page_table, p_prompt, p_sample, ffn1_norm, ffn1_w_gate, ffn1_w_up, ffn1_w_down, mix_norm, w_in, q_a_norm, w_q_b, kv_a_norm, w_kv_b, ret_norm, w_branch_att, w_branch_ret, w_out, ffn2_norm, ffn2_w_gate, ffn2_w_up, ffn2_w_down, ple_norm, w_ple_gate, w_ple_proj, final_norm):
    batch, seq, _ = x_prompt.shape
    n_dec, dec_seq, _ = x_sample.shape
    depth = w_in.shape[0]
    assert dec_seq == 1 and depth == 1
    n_past = page_table.shape[1] * PAGE_SIZE

    w = _layer_weights(0, ffn1_norm, ffn1_w_gate, ffn1_w_up, ffn1_w_down, mix_norm, w_in, q_a_norm,
                       w_q_b, kv_a_norm, w_kv_b, ret_norm, w_branch_att, w_branch_ret, w_out,
                       ffn2_norm, ffn2_w_gate, ffn2_w_up, ffn2_w_down, ple_norm, w_ple_gate,
                       w_ple_proj, final_norm)

    tabs_p = _rope_tables(jnp.arange(seq, dtype=jnp.int32))
    (h_p, qt_p, k_p, vt_p, ckv_p, kpe_p, qr_p, kr_p, rv_p) = _front_stage(
        x_prompt.reshape(batch * seq, D_MODEL), tabs_p, w, TOKEN_TILE, seq // TOKEN_TILE)
    bt = lambda a: a.reshape(batch, seq, a.shape[-1])
    o_att_p = _prompt_attention(qt_p, k_p, vt_p, batch, seq)
    o_ret_p, ret_p = _prompt_retention(bt(qr_p), bt(kr_p), bt(rv_p), w['ret_norm'])
    y_p = _back_stage(h_p, o_att_p, o_ret_p.reshape(batch * seq, -1),
                      p_prompt.reshape(batch * seq, PLE_DIM), w, TOKEN_TILE)

    tabs_1 = _rope_tables(jnp.full((1,), n_past, jnp.int32))
    tabs_s = (tuple(jnp.broadcast_to(t, (n_dec, LANES)) for t in tabs_1[:5])
              + tuple(jnp.broadcast_to(t, (ROPE_HALF, n_dec)) for t in tabs_1[5:]))
    (h_s, qt_s, _, _, ckv_s, kpe_s, qr_s, kr_s, rv_s) = _front_stage(
        x_sample.reshape(n_dec, D_MODEL), tabs_s, w, n_dec, 1)
    q_abs = jnp.transpose(_absorb_q(qt_s, w['w_absorb_t']), (2, 0, 1))
    k_new = jnp.concatenate([ckv_s, kpe_s[:, ROPE_OFF:ROPE_OFF + MLA_ROPE],
                             jnp.zeros((n_dec, KEY_W - KV_LORA - MLA_ROPE), F32)], axis=1)
    o_lat = _paged_attention(page_table, q_abs, k_new.reshape(n_dec, 1, KEY_W), cache_ckv,
                             jnp.swapaxes(cache_kpe, 2, 3))
    o_att_s = _unabsorb(jnp.transpose(o_lat, (1, 0, 2)), w['w_unabsorb'])
    o_ret_s, ret_s = _sample_retention(qr_s, kr_s, rv_s, state_ret, w['ret_norm'])
    y_s = _back_stage(h_s, o_att_s, o_ret_s, p_sample.reshape(n_dec, PLE_DIM), w, n_dec)

    kpe_cols = slice(ROPE_OFF, ROPE_OFF + MLA_ROPE)
    return (y_p.reshape(batch, seq, D_MODEL),
            y_s.reshape(n_dec, 1, D_MODEL),
            ckv_p.reshape(1, batch, seq, KV_LORA),
            kpe_p[:, kpe_cols].reshape(1, batch, seq, MLA_ROPE),
            ret_p[None],
            ckv_s.reshape(1, n_dec, 1, KV_LORA),
            kpe_s[:, kpe_cols].reshape(1, n_dec, 1, MLA_ROPE),
            ret_s)
```

```python
import jax
import jax.numpy as jnp
import numpy as np
from jax import lax
from jax.experimental import pallas as pl
from jax.experimental.pallas import tpu as pltpu

F32 = jnp.float32
BF16 = jnp.bfloat16

D_MODEL = 1024
D_FF = 2816
PLE_DIM = 256
MLA_HEADS = 8
MLA_NOPE = 64
MLA_ROPE = 32
MLA_V = 64
Q_LORA = 256
KV_LORA = 128
RET_HEADS = 4
RET_DK = 128
RET_DV = 128
PAGE_SIZE = 128
ROPE_THETA = 10000.0
EPS = 1e-6

LANES = 128
HEAD_PAD = LANES
ROPE_OFF = MLA_NOPE
ROPE_HALF = MLA_ROPE // 2
SOFTMAX_SCALE = (MLA_NOPE + MLA_ROPE) ** -0.5
LOG2E = 1.4426950408889634
Q_SCALE = SOFTMAX_SCALE * LOG2E
NEG_BIG = -1e30

TOKEN_TILE = 512
ATTN_TILE = 512
RET_CHUNK = 512
PAGES_PER_STEP = 32
PAGE_GROUPS = 4
PAGE_SLOTS = 4
RET_SEQS_PER_STEP = 8
KEY_W = 2 * LANES
VMEM_LIMIT = 60 * 1024 * 1024

_OFF_CQ = 0
_OFF_CKV = _OFF_CQ + Q_LORA
_OFF_KPE = _OFF_CKV + KV_LORA
_OFF_RQ = _OFF_KPE + MLA_ROPE
_OFF_RK = _OFF_RQ + RET_HEADS * RET_DK
_OFF_RV = _OFF_RK + RET_HEADS * RET_DK
_OFF_RG = _OFF_RV + RET_HEADS * RET_DV
_RET_W = RET_HEADS * RET_DK
_ATT_W = MLA_HEADS * MLA_V
V_ROWS = MLA_V + 16
_VT_ROWS = MLA_HEADS * V_ROWS


def _rms(x, g):
    return x * lax.rsqrt(jnp.mean(x * x, axis=-1, keepdims=True) + EPS) * g


def _dot(a, b):
    return jnp.dot(a, b, preferred_element_type=F32)


def _dot_nt(a, b):
    return lax.dot_general(a, b, (((1,), (1,)), ((), ())), preferred_element_type=F32)


def _swiglu_half(x, g, wg, wu, wd):
    xn = _rms(x, g).astype(BF16)
    gate = _dot(xn, wg)
    up = _dot(xn, wu)
    act = (gate * jax.nn.sigmoid(gate) * up).astype(BF16)
    return x + 0.5 * _dot(act, wd)


def _const_spec(shape):
    n = len(shape)
    return pl.BlockSpec(shape, lambda *_: (0,) * n, pipeline_mode=pl.Buffered(1))


def _front_kernel(x_ref, cm_ref, s1_ref, s2_ref, cr_ref, sr_ref, ct_ref, st_ref,
                  g1_ref, wg_ref, wu_ref, wd_ref, gmix_ref, win_ref,
                  gqa_ref, wqbt_ref, gkv_ref, wk_ref, wvt_ref,
                  h_ref, qt_ref, k_ref, vt_ref, ckv_ref, kpe_ref, qr_ref, kr_ref, rv_ref):
    x = x_ref[...]
    h = _swiglu_half(x, g1_ref[...], wg_ref[...], wu_ref[...], wd_ref[...])
    h_ref[...] = h
    un = _rms(h, gmix_ref[...]).astype(BF16)
    z = _dot(un, win_ref[...])

    cm, s1, s2 = cm_ref[...], s1_ref[...], s2_ref[...]

    def mla_rope(t):
        return (t * cm + pltpu.roll(t, LANES - ROPE_HALF, 1) * s1
                + pltpu.roll(t, ROPE_HALF, 1) * s2)

    cq = z[:, _OFF_CQ:_OFF_CQ + Q_LORA]
    qt = _dot_nt(wqbt_ref[...], _rms(cq, gqa_ref[...]).astype(BF16))
    ct, st = ct_ref[...], st_ref[...]
    c_kv = _rms(z[:, Q_LORA:Q_LORA + KV_LORA], gkv_ref[...])
    ckv_ref[...] = c_kv
    c_kv_b = c_kv.astype(BF16)
    k_nope = _dot(c_kv_b, wk_ref[...])
    vt = _dot_nt(wvt_ref[...], c_kv_b)
    sum_row = lax.broadcasted_iota(jnp.int32, vt.shape, 0) % V_ROWS >= MLA_V
    vt_ref[...] = jnp.where(sum_row, 1.0, vt).astype(BF16)
    k_pe = mla_rope(z[:, Q_LORA + KV_LORA:Q_LORA + KV_LORA + LANES])
    kpe_ref[...] = k_pe
    for hd in range(MLA_HEADS):
        sl = slice(hd * HEAD_PAD, (hd + 1) * HEAD_PAD)
        k_ref[:, sl] = (k_nope[:, sl] + k_pe).astype(BF16)
        r0 = hd * HEAD_PAD + ROPE_OFF
        x1 = qt[r0:r0 + ROPE_HALF]
        x2 = qt[r0 + ROPE_HALF:r0 + MLA_ROPE]
        qt_ref[hd * HEAD_PAD:r0, :] = (qt[hd * HEAD_PAD:r0] * Q_SCALE).astype(BF16)
        qt_ref[r0:r0 + ROPE_HALF, :] = ((x1 * ct - x2 * st) * Q_SCALE).astype(BF16)
        qt_ref[r0 + ROPE_HALF:r0 + MLA_ROPE, :] = ((x1 * st + x2 * ct) * Q_SCALE).astype(BF16)
        qt_ref[r0 + MLA_ROPE:(hd + 1) * HEAD_PAD, :] = jnp.zeros(
            (HEAD_PAD - ROPE_OFF - MLA_ROPE, qt.shape[1]), BF16)

    cr, sr = cr_ref[...], sr_ref[...]
    base = Q_LORA + KV_LORA + LANES
    for hd in range(RET_HEADS):
        sl = slice(hd * RET_DK, (hd + 1) * RET_DK)
        rq = z[:, base + hd * RET_DK:base + (hd + 1) * RET_DK]
        rk = z[:, base + _RET_W + hd * RET_DK:base + _RET_W + (hd + 1) * RET_DK]
        qr_ref[:, sl] = rq * cr + pltpu.roll(rq, RET_DK // 2, 1) * sr
        kr_ref[:, sl] = (rk * cr + pltpu.roll(rk, RET_DK // 2, 1) * sr) * (RET_DK ** -0.5)
    rv_ref[...] = z[:, base + 2 * _RET_W:base + 3 * _RET_W]


def _front_stage(x, tables, w, tile, table_tiles):
    n = x.shape[0]
    row = lambda width: pl.BlockSpec((tile, width), lambda i: (i, 0))
    col = lambda height: pl.BlockSpec((height, tile), lambda i: (0, i))
    tab = pl.BlockSpec((tile, LANES), lambda i: (i % table_tiles, 0))
    tab_t = pl.BlockSpec((ROPE_HALF, tile), lambda i: (0, i % table_tiles))
    weights = [w['ffn1_norm'], w['ffn1_w_gate'], w['ffn1_w_up'], w['ffn1_w_down'], w['mix_norm'],
               w['w_in_front'], w['q_a_norm'], w['w_q_bt'], w['kv_a_norm'], w['w_k'], w['w_vt']]
    outs = [((n, D_MODEL), F32, row(D_MODEL)),
            ((MLA_HEADS * HEAD_PAD, n), BF16, col(MLA_HEADS * HEAD_PAD)),
            ((n, MLA_HEADS * HEAD_PAD), BF16, row(MLA_HEADS * HEAD_PAD)),
            ((_VT_ROWS, n), BF16, col(_VT_ROWS)),
            ((n, KV_LORA), F32, row(KV_LORA)),
            ((n, LANES), F32, row(LANES)),
            ((n, _RET_W), F32, row(_RET_W)),
            ((n, _RET_W), F32, row(_RET_W)),
            ((n, _RET_W), F32, row(_RET_W))]
    return pl.pallas_call(
        _front_kernel,
        grid=(n // tile,),
        in_specs=[row(D_MODEL)] + [tab] * 5 + [tab_t] * 2 + [_const_spec(a.shape) for a in weights],
        out_specs=[spec for _, _, spec in outs],
        out_shape=[jax.ShapeDtypeStruct(shape, dt) for shape, dt, _ in outs],
        compiler_params=pltpu.CompilerParams(dimension_semantics=("arbitrary",),
                                             vmem_limit_bytes=VMEM_LIMIT),
        name="front_stage",
    )(x, *tables, *weights)


def _attn_kernel(qi_ref, ki_ref, qt_ref, k_ref, vt_ref, o_ref, m_sc, acc_sc):
    t = pl.program_id(1)
    qi = qi_ref[t]
    ki = ki_ref[t]

    @pl.when(ki == 0)
    def _():
        m_sc[...] = jnp.full(m_sc.shape, NEG_BIG, F32)
        acc_sc[...] = jnp.zeros(acc_sc.shape, F32)

    def scores(hd):
        kh = k_ref[:, hd * HEAD_PAD:(hd + 1) * HEAD_PAD]
        return _dot(kh, qt_ref[hd * HEAD_PAD:(hd + 1) * HEAD_PAD, :])

    def update(masked):
        st_next = scores(0)
        for hd in range(MLA_HEADS):
            st = st_next
            if hd + 1 < MLA_HEADS:
                st_next = scores(hd + 1)
            if masked:
                key = lax.broadcasted_iota(jnp.int32, st.shape, 0)
                qry = lax.broadcasted_iota(jnp.int32, st.shape, 1)
                st = jnp.where(key <= qry, st, NEG_BIG)
            m_prev = m_sc[hd]
            m_new = jnp.maximum(m_prev, jnp.max(st, axis=0, keepdims=True))
            alpha = jnp.exp2(m_prev - m_new)
            p = jnp.exp2(st - m_new)
            acc_sc[hd] = alpha * acc_sc[hd] + _dot(vt_ref[hd * V_ROWS:(hd + 1) * V_ROWS, :],
                                                    p.astype(BF16))
            m_sc[hd] = m_new

    @pl.when(ki < qi)
    def _():
        update(False)

    @pl.when(ki == qi)
    def _():
        update(True)
        acc = acc_sc[...]
        o_t = (acc[:, :MLA_V, :] / acc[:, MLA_V:MLA_V + 1, :]).reshape(_ATT_W, ATTN_TILE)
        o_ref[...] = jnp.transpose(o_t).astype(o_ref.dtype)


def _prompt_attention(qt, k, vt, batch, seq):
    nt = seq // ATTN_TILE
    qi_list = np.array([i for i in range(nt) for _ in range(i + 1)], np.int32)
    ki_list = np.array([j for i in range(nt) for j in range(i + 1)], np.int32)
    qk_w = MLA_HEADS * HEAD_PAD
    grid_spec = pltpu.PrefetchScalarGridSpec(
        num_scalar_prefetch=2,
        grid=(batch, len(qi_list)),
        in_specs=[
            pl.BlockSpec((qk_w, ATTN_TILE), lambda bb, s, qi, ki: (0, bb * nt + qi[s])),
            pl.BlockSpec((ATTN_TILE, qk_w), lambda bb, s, qi, ki: (bb * nt + ki[s], 0)),
            pl.BlockSpec((_VT_ROWS, ATTN_TILE), lambda bb, s, qi, ki: (0, bb * nt + ki[s])),
        ],
        out_specs=pl.BlockSpec((ATTN_TILE, _ATT_W), lambda bb, s, qi, ki: (bb * nt + qi[s], 0)),
        scratch_shapes=[pltpu.VMEM((MLA_HEADS, 1, ATTN_TILE), F32),
                        pltpu.VMEM((MLA_HEADS, V_ROWS, ATTN_TILE), F32)],
    )
    return pl.pallas_call(
        _attn_kernel,
        grid_spec=grid_spec,
        out_shape=jax.ShapeDtypeStruct((batch * seq, _ATT_W), BF16),
        compiler_params=pltpu.CompilerParams(dimension_semantics=("arbitrary", "arbitrary"),
                                             vmem_limit_bytes=VMEM_LIMIT),
        name="prompt_attention",
    )(jnp.asarray(qi_list), jnp.asarray(ki_list), qt, k, vt)


def _head_layernorm(o, g):
    mu = jnp.mean(o, axis=-1, keepdims=True)
    d = o - mu
    var = jnp.mean(d * d, axis=-1, keepdims=True)
    return d * lax.rsqrt(var + EPS) * g


def _ret_prompt_kernel(q_ref, k_ref, v_ref, dec_ref, qd_ref, kd_ref, gc_ref, g_ref,
                       o_ref, s_out_ref, s_sc):
    c = pl.program_id(1)

    @pl.when(c == 0)
    def _():
        s_sc[...] = jnp.zeros(s_sc.shape, F32)

    for hd in range(RET_HEADS):
        sl = slice(hd * RET_DK, (hd + 1) * RET_DK)
        q = q_ref[0, :, sl]
        k = k_ref[0, :, sl]
        vb = v_ref[0, :, sl].astype(BF16)
        qb = q.astype(BF16)
        state = s_sc[hd]
        scores = _dot_nt(qb, k.astype(BF16)) * dec_ref[hd]
        inner = _dot(scores.astype(BF16), vb)
        cross = _dot(qb, state.astype(BF16)) * qd_ref[hd]
        o_ref[0, :, sl] = _head_layernorm(inner + cross, g_ref[hd:hd + 1, :])
        k_dec_t = jnp.transpose(k * kd_ref[hd]).astype(BF16)
        s_sc[hd] = gc_ref[hd:hd + 1, :] * state + _dot(k_dec_t, vb)

    @pl.when(c == pl.num_programs(1) - 1)
    def _():
        s_out_ref[0] = s_sc[...]


def _log_gamma():
    return np.log1p(-np.exp2(-5.0 - np.arange(RET_HEADS, dtype=np.float64)))


def _retention_consts(chunk):
    log_gamma = _log_gamma()
    idx = np.arange(chunk, dtype=np.float64)
    diff = idx[:, None] - idx[None, :]
    decay = np.where(diff[None] >= 0,
                     np.exp(log_gamma[:, None, None] * np.maximum(diff, 0.0)[None]), 0.0)
    q_decay = np.exp(log_gamma[:, None] * (idx + 1.0)[None, :])
    k_decay = np.exp(log_gamma[:, None] * (chunk - 1.0 - idx)[None, :])
    chunk_decay = np.exp(log_gamma * chunk)
    lane = lambda a: np.broadcast_to(a[..., None], a.shape + (LANES,))
    tables = (decay, lane(q_decay), lane(k_decay), lane(chunk_decay))
    return tuple(np.ascontiguousarray(t, dtype=np.float32) for t in tables)


def _prompt_retention(qr, kr, rv, ret_norm):
    b, t, _ = qr.shape
    decay, q_decay, k_decay, chunk_decay = _retention_consts(RET_CHUNK)
    seq = pl.BlockSpec((1, RET_CHUNK, _RET_W), lambda bb, c: (bb, c, 0))
    return pl.pallas_call(
        _ret_prompt_kernel,
        grid=(b, t // RET_CHUNK),
        in_specs=[seq, seq, seq, _const_spec(decay.shape), _const_spec(q_decay.shape),
                  _const_spec(k_decay.shape), _const_spec(chunk_decay.shape),
                  _const_spec(ret_norm.shape)],
        out_specs=[seq, pl.BlockSpec((1, RET_HEADS, RET_DK, RET_DV), lambda bb, c: (bb, 0, 0, 0))],
        out_shape=[jax.ShapeDtypeStruct((b, t, _RET_W), F32),
                   jax.ShapeDtypeStruct((b, RET_HEADS, RET_DK, RET_DV), F32)],
        scratch_shapes=[pltpu.VMEM((RET_HEADS, RET_DK, RET_DV), F32)],
        compiler_params=pltpu.CompilerParams(dimension_semantics=("arbitrary", "arbitrary")),
        name="prompt_retention",
    )(qr, kr, rv, decay, q_decay, k_decay, chunk_decay, ret_norm)


def _ret_sample_kernel(q_ref, k_ref, v_ref, s_ref, gam_ref, g_ref, o_ref, s_out_ref):
    rows = lax.broadcasted_iota(jnp.int32, (RET_DK, RET_DK), 0)
    cols = lax.broadcasted_iota(jnp.int32, (RET_DK, RET_DK), 1)
    eye = rows == cols

    def column(r):
        return jnp.sum(jnp.where(eye, jnp.broadcast_to(r, (RET_DK, RET_DK)), 0.0),
                       axis=1, keepdims=True)

    heads = range(RET_HEADS)
    lanes = [slice(hd * RET_DK, (hd + 1) * RET_DK) for hd in heads]
    gams = [gam_ref[hd:hd + 1, :] for hd in heads]
    for i in range(RET_SEQS_PER_STEP):
        qs = [q_ref[i, :, sl] for sl in lanes]
        ks = [k_ref[i, :, sl] for sl in lanes]
        vs = [v_ref[i, :, sl] for sl in lanes]
        q_cols = [column(q) for q in qs]
        k_cols = [column(k) for k in ks]
        qk = [jnp.sum(q * k, axis=-1, keepdims=True) for q, k in zip(qs, ks)]
        states = [s_ref[0, i, hd] for hd in heads]
        outs = [qk[hd] * vs[hd] + jnp.sum(q_cols[hd] * states[hd], axis=0, keepdims=True) * gams[hd]
                for hd in heads]
        for hd in heads:
            s_out_ref[0, i, hd] = gams[hd] * states[hd] + k_cols[hd] * vs[hd]
        mus = [jnp.mean(o, axis=-1, keepdims=True) for o in outs]
        devs = [o - mu for o, mu in zip(outs, mus)]
        variances = [jnp.mean(d * d, axis=-1, keepdims=True) for d in devs]
        for hd in heads:
            o_ref[i, :, lanes[hd]] = devs[hd] * lax.rsqrt(variances[hd] + EPS) * g_ref[hd:hd + 1, :]


def _sample_retention(qr, kr, rv, state, ret_norm):
    n = qr.shape[0]
    gam = np.ascontiguousarray(
        np.broadcast_to(np.exp(_log_gamma() * 1.0)[:, None], (RET_HEADS, LANES)), dtype=np.float32)
    g = RET_SEQS_PER_STEP
    tok = pl.BlockSpec((g, 1, _RET_W), lambda i: (i, 0, 0))
    st = pl.BlockSpec((1, g, RET_HEADS, RET_DK, RET_DV), lambda i: (0, i, 0, 0, 0))
    r3 = lambda a: a.reshape(n, 1, _RET_W)
    o, s_new = pl.pallas_call(
        _ret_sample_kernel,
        grid=(n // g,),
        in_specs=[tok, tok, tok, st, _const_spec(gam.shape), _const_spec(ret_norm.shape)],
        out_specs=[tok, st],
        out_shape=[jax.ShapeDtypeStruct((n, 1, _RET_W), F32),
                   jax.ShapeDtypeStruct(state.shape, F32)],
        compiler_params=pltpu.CompilerParams(dimension_semantics=("arbitrary",)),
        name="sample_retention",
    )(r3(qr), r3(kr), r3(rv), state, gam, ret_norm)
    return o.reshape(n, _RET_W), s_new


def _absorb_q_kernel(qt_ref, wt_ref, o_ref):
    for hd in range(MLA_HEADS):
        o_ref[hd] = _dot(wt_ref[hd], qt_ref[hd * HEAD_PAD:(hd + 1) * HEAD_PAD, :])


def _absorb_q(qt, w_absorb_t):
    n = qt.shape[1]
    return pl.pallas_call(
        _absorb_q_kernel,
        out_shape=jax.ShapeDtypeStruct((MLA_HEADS, KEY_W, n), F32),
        name="sample_absorb_q",
    )(qt, w_absorb_t)


def _unabsorb_kernel(o_ref, w_ref, out_ref):
    acc = _dot(o_ref[0].astype(BF16), w_ref[0])
    for hd in range(1, MLA_HEADS):
        acc = acc + _dot(o_ref[hd].astype(BF16), w_ref[hd])
    out_ref[...] = acc.astype(out_ref.dtype)


def _unabsorb(o_lat, w_unabsorb):
    n = o_lat.shape[1]
    return pl.pallas_call(
        _unabsorb_kernel,
        out_shape=jax.ShapeDtypeStruct((n, _ATT_W), BF16),
        name="sample_unabsorb",
    )(o_lat, w_unabsorb)


def _paged_kernel(pt_ref, q_ref, knew_ref, ckv_hbm, kpet_hbm, o_ref, buf, kpe_buf, sem):
    b = pl.program_id(0)
    n_seq = pl.num_programs(0)
    n_chunks = pt_ref.shape[1] // PAGES_PER_STEP
    assert n_chunks == PAGE_SLOTS
    ahead = PAGE_SLOTS - 1
    chunk_keys = PAGES_PER_STEP * PAGE_SIZE
    group_keys = chunk_keys // PAGE_GROUPS

    def page_copies(bb, cc, sl):
        out = []
        for j in range(PAGES_PER_STEP):
            pg = pt_ref[bb, cc * PAGES_PER_STEP + j]
            keys = pl.ds(j * PAGE_SIZE, PAGE_SIZE)
            out.append(pltpu.make_async_copy(ckv_hbm.at[0, pg], buf.at[sl, keys], sem.at[sl, 0]))
            out.append(pltpu.make_async_copy(kpet_hbm.at[0, pg], kpe_buf.at[sl, :, keys],
                                             sem.at[sl, 1]))
        return out

    @pl.when(b == 0)
    def _():
        for c in range(ahead):
            for cp in page_copies(0, c, c):
                cp.start()

    b_next = jnp.minimum(b + 1, n_seq - 1)

    q = q_ref[0]
    q_lat = q[:, 0:KV_LORA]
    q_pe = q[:, KV_LORA:KV_LORA + MLA_ROPE]

    def scores(sl, g):
        keys = slice(g * group_keys, (g + 1) * group_keys)
        return _dot_nt(q_lat, buf[sl, keys, :]) + _dot(q_pe, kpe_buf[sl, :, keys])

    m_run = jnp.full((MLA_HEADS, 1), NEG_BIG, F32)
    l_run = jnp.zeros((MLA_HEADS, 1), F32)
    acc_run = jnp.zeros((MLA_HEADS, KV_LORA), F32)
    def wait_chunk(c):
        pltpu.make_async_copy(buf.at[c], buf.at[c], sem.at[c, 0]).wait()
        pltpu.make_async_copy(kpe_buf.at[c], kpe_buf.at[c], sem.at[c, 1]).wait()

    wait_chunk(0)
    s_next = scores(0, 0)
    for c in range(n_chunks):
        sl = c
        nxt = c + ahead
        prefetch = (page_copies(b, nxt, nxt) if nxt < n_chunks
                    else page_copies(b_next, nxt - n_chunks, nxt - n_chunks))
        parts = []
        per_group = len(prefetch) // PAGE_GROUPS
        for g in range(PAGE_GROUPS):
            s = s_next
            if g + 1 < PAGE_GROUPS:
                s_next = scores(sl, g + 1)
            elif c + 1 < n_chunks:
                wait_chunk(c + 1)
                s_next = scores(c + 1, 0)
            for cp in prefetch[g * per_group:(g + 1) * per_group]:
                cp.start()
            m_g = jnp.max(s, axis=-1, keepdims=True)
            p = jnp.exp2(s - m_g)
            l_g = jnp.sum(p, axis=-1, keepdims=True)
            keys = slice(g * group_keys, (g + 1) * group_keys)
            parts.append((m_g, l_g, _dot(p, buf[sl, keys, :])))

        m_new = m_run
        for m_g, _, _ in parts:
            m_new = jnp.maximum(m_new, m_g)
        a_run = jnp.exp2(m_run - m_new)
        l_run = a_run * l_run
        acc_run = a_run * acc_run
        for m_g, l_g, o_g in parts:
            a_g = jnp.exp2(m_g - m_new)
            l_run = l_run + a_g * l_g
            acc_run = acc_run + a_g * o_g
        m_run = m_new

    kn = knew_ref[0]
    s_self = jnp.sum(q * kn, axis=-1, keepdims=True)
    m_fin = jnp.maximum(m_run, s_self)
    a = jnp.exp2(m_run - m_fin)
    p_self = jnp.exp2(s_self - m_fin)
    o_ref[0] = (a * acc_run + p_self * kn[:, 0:KV_LORA]) / (a * l_run + p_self)

    @pl.when(b == n_seq - 1)
    def _():
        for c in range(ahead):
            wait_chunk(c)


def _paged_attention(page_table, q_abs, k_new, cache_ckv, cache_kpe_t):
    n, n_pages = page_table.shape
    chunk_keys = PAGES_PER_STEP * PAGE_SIZE
    grid_spec = pltpu.PrefetchScalarGridSpec(
        num_scalar_prefetch=1,
        grid=(n,),
        in_specs=[
            pl.BlockSpec((1, MLA_HEADS, KEY_W), lambda b, pt: (b, 0, 0)),
            pl.BlockSpec((1, 1, KEY_W), lambda b, pt: (b, 0, 0)),
            pl.BlockSpec(memory_space=pl.ANY),
            pl.BlockSpec(memory_space=pl.ANY),
        ],
        out_specs=pl.BlockSpec((1, MLA_HEADS, KV_LORA), lambda b, pt: (b, 0, 0)),
        scratch_shapes=[pltpu.VMEM((PAGE_SLOTS, chunk_keys, KV_LORA), F32),
                        pltpu.VMEM((PAGE_SLOTS, MLA_ROPE, chunk_keys), F32),
                        pltpu.SemaphoreType.DMA((PAGE_SLOTS, 2))],
    )
    return pl.pallas_call(
        _paged_kernel,
        grid_spec=grid_spec,
        out_shape=jax.ShapeDtypeStruct((n, MLA_HEADS, KV_LORA), F32),
        compiler_params=pltpu.CompilerParams(dimension_semantics=("arbitrary",),
                                             vmem_limit_bytes=VMEM_LIMIT),
        name="sample_paged_attention",
    )(page_table, q_abs, k_new, cache_ckv, cache_kpe_t)


def _back_kernel(h_ref, oatt_ref, oret_ref, p_ref,
                 gmix_ref, wgate_ref, wba_ref, wbr_ref, wout_ref,
                 g2_ref, wg_ref, wu_ref, wd_ref, gple_ref, wpg_ref, wpp_ref, gfin_ref, y_ref):
    h = h_ref[...]
    un = _rms(h, gmix_ref[...]).astype(BF16)
    gates = _dot(un, wgate_ref[...])
    rg = gates[:, :_RET_W]
    ga = gates[:, _RET_W:_RET_W + D_MODEL]
    gr = gates[:, _RET_W + D_MODEL:]
    o_ret = (rg * jax.nn.sigmoid(rg) * oret_ref[...]).astype(BF16)
    merged = (jax.nn.sigmoid(ga) * _dot(oatt_ref[...], wba_ref[...])
              + jax.nn.sigmoid(gr) * _dot(o_ret, wbr_ref[...]))
    h = h + _dot(merged.astype(BF16), wout_ref[...])
    h = _swiglu_half(h, g2_ref[...], wg_ref[...], wu_ref[...], wd_ref[...])
    gate = jax.nn.sigmoid(_dot(_rms(h, gple_ref[...]).astype(BF16), wpg_ref[...]))
    h = h + gate * _dot(p_ref[...].astype(BF16), wpp_ref[...])
    y_ref[...] = _rms(h, gfin_ref[...])


def _back_stage(h, o_att, o_ret, p_emb, w, tile):
    n = h.shape[0]
    row = lambda width: pl.BlockSpec((tile, width), lambda i: (i, 0))
    weights = [w['mix_norm'], w['w_in_gates'], w['w_branch_att'], w['w_branch_ret'], w['w_out'],
               w['ffn2_norm'], w['ffn2_w_gate'], w['ffn2_w_up'], w['ffn2_w_down'],
               w['ple_norm'], w['w_ple_gate'], w['w_ple_proj'], w['final_norm']]
    return pl.pallas_call(
        _back_kernel,
        grid=(n // tile,),
        in_specs=[row(D_MODEL), row(_ATT_W), row(_RET_W), row(PLE_DIM)]
        + [_const_spec(a.shape) for a in weights],
        out_specs=row(D_MODEL),
        out_shape=jax.ShapeDtypeStruct((n, D_MODEL), F32),
        compiler_params=pltpu.CompilerParams(dimension_semantics=("arbitrary",),
                                             vmem_limit_bytes=VMEM_LIMIT),
        name="back_stage",
    )(h, o_att, o_ret, p_emb, *weights)


def _rope_tables(pos):
    pos = np.asarray(pos, np.float64)[:, None]
    n = pos.shape[0]
    inv_m = ROPE_THETA ** (-np.arange(ROPE_HALF, dtype=np.float64) / ROPE_HALF)
    cos_m, sin_m = np.cos(pos * inv_m[None, :]), np.sin(pos * inv_m[None, :])
    z = lambda width: np.zeros((n, width))
    tail = HEAD_PAD - ROPE_OFF - MLA_ROPE
    cm = np.concatenate([np.ones((n, ROPE_OFF)), cos_m, cos_m, z(tail)], axis=1)
    s1 = np.concatenate([z(ROPE_OFF), -sin_m, z(ROPE_HALF), z(tail)], axis=1)
    s2 = np.concatenate([z(ROPE_OFF), z(ROPE_HALF), sin_m, z(tail)], axis=1)
    half = RET_DK // 2
    inv_r = ROPE_THETA ** (-np.arange(half, dtype=np.float64) / half)
    cos_r, sin_r = np.cos(pos * inv_r[None, :]), np.sin(pos * inv_r[None, :])
    cr = np.concatenate([cos_r, cos_r], axis=1)
    sr = np.concatenate([-sin_r, sin_r], axis=1)
    tables = (cm, s1, s2, cr, sr, cos_m.T, sin_m.T)
    return tuple(np.ascontiguousarray(t, dtype=np.float32) for t in tables)


def _layer_weights(i, ffn1_norm, ffn1_w_gate, ffn1_w_up, ffn1_w_down, mix_norm, w_in, q_a_norm, w_q_b,
                   kv_a_norm, w_kv_b, ret_norm, w_branch_att, w_branch_ret, w_out,
                   ffn2_norm, ffn2_w_gate, ffn2_w_up, ffn2_w_down, ple_norm, w_ple_gate, w_ple_proj,
                   final_norm):
    vec = lambda a: a.reshape(1, -1)
    bf = lambda a: a.astype(BF16)
    win = w_in[i]
    zcols = lambda width: jnp.zeros((D_MODEL, width), F32)
    w_in_front = jnp.concatenate(
        [win[:, :_OFF_KPE], zcols(ROPE_OFF), win[:, _OFF_KPE:_OFF_RQ],
         zcols(HEAD_PAD - ROPE_OFF - MLA_ROPE), win[:, _OFF_RQ:_OFF_RG]], axis=1)
    wqb = w_q_b[i].reshape(Q_LORA, MLA_HEADS, MLA_NOPE + MLA_ROPE)
    wqb = jnp.pad(wqb, ((0, 0), (0, 0), (0, HEAD_PAD - MLA_NOPE - MLA_ROPE)))
    wkv = w_kv_b[i].reshape(KV_LORA, MLA_HEADS, MLA_NOPE + MLA_V)
    w_uk, w_uv = wkv[..., :MLA_NOPE], wkv[..., MLA_NOPE:]
    w_k = jnp.pad(w_uk, ((0, 0), (0, 0), (0, HEAD_PAD - MLA_NOPE)))
    pass_rope = np.zeros((1, KEY_W, HEAD_PAD), np.float32)
    pass_rope[0, KV_LORA:KV_LORA + MLA_ROPE, MLA_NOPE:MLA_NOPE + MLA_ROPE] = np.eye(MLA_ROPE)
    w_absorb_t = jnp.pad(jnp.transpose(w_uk, (1, 0, 2)),
                         ((0, 0), (0, KEY_W - KV_LORA), (0, HEAD_PAD - MLA_NOPE))) + pass_rope
    own_cols = np.eye(MLA_HEADS, dtype=np.float32)[:, None, :, None]
    w_unabsorb = (own_cols * jnp.transpose(w_uv, (1, 0, 2))[:, :, None, :]).reshape(
        MLA_HEADS, KV_LORA, _ATT_W)
    return {
        'ffn1_norm': vec(ffn1_norm[i]), 'ffn1_w_gate': bf(ffn1_w_gate[i]), 'ffn1_w_up': bf(ffn1_w_up[i]),
        'ffn1_w_down': bf(ffn1_w_down[i]), 'mix_norm': vec(mix_norm[i]),
        'w_in_front': bf(w_in_front), 'w_in_gates': bf(win[:, _OFF_RG:]),
        'q_a_norm': vec(q_a_norm[i]),
        'w_q_bt': bf(jnp.transpose(wqb.reshape(Q_LORA, MLA_HEADS * HEAD_PAD))),
        'kv_a_norm': vec(kv_a_norm[i]), 'w_k': bf(w_k.reshape(KV_LORA, MLA_HEADS * HEAD_PAD)),
        'w_vt': bf(jnp.pad(jnp.transpose(w_uv, (1, 2, 0)), ((0, 0), (0, V_ROWS - MLA_V), (0, 0)))
                   .reshape(_VT_ROWS, KV_LORA)),
        'w_absorb_t': bf(w_absorb_t), 'w_unabsorb': bf(w_unabsorb),
        'ret_norm': ret_norm[i],
        'w_branch_att': bf(w_branch_att[i]), 'w_branch_ret': bf(w_branch_ret[i]), 'w_out': bf(w_out[i]),
        'ffn2_norm': vec(ffn2_norm[i]), 'ffn2_w_gate': bf(ffn2_w_gate[i]), 'ffn2_w_up': bf(ffn2_w_up[i]),
        'ffn2_w_down': bf(ffn2_w_down[i]), 'ple_norm': vec(ple_norm[i]),
        'w_ple_gate': bf(w_ple_gate[i]), 'w_ple_proj': bf(w_ple_proj[i]),
        'final_norm': vec(final_norm),
    }


def kernel(x_prompt, x_sample, cache_ckv, cache_kpe, state_ret, page_table, p_prompt, p_sample, ffn1_norm, ffn1_w_gate, ffn1_w_up, ffn1_w_down, mix_norm, w_in, q_a_norm, w_q_b, kv_a_norm, w_kv_b, ret_norm, w_branch_att, w_branch_ret, w_out, ffn2_norm, ffn2_w_gate, ffn2_w_up, ffn2_w_down, ple_norm, w_ple_gate, w_ple_proj, final_norm):
    batch, seq, _ = x_prompt.shape
    n_dec, dec_seq, _ = x_sample.shape
    depth = w_in.shape[0]
    assert dec_seq == 1 and depth == 1
    n_past = page_table.shape[1] * PAGE_SIZE

    w = _layer_weights(0, ffn1_norm, ffn1_w_gate, ffn1_w_up, ffn1_w_down, mix_norm, w_in, q_a_norm,
                       w_q_b, kv_a_norm, w_kv_b, ret_norm, w_branch_att, w_branch_ret, w_out,
                       ffn2_norm, ffn2_w_gate, ffn2_w_up, ffn2_w_down, ple_norm, w_ple_gate,
                       w_ple_proj, final_norm)

    tabs_p = _rope_tables(np.arange(seq))
    (h_p, qt_p, k_p, vt_p, ckv_p, kpe_p, qr_p, kr_p, rv_p) = _front_stage(
        x_prompt.reshape(batch * seq, D_MODEL), tabs_p, w, TOKEN_TILE, seq // TOKEN_TILE)
    bt = lambda a: a.reshape(batch, seq, a.shape[-1])
    o_att_p = _prompt_attention(qt_p, k_p, vt_p, batch, seq)
    o_ret_p, ret_p = _prompt_retention(bt(qr_p), bt(kr_p), bt(rv_p), w['ret_norm'])
    y_p = _back_stage(h_p, o_att_p, o_ret_p.reshape(batch * seq, -1),
                      p_prompt.reshape(batch * seq, PLE_DIM), w, TOKEN_TILE)

    tabs_1 = _rope_tables(np.full((1,), n_past))
    tabs_s = (tuple(np.ascontiguousarray(np.broadcast_to(t, (n_dec, LANES))) for t in tabs_1[:5])
              + tuple(np.ascontiguousarray(np.broadcast_to(t, (ROPE_HALF, n_dec))) for t in tabs_1[5:]))
    (h_s, qt_s, _, _, ckv_s, kpe_s, qr_s, kr_s, rv_s) = _front_stage(
        x_sample.reshape(n_dec, D_MODEL), tabs_s, w, n_dec, 1)
    q_abs = jnp.transpose(_absorb_q(qt_s, w['w_absorb_t']), (2, 0, 1))
    k_new = jnp.concatenate([ckv_s, kpe_s[:, ROPE_OFF:ROPE_OFF + MLA_ROPE],
                             jnp.zeros((n_dec, KEY_W - KV_LORA - MLA_ROPE), F32)], axis=1)
    o_lat = _paged_attention(page_table, q_abs, k_new.reshape(n_dec, 1, KEY_W), cache_ckv,
                             jnp.swapaxes(cache_kpe, 2, 3))
    o_att_s = _unabsorb(jnp.transpose(o_lat, (1, 0, 2)), w['w_unabsorb'])
    o_ret_s, ret_s = _sample_retention(qr_s, kr_s, rv_s, state_ret, w['ret_norm'])
    y_s = _back_stage(h_s, o_att_s, o_ret_s, p_sample.reshape(n_dec, PLE_DIM), w, n_dec)

    kpe_cols = slice(ROPE_OFF, ROPE_OFF + MLA_ROPE)
    return (y_p.reshape(batch, seq, D_MODEL),
            y_s.reshape(n_dec, 1, D_MODEL),
            ckv_p.reshape(1, batch, seq, KV_LORA),
            kpe_p[:, kpe_cols].reshape(1, batch, seq, MLA_ROPE),
            ret_p[None],
            ckv_s.reshape(1, n_dec, 1, KV_LORA),
            kpe_s[:, kpe_cols].reshape(1, n_dec, 1, MLA_ROPE),
            ret_s)
```

```python
import jax
import jax.numpy as jnp
import numpy as np
from jax import lax
from jax.experimental import pallas as pl
from jax.experimental.pallas import tpu as pltpu

F32 = jnp.float32
BF16 = jnp.bfloat16

D_MODEL = 1024
D_FF = 2816
PLE_DIM = 256
MLA_HEADS = 8
MLA_NOPE = 64
MLA_ROPE = 32
MLA_V = 64
Q_LORA = 256
KV_LORA = 128
RET_HEADS = 4
RET_DK = 128
RET_DV = 128
PAGE_SIZE = 128
ROPE_THETA = 10000.0
EPS = 1e-6

LANES = 128
HEAD_PAD = LANES
ROPE_OFF = MLA_NOPE
ROPE_HALF = MLA_ROPE // 2
SOFTMAX_SCALE = (MLA_NOPE + MLA_ROPE) ** -0.5
LOG2E = 1.4426950408889634
Q_SCALE = SOFTMAX_SCALE * LOG2E
NEG_BIG = -1e30

TOKEN_TILE = 512
ATTN_TILE = 512
RET_CHUNK = 512
PAGES_PER_STEP = 32
PAGE_GROUPS = 4
PAGE_SLOTS = 4
RET_SEQS_PER_STEP = 8
KEY_W = 2 * LANES
VMEM_LIMIT = 60 * 1024 * 1024

_OFF_CQ = 0
_OFF_CKV = _OFF_CQ + Q_LORA
_OFF_KPE = _OFF_CKV + KV_LORA
_OFF_RQ = _OFF_KPE + MLA_ROPE
_OFF_RK = _OFF_RQ + RET_HEADS * RET_DK
_OFF_RV = _OFF_RK + RET_HEADS * RET_DK
_OFF_RG = _OFF_RV + RET_HEADS * RET_DV
_RET_W = RET_HEADS * RET_DK
_ATT_W = MLA_HEADS * MLA_V
V_ROWS = MLA_V + 16
_VT_ROWS = MLA_HEADS * V_ROWS


def _rms(x, g):
    return x * lax.rsqrt(jnp.mean(x * x, axis=-1, keepdims=True) + EPS) * g


def _dot(a, b):
    return jnp.dot(a, b, preferred_element_type=F32)


def _dot_nt(a, b):
    return lax.dot_general(a, b, (((1,), (1,)), ((), ())), preferred_element_type=F32)


def _swiglu_half(x, g, wg, wu, wd):
    xn = _rms(x, g).astype(BF16)
    gate = _dot(xn, wg)
    up = _dot(xn, wu)
    act = (gate * jax.nn.sigmoid(gate) * up).astype(BF16)
    return x + 0.5 * _dot(act, wd)


def _const_spec(shape):
    n = len(shape)
    return pl.BlockSpec(shape, lambda *_: (0,) * n, pipeline_mode=pl.Buffered(1))


def _front_kernel(x_ref, cm_ref, s1_ref, s2_ref, cr_ref, sr_ref, ct_ref, st_ref,
                  g1_ref, wg_ref, wu_ref, wd_ref, gmix_ref, win_ref,
                  gqa_ref, wqbt_ref, gkv_ref, wk_ref, wvt_ref,
                  h_ref, qt_ref, k_ref, vt_ref, ckv_ref, kpet_ref, qr_ref, kr_ref, rv_ref):
    x = x_ref[...]
    h = _swiglu_half(x, g1_ref[...], wg_ref[...], wu_ref[...], wd_ref[...])
    h_ref[...] = h
    un = _rms(h, gmix_ref[...]).astype(BF16)
    z = _dot(un, win_ref[...])

    cm, s1, s2 = cm_ref[...], s1_ref[...], s2_ref[...]

    def mla_rope(t):
        return (t * cm + pltpu.roll(t, LANES - ROPE_HALF, 1) * s1
                + pltpu.roll(t, ROPE_HALF, 1) * s2)

    cq = z[:, _OFF_CQ:_OFF_CQ + Q_LORA]
    qt = _dot_nt(wqbt_ref[...], _rms(cq, gqa_ref[...]).astype(BF16))
    ct, st = ct_ref[...], st_ref[...]
    c_kv = _rms(z[:, Q_LORA:Q_LORA + KV_LORA], gkv_ref[...])
    ckv_ref[...] = c_kv
    c_kv_b = c_kv.astype(BF16)
    k_nope = _dot(c_kv_b, wk_ref[...])
    vt = _dot_nt(wvt_ref[...], c_kv_b)
    sum_row = lax.broadcasted_iota(jnp.int32, vt.shape, 0) % V_ROWS >= MLA_V
    vt_ref[...] = jnp.where(sum_row, 1.0, vt).astype(BF16)
    k_pe = mla_rope(z[:, Q_LORA + KV_LORA:Q_LORA + KV_LORA + LANES])
    kpet_ref[0] = jnp.transpose(k_pe)[ROPE_OFF:ROPE_OFF + MLA_ROPE, :]
    for hd in range(MLA_HEADS):
        sl = slice(hd * HEAD_PAD, (hd + 1) * HEAD_PAD)
        k_ref[:, sl] = (k_nope[:, sl] + k_pe).astype(BF16)
        r0 = hd * HEAD_PAD + ROPE_OFF
        x1 = qt[r0:r0 + ROPE_HALF]
        x2 = qt[r0 + ROPE_HALF:r0 + MLA_ROPE]
        qt_ref[hd * HEAD_PAD:r0, :] = (qt[hd * HEAD_PAD:r0] * Q_SCALE).astype(BF16)
        qt_ref[r0:r0 + ROPE_HALF, :] = ((x1 * ct - x2 * st) * Q_SCALE).astype(BF16)
        qt_ref[r0 + ROPE_HALF:r0 + MLA_ROPE, :] = ((x1 * st + x2 * ct) * Q_SCALE).astype(BF16)
        qt_ref[r0 + MLA_ROPE:(hd + 1) * HEAD_PAD, :] = jnp.zeros(
            (HEAD_PAD - ROPE_OFF - MLA_ROPE, qt.shape[1]), BF16)

    cr, sr = cr_ref[...], sr_ref[...]
    base = Q_LORA + KV_LORA + LANES
    for hd in range(RET_HEADS):
        sl = slice(hd * RET_DK, (hd + 1) * RET_DK)
        rq = z[:, base + hd * RET_DK:base + (hd + 1) * RET_DK]
        rk = z[:, base + _RET_W + hd * RET_DK:base + _RET_W + (hd + 1) * RET_DK]
        qr_ref[:, sl] = rq * cr + pltpu.roll(rq, RET_DK // 2, 1) * sr
        kr_ref[:, sl] = (rk * cr + pltpu.roll(rk, RET_DK // 2, 1) * sr) * (RET_DK ** -0.5)
    rv_ref[...] = z[:, base + 2 * _RET_W:base + 3 * _RET_W]


def _front_stage(x, tables, w, tile, table_tiles):
    n = x.shape[0]
    seq_rows = tile * table_tiles
    row = lambda width: pl.BlockSpec((tile, width), lambda i: (i, 0))
    col = lambda height: pl.BlockSpec((height, tile), lambda i: (0, i))
    tab = pl.BlockSpec((tile, LANES), lambda i: (i % table_tiles, 0))
    tab_t = pl.BlockSpec((ROPE_HALF, tile), lambda i: (0, i % table_tiles))
    weights = [w['ffn1_norm'], w['ffn1_w_gate'], w['ffn1_w_up'], w['ffn1_w_down'], w['mix_norm'],
               w['w_in_front'], w['q_a_norm'], w['w_q_bt'], w['kv_a_norm'], w['w_k'], w['w_vt']]
    outs = [((n, D_MODEL), F32, row(D_MODEL)),
            ((MLA_HEADS * HEAD_PAD, n), BF16, col(MLA_HEADS * HEAD_PAD)),
            ((n, MLA_HEADS * HEAD_PAD), BF16, row(MLA_HEADS * HEAD_PAD)),
            ((_VT_ROWS, n), BF16, col(_VT_ROWS)),
            ((n, KV_LORA), F32, row(KV_LORA)),
            ((n // seq_rows, MLA_ROPE, seq_rows), F32,
             pl.BlockSpec((1, MLA_ROPE, tile), lambda i: (i // table_tiles, 0, i % table_tiles))),
            ((n, _RET_W), F32, row(_RET_W)),
            ((n, _RET_W), F32, row(_RET_W)),
            ((n, _RET_W), F32, row(_RET_W))]
    return pl.pallas_call(
        _front_kernel,
        grid=(n // tile,),
        in_specs=[row(D_MODEL)] + [tab] * 5 + [tab_t] * 2 + [_const_spec(a.shape) for a in weights],
        out_specs=[spec for _, _, spec in outs],
        out_shape=[jax.ShapeDtypeStruct(shape, dt) for shape, dt, _ in outs],
        compiler_params=pltpu.CompilerParams(dimension_semantics=("arbitrary",),
                                             vmem_limit_bytes=VMEM_LIMIT),
        name="front_stage",
    )(x, *tables, *weights)


def _attn_kernel(qi_ref, ki_ref, qt_ref, k_ref, vt_ref, o_ref, m_sc, acc_sc):
    t = pl.program_id(1)
    qi = qi_ref[t]
    ki = ki_ref[t]

    @pl.when(ki == 0)
    def _():
        m_sc[...] = jnp.full(m_sc.shape, NEG_BIG, F32)
        acc_sc[...] = jnp.zeros(acc_sc.shape, F32)

    def scores(hd):
        kh = k_ref[:, hd * HEAD_PAD:(hd + 1) * HEAD_PAD]
        return _dot(kh, qt_ref[hd * HEAD_PAD:(hd + 1) * HEAD_PAD, :])

    def value_update(hd, alpha, p):
        acc_sc[hd] = alpha * acc_sc[hd] + _dot(vt_ref[hd * V_ROWS:(hd + 1) * V_ROWS, :], p)

    def update(masked):
        st_next = scores(0)
        pending = None
        for hd in range(MLA_HEADS):
            st = st_next
            if hd + 1 < MLA_HEADS:
                st_next = scores(hd + 1)
            if masked:
                key = lax.broadcasted_iota(jnp.int32, st.shape, 0)
                qry = lax.broadcasted_iota(jnp.int32, st.shape, 1)
                st = jnp.where(key <= qry, st, NEG_BIG)
            m_prev = m_sc[hd]
            m_new = jnp.maximum(m_prev, jnp.max(st, axis=0, keepdims=True))
            alpha = jnp.exp2(m_prev - m_new)
            p = jnp.exp2(st - m_new).astype(BF16)
            m_sc[hd] = m_new
            if pending is not None:
                value_update(*pending)
            pending = (hd, alpha, p)
        value_update(*pending)

    @pl.when(ki < qi)
    def _():
        update(False)

    @pl.when(ki == qi)
    def _():
        update(True)
        acc = acc_sc[...]
        o_t = (acc[:, :MLA_V, :] / acc[:, MLA_V:MLA_V + 1, :]).reshape(_ATT_W, ATTN_TILE)
        o_ref[...] = jnp.transpose(o_t).astype(o_ref.dtype)


def _prompt_attention(qt, k, vt, batch, seq):
    nt = seq // ATTN_TILE
    qi_list = np.array([i for i in range(nt) for _ in range(i + 1)], np.int32)
    ki_list = np.array([j for i in range(nt) for j in range(i + 1)], np.int32)
    qk_w = MLA_HEADS * HEAD_PAD
    grid_spec = pltpu.PrefetchScalarGridSpec(
        num_scalar_prefetch=2,
        grid=(batch, len(qi_list)),
        in_specs=[
            pl.BlockSpec((qk_w, ATTN_TILE), lambda bb, s, qi, ki: (0, bb * nt + qi[s])),
            pl.BlockSpec((ATTN_TILE, qk_w), lambda bb, s, qi, ki: (bb * nt + ki[s], 0)),
            pl.BlockSpec((_VT_ROWS, ATTN_TILE), lambda bb, s, qi, ki: (0, bb * nt + ki[s])),
        ],
        out_specs=pl.BlockSpec((ATTN_TILE, _ATT_W), lambda bb, s, qi, ki: (bb * nt + qi[s], 0)),
        scratch_shapes=[pltpu.VMEM((MLA_HEADS, 1, ATTN_TILE), F32),
                        pltpu.VMEM((MLA_HEADS, V_ROWS, ATTN_TILE), F32)],
    )
    return pl.pallas_call(
        _attn_kernel,
        grid_spec=grid_spec,
        out_shape=jax.ShapeDtypeStruct((batch * seq, _ATT_W), BF16),
        compiler_params=pltpu.CompilerParams(dimension_semantics=("arbitrary", "arbitrary"),
                                             vmem_limit_bytes=VMEM_LIMIT),
        name="prompt_attention",
    )(jnp.asarray(qi_list), jnp.asarray(ki_list), qt, k, vt)


def _head_layernorm(o, g):
    mu = jnp.mean(o, axis=-1, keepdims=True)
    d = o - mu
    var = jnp.mean(d * d, axis=-1, keepdims=True)
    return d * lax.rsqrt(var + EPS) * g


def _ret_prompt_kernel(q_ref, k_ref, v_ref, dec_ref, qd_ref, kd_ref, gc_ref, g_ref,
                       o_ref, s_out_ref, s_sc):
    c = pl.program_id(1)

    @pl.when(c == 0)
    def _():
        s_sc[...] = jnp.zeros(s_sc.shape, F32)

    for hd in range(RET_HEADS):
        sl = slice(hd * RET_DK, (hd + 1) * RET_DK)
        q = q_ref[0, :, sl]
        k = k_ref[0, :, sl]
        vb = v_ref[0, :, sl].astype(BF16)
        qb = q.astype(BF16)
        state = s_sc[hd]
        scores = _dot_nt(qb, k.astype(BF16)) * dec_ref[hd]
        inner = _dot(scores.astype(BF16), vb)
        cross = _dot(qb, state.astype(BF16)) * qd_ref[hd]
        o_ref[0, :, sl] = _head_layernorm(inner + cross, g_ref[hd:hd + 1, :])
        k_dec_t = jnp.transpose(k * kd_ref[hd]).astype(BF16)
        s_sc[hd] = gc_ref[hd:hd + 1, :] * state + _dot(k_dec_t, vb)

    @pl.when(c == pl.num_programs(1) - 1)
    def _():
        s_out_ref[0] = s_sc[...]


def _log_gamma():
    return np.log1p(-np.exp2(-5.0 - np.arange(RET_HEADS, dtype=np.float64)))


def _retention_consts(chunk):
    log_gamma = _log_gamma()
    idx = np.arange(chunk, dtype=np.float64)
    diff = idx[:, None] - idx[None, :]
    decay = np.where(diff[None] >= 0,
                     np.exp(log_gamma[:, None, None] * np.maximum(diff, 0.0)[None]), 0.0)
    q_decay = np.exp(log_gamma[:, None] * (idx + 1.0)[None, :])
    k_decay = np.exp(log_gamma[:, None] * (chunk - 1.0 - idx)[None, :])
    chunk_decay = np.exp(log_gamma * chunk)
    lane = lambda a: np.broadcast_to(a[..., None], a.shape + (LANES,))
    tables = (decay, lane(q_decay), lane(k_decay), lane(chunk_decay))
    return tuple(np.ascontiguousarray(t, dtype=np.float32) for t in tables)


def _prompt_retention(qr, kr, rv, ret_norm):
    b, t, _ = qr.shape
    decay, q_decay, k_decay, chunk_decay = _retention_consts(RET_CHUNK)
    seq = pl.BlockSpec((1, RET_CHUNK, _RET_W), lambda bb, c: (bb, c, 0))
    return pl.pallas_call(
        _ret_prompt_kernel,
        grid=(b, t // RET_CHUNK),
        in_specs=[seq, seq, seq, _const_spec(decay.shape), _const_spec(q_decay.shape),
                  _const_spec(k_decay.shape), _const_spec(chunk_decay.shape),
                  _const_spec(ret_norm.shape)],
        out_specs=[seq, pl.BlockSpec((1, RET_HEADS, RET_DK, RET_DV), lambda bb, c: (bb, 0, 0, 0))],
        out_shape=[jax.ShapeDtypeStruct((b, t, _RET_W), F32),
                   jax.ShapeDtypeStruct((b, RET_HEADS, RET_DK, RET_DV), F32)],
        scratch_shapes=[pltpu.VMEM((RET_HEADS, RET_DK, RET_DV), F32)],
        compiler_params=pltpu.CompilerParams(dimension_semantics=("arbitrary", "arbitrary")),
        name="prompt_retention",
    )(qr, kr, rv, decay, q_decay, k_decay, chunk_decay, ret_norm)


def _ret_sample_kernel(q_ref, k_ref, v_ref, s_ref, gam_ref, g_ref, o_ref, s_out_ref):
    rows = lax.broadcasted_iota(jnp.int32, (RET_DK, RET_DK), 0)
    cols = lax.broadcasted_iota(jnp.int32, (RET_DK, RET_DK), 1)
    eye = rows == cols

    def column(r):
        return jnp.sum(jnp.where(eye, jnp.broadcast_to(r, (RET_DK, RET_DK)), 0.0),
                       axis=1, keepdims=True)

    heads = range(RET_HEADS)
    lanes = [slice(hd * RET_DK, (hd + 1) * RET_DK) for hd in heads]
    gams = [gam_ref[hd:hd + 1, :] for hd in heads]
    for i in range(RET_SEQS_PER_STEP):
        qs = [q_ref[i, :, sl] for sl in lanes]
        ks = [k_ref[i, :, sl] for sl in lanes]
        vs = [v_ref[i, :, sl] for sl in lanes]
        q_cols = [column(q) for q in qs]
        k_cols = [column(k) for k in ks]
        qk = [jnp.sum(q * k, axis=-1, keepdims=True) for q, k in zip(qs, ks)]
        states = [s_ref[0, i, hd] for hd in heads]
        outs = [qk[hd] * vs[hd] + jnp.sum(q_cols[hd] * states[hd], axis=0, keepdims=True) * gams[hd]
                for hd in heads]
        for hd in heads:
            s_out_ref[0, i, hd] = gams[hd] * states[hd] + k_cols[hd] * vs[hd]
        mus = [jnp.mean(o, axis=-1, keepdims=True) for o in outs]
        devs = [o - mu for o, mu in zip(outs, mus)]
        variances = [jnp.mean(d * d, axis=-1, keepdims=True) for d in devs]
        for hd in heads:
            o_ref[i, :, lanes[hd]] = devs[hd] * lax.rsqrt(variances[hd] + EPS) * g_ref[hd:hd + 1, :]


def _sample_retention(qr, kr, rv, state, ret_norm):
    n = qr.shape[0]
    gam = np.ascontiguousarray(
        np.broadcast_to(np.exp(_log_gamma() * 1.0)[:, None], (RET_HEADS, LANES)), dtype=np.float32)
    g = RET_SEQS_PER_STEP
    tok = pl.BlockSpec((g, 1, _RET_W), lambda i: (i, 0, 0))
    st = pl.BlockSpec((1, g, RET_HEADS, RET_DK, RET_DV), lambda i: (0, i, 0, 0, 0))
    r3 = lambda a: a.reshape(n, 1, _RET_W)
    o, s_new = pl.pallas_call(
        _ret_sample_kernel,
        grid=(n // g,),
        in_specs=[tok, tok, tok, st, _const_spec(gam.shape), _const_spec(ret_norm.shape)],
        out_specs=[tok, st],
        out_shape=[jax.ShapeDtypeStruct((n, 1, _RET_W), F32),
                   jax.ShapeDtypeStruct(state.shape, F32)],
        compiler_params=pltpu.CompilerParams(dimension_semantics=("arbitrary",)),
        name="sample_retention",
    )(r3(qr), r3(kr), r3(rv), state, gam, ret_norm)
    return o.reshape(n, _RET_W), s_new


def _absorb_q_kernel(qt_ref, wt_ref, o_ref):
    for hd in range(MLA_HEADS):
        o_ref[hd] = _dot(wt_ref[hd], qt_ref[hd * HEAD_PAD:(hd + 1) * HEAD_PAD, :])


def _absorb_q(qt, w_absorb_t):
    n = qt.shape[1]
    return pl.pallas_call(
        _absorb_q_kernel,
        out_shape=jax.ShapeDtypeStruct((MLA_HEADS, KEY_W, n), F32),
        name="sample_absorb_q",
    )(qt, w_absorb_t)


def _unabsorb_kernel(o_ref, w_ref, out_ref):
    acc = _dot(o_ref[0].astype(BF16), w_ref[0])
    for hd in range(1, MLA_HEADS):
        acc = acc + _dot(o_ref[hd].astype(BF16), w_ref[hd])
    out_ref[...] = acc.astype(out_ref.dtype)


def _unabsorb(o_lat, w_unabsorb):
    n = o_lat.shape[1]
    return pl.pallas_call(
        _unabsorb_kernel,
        out_shape=jax.ShapeDtypeStruct((n, _ATT_W), BF16),
        name="sample_unabsorb",
    )(o_lat, w_unabsorb)


def _paged_kernel(pt_ref, q_ref, knew_ref, ckv_hbm, kpet_hbm, o_ref, buf, kpe_buf, sem):
    b = pl.program_id(0)
    n_seq = pl.num_programs(0)
    n_chunks = pt_ref.shape[1] // PAGES_PER_STEP
    assert n_chunks == PAGE_SLOTS
    ahead = PAGE_SLOTS - 1
    chunk_keys = PAGES_PER_STEP * PAGE_SIZE
    group_keys = chunk_keys // PAGE_GROUPS

    def page_copies(bb, cc, sl):
        out = []
        for j in range(PAGES_PER_STEP):
            pg = pt_ref[bb, cc * PAGES_PER_STEP + j]
            keys = pl.ds(j * PAGE_SIZE, PAGE_SIZE)
            out.append(pltpu.make_async_copy(ckv_hbm.at[0, pg], buf.at[sl, keys], sem.at[sl, 0]))
            out.append(pltpu.make_async_copy(kpet_hbm.at[0, pg], kpe_buf.at[sl, :, keys],
                                             sem.at[sl, 1]))
        return out

    @pl.when(b == 0)
    def _():
        for c in range(ahead):
            for cp in page_copies(0, c, c):
                cp.start()

    b_next = jnp.minimum(b + 1, n_seq - 1)

    q = q_ref[0]
    q_lat = q[:, 0:KV_LORA]
    q_pe = q[:, KV_LORA:KV_LORA + MLA_ROPE]

    def scores(sl, g):
        keys = slice(g * group_keys, (g + 1) * group_keys)
        return _dot_nt(q_lat, buf[sl, keys, :]) + _dot(q_pe, kpe_buf[sl, :, keys])

    m_run = jnp.full((MLA_HEADS, 1), NEG_BIG, F32)
    l_run = jnp.zeros((MLA_HEADS, 1), F32)
    acc_run = jnp.zeros((MLA_HEADS, KV_LORA), F32)
    def wait_chunk(c):
        pltpu.make_async_copy(buf.at[c], buf.at[c], sem.at[c, 0]).wait()
        pltpu.make_async_copy(kpe_buf.at[c], kpe_buf.at[c], sem.at[c, 1]).wait()

    wait_chunk(0)
    s_next = scores(0, 0)
    for c in range(n_chunks):
        sl = c
        nxt = c + ahead
        prefetch = (page_copies(b, nxt, nxt) if nxt < n_chunks
                    else page_copies(b_next, nxt - n_chunks, nxt - n_chunks))
        parts = []
        per_group = len(prefetch) // PAGE_GROUPS
        for g in range(PAGE_GROUPS):
            s = s_next
            if g + 1 < PAGE_GROUPS:
                s_next = scores(sl, g + 1)
            elif c + 1 < n_chunks:
                wait_chunk(c + 1)
                s_next = scores(c + 1, 0)
            for cp in prefetch[g * per_group:(g + 1) * per_group]:
                cp.start()
            m_g = jnp.max(s, axis=-1, keepdims=True)
            p = jnp.exp2(s - m_g)
            l_g = jnp.sum(p, axis=-1, keepdims=True)
            keys = slice(g * group_keys, (g + 1) * group_keys)
            parts.append((m_g, l_g, _dot(p, buf[sl, keys, :])))

        m_new = m_run
        for m_g, _, _ in parts:
            m_new = jnp.maximum(m_new, m_g)
        a_run = jnp.exp2(m_run - m_new)
        l_run = a_run * l_run
        acc_run = a_run * acc_run
        for m_g, l_g, o_g in parts:
            a_g = jnp.exp2(m_g - m_new)
            l_run = l_run + a_g * l_g
            acc_run = acc_run + a_g * o_g
        m_run = m_new

    kn = knew_ref[0]
    s_self = jnp.sum(q * kn, axis=-1, keepdims=True)
    m_fin = jnp.maximum(m_run, s_self)
    a = jnp.exp2(m_run - m_fin)
    p_self = jnp.exp2(s_self - m_fin)
    o_ref[0] = (a * acc_run + p_self * kn[:, 0:KV_LORA]) / (a * l_run + p_self)

    @pl.when(b == n_seq - 1)
    def _():
        for c in range(ahead):
            wait_chunk(c)


def _paged_attention(page_table, q_abs, k_new, cache_ckv, cache_kpe_t):
    n, n_pages = page_table.shape
    chunk_keys = PAGES_PER_STEP * PAGE_SIZE
    grid_spec = pltpu.PrefetchScalarGridSpec(
        num_scalar_prefetch=1,
        grid=(n,),
        in_specs=[
            pl.BlockSpec((1, MLA_HEADS, KEY_W), lambda b, pt: (b, 0, 0)),
            pl.BlockSpec((1, 1, KEY_W), lambda b, pt: (b, 0, 0)),
            pl.BlockSpec(memory_space=pl.ANY),
            pl.BlockSpec(memory_space=pl.ANY),
        ],
        out_specs=pl.BlockSpec((1, MLA_HEADS, KV_LORA), lambda b, pt: (b, 0, 0)),
        scratch_shapes=[pltpu.VMEM((PAGE_SLOTS, chunk_keys, KV_LORA), F32),
                        pltpu.VMEM((PAGE_SLOTS, MLA_ROPE, chunk_keys), F32),
                        pltpu.SemaphoreType.DMA((PAGE_SLOTS, 2))],
    )
    return pl.pallas_call(
        _paged_kernel,
        grid_spec=grid_spec,
        out_shape=jax.ShapeDtypeStruct((n, MLA_HEADS, KV_LORA), F32),
        compiler_params=pltpu.CompilerParams(dimension_semantics=("arbitrary",),
                                             vmem_limit_bytes=VMEM_LIMIT),
        name="sample_paged_attention",
    )(page_table, q_abs, k_new, cache_ckv, cache_kpe_t)


def _back_kernel(h_ref, oatt_ref, oret_ref, p_ref,
                 gmix_ref, wgate_ref, wba_ref, wbr_ref, wout_ref,
                 g2_ref, wg_ref, wu_ref, wd_ref, gple_ref, wpg_ref, wpp_ref, gfin_ref, y_ref):
    h = h_ref[...]
    un = _rms(h, gmix_ref[...]).astype(BF16)
    gates = _dot(un, wgate_ref[...])
    rg = gates[:, :_RET_W]
    ga = gates[:, _RET_W:_RET_W + D_MODEL]
    gr = gates[:, _RET_W + D_MODEL:]
    o_ret = (rg * jax.nn.sigmoid(rg) * oret_ref[...]).astype(BF16)
    merged = (jax.nn.sigmoid(ga) * _dot(oatt_ref[...], wba_ref[...])
              + jax.nn.sigmoid(gr) * _dot(o_ret, wbr_ref[...]))
    h = h + _dot(merged.astype(BF16), wout_ref[...])
    h = _swiglu_half(h, g2_ref[...], wg_ref[...], wu_ref[...], wd_ref[...])
    gate = jax.nn.sigmoid(_dot(_rms(h, gple_ref[...]).astype(BF16), wpg_ref[...]))
    h = h + gate * _dot(p_ref[...].astype(BF16), wpp_ref[...])
    y_ref[...] = _rms(h, gfin_ref[...])


def _back_stage(h, o_att, o_ret, p_emb, w, tile):
    n = h.shape[0]
    row = lambda width: pl.BlockSpec((tile, width), lambda i: (i, 0))
    weights = [w['mix_norm'], w['w_in_gates'], w['w_branch_att'], w['w_branch_ret'], w['w_out'],
               w['ffn2_norm'], w['ffn2_w_gate'], w['ffn2_w_up'], w['ffn2_w_down'],
               w['ple_norm'], w['w_ple_gate'], w['w_ple_proj'], w['final_norm']]
    return pl.pallas_call(
        _back_kernel,
        grid=(n // tile,),
        in_specs=[row(D_MODEL), row(_ATT_W), row(_RET_W), row(PLE_DIM)]
        + [_const_spec(a.shape) for a in weights],
        out_specs=row(D_MODEL),
        out_shape=jax.ShapeDtypeStruct((n, D_MODEL), F32),
        compiler_params=pltpu.CompilerParams(dimension_semantics=("arbitrary",),
                                             vmem_limit_bytes=VMEM_LIMIT),
        name="back_stage",
    )(h, o_att, o_ret, p_emb, *weights)


def _rope_tables(pos):
    pos = np.asarray(pos, np.float64)[:, None]
    n = pos.shape[0]
    inv_m = ROPE_THETA ** (-np.arange(ROPE_HALF, dtype=np.float64) / ROPE_HALF)
    cos_m, sin_m = np.cos(pos * inv_m[None, :]), np.sin(pos * inv_m[None, :])
    z = lambda width: np.zeros((n, width))
    tail = HEAD_PAD - ROPE_OFF - MLA_ROPE
    cm = np.concatenate([np.ones((n, ROPE_OFF)), cos_m, cos_m, z(tail)], axis=1)
    s1 = np.concatenate([z(ROPE_OFF), -sin_m, z(ROPE_HALF), z(tail)], axis=1)
    s2 = np.concatenate([z(ROPE_OFF), z(ROPE_HALF), sin_m, z(tail)], axis=1)
    half = RET_DK // 2
    inv_r = ROPE_THETA ** (-np.arange(half, dtype=np.float64) / half)
    cos_r, sin_r = np.cos(pos * inv_r[None, :]), np.sin(pos * inv_r[None, :])
    cr = np.concatenate([cos_r, cos_r], axis=1)
    sr = np.concatenate([-sin_r, sin_r], axis=1)
    tables = (cm, s1, s2, cr, sr, cos_m.T, sin_m.T)
    return tuple(np.ascontiguousarray(t, dtype=np.float32) for t in tables)


def _layer_weights(i, ffn1_norm, ffn1_w_gate, ffn1_w_up, ffn1_w_down, mix_norm, w_in, q_a_norm, w_q_b,
                   kv_a_norm, w_kv_b, ret_norm, w_branch_att, w_branch_ret, w_out,
                   ffn2_norm, ffn2_w_gate, ffn2_w_up, ffn2_w_down, ple_norm, w_ple_gate, w_ple_proj,
                   final_norm):
    vec = lambda a: a.reshape(1, -1)
    bf = lambda a: a.astype(BF16)
    win = w_in[i]
    zcols = lambda width: jnp.zeros((D_MODEL, width), F32)
    w_in_front = jnp.concatenate(
        [win[:, :_OFF_KPE], zcols(ROPE_OFF), win[:, _OFF_KPE:_OFF_RQ],
         zcols(HEAD_PAD - ROPE_OFF - MLA_ROPE), win[:, _OFF_RQ:_OFF_RG]], axis=1)
    wqb = w_q_b[i].reshape(Q_LORA, MLA_HEADS, MLA_NOPE + MLA_ROPE)
    wqb = jnp.pad(wqb, ((0, 0), (0, 0), (0, HEAD_PAD - MLA_NOPE - MLA_ROPE)))
    wkv = w_kv_b[i].reshape(KV_LORA, MLA_HEADS, MLA_NOPE + MLA_V)
    w_uk, w_uv = wkv[..., :MLA_NOPE], wkv[..., MLA_NOPE:]
    w_k = jnp.pad(w_uk, ((0, 0), (0, 0), (0, HEAD_PAD - MLA_NOPE)))
    pass_rope = np.zeros((1, KEY_W, HEAD_PAD), np.float32)
    pass_rope[0, KV_LORA:KV_LORA + MLA_ROPE, MLA_NOPE:MLA_NOPE + MLA_ROPE] = np.eye(MLA_ROPE)
    w_absorb_t = jnp.pad(jnp.transpose(w_uk, (1, 0, 2)),
                         ((0, 0), (0, KEY_W - KV_LORA), (0, HEAD_PAD - MLA_NOPE))) + pass_rope
    own_cols = np.eye(MLA_HEADS, dtype=np.float32)[:, None, :, None]
    w_unabsorb = (own_cols * jnp.transpose(w_uv, (1, 0, 2))[:, :, None, :]).reshape(
        MLA_HEADS, KV_LORA, _ATT_W)
    return {
        'ffn1_norm': vec(ffn1_norm[i]), 'ffn1_w_gate': bf(ffn1_w_gate[i]), 'ffn1_w_up': bf(ffn1_w_up[i]),
        'ffn1_w_down': bf(ffn1_w_down[i]), 'mix_norm': vec(mix_norm[i]),
        'w_in_front': bf(w_in_front), 'w_in_gates': bf(win[:, _OFF_RG:]),
        'q_a_norm': vec(q_a_norm[i]),
        'w_q_bt': bf(jnp.transpose(wqb.reshape(Q_LORA, MLA_HEADS * HEAD_PAD))),
        'kv_a_norm': vec(kv_a_norm[i]), 'w_k': bf(w_k.reshape(KV_LORA, MLA_HEADS * HEAD_PAD)),
        'w_vt': bf(jnp.pad(jnp.transpose(w_uv, (1, 2, 0)), ((0, 0), (0, V_ROWS - MLA_V), (0, 0)))
                   .reshape(_VT_ROWS, KV_LORA)),
        'w_absorb_t': bf(w_absorb_t), 'w_unabsorb': bf(w_unabsorb),
        'ret_norm': ret_norm[i],
        'w_branch_att': bf(w_branch_att[i]), 'w_branch_ret': bf(w_branch_ret[i]), 'w_out': bf(w_out[i]),
        'ffn2_norm': vec(ffn2_norm[i]), 'ffn2_w_gate': bf(ffn2_w_gate[i]), 'ffn2_w_up': bf(ffn2_w_up[i]),
        'ffn2_w_down': bf(ffn2_w_down[i]), 'ple_norm': vec(ple_norm[i]),
        'w_ple_gate': bf(w_ple_gate[i]), 'w_ple_proj': bf(w_ple_proj[i]),
        'final_norm': vec(final_norm),
    }


def kernel(x_prompt, x_sample, cache_ckv, cache_kpe, state_ret, page_table, p_prompt, p_sample, ffn1_norm, ffn1_w_gate, ffn1_w_up, ffn1_w_down, mix_norm, w_in, q_a_norm, w_q_b, kv_a_norm, w_kv_b, ret_norm, w_branch_att, w_branch_ret, w_out, ffn2_norm, ffn2_w_gate, ffn2_w_up, ffn2_w_down, ple_norm, w_ple_gate, w_ple_proj, final_norm):
    batch, seq, _ = x_prompt.shape
    n_dec, dec_seq, _ = x_sample.shape
    depth = w_in.shape[0]
    assert dec_seq == 1 and depth == 1
    n_past = page_table.shape[1] * PAGE_SIZE

    w = _layer_weights(0, ffn1_norm, ffn1_w_gate, ffn1_w_up, ffn1_w_down, mix_norm, w_in, q_a_norm,
                       w_q_b, kv_a_norm, w_kv_b, ret_norm, w_branch_att, w_branch_ret, w_out,
                       ffn2_norm, ffn2_w_gate, ffn2_w_up, ffn2_w_down, ple_norm, w_ple_gate,
                       w_ple_proj, final_norm)

    tabs_p = _rope_tables(np.arange(seq))
    (h_p, qt_p, k_p, vt_p, ckv_p, kpet_p, qr_p, kr_p, rv_p) = _front_stage(
        x_prompt.reshape(batch * seq, D_MODEL), tabs_p, w, TOKEN_TILE, seq // TOKEN_TILE)
    bt = lambda a: a.reshape(batch, seq, a.shape[-1])
    o_att_p = _prompt_attention(qt_p, k_p, vt_p, batch, seq)
    o_ret_p, ret_p = _prompt_retention(bt(qr_p), bt(kr_p), bt(rv_p), w['ret_norm'])
    y_p = _back_stage(h_p, o_att_p, o_ret_p.reshape(batch * seq, -1),
                      p_prompt.reshape(batch * seq, PLE_DIM), w, TOKEN_TILE)

    tabs_1 = _rope_tables(np.full((1,), n_past))
    tabs_s = (tuple(np.ascontiguousarray(np.broadcast_to(t, (n_dec, LANES))) for t in tabs_1[:5])
              + tuple(np.ascontiguousarray(np.broadcast_to(t, (ROPE_HALF, n_dec))) for t in tabs_1[5:]))
    (h_s, qt_s, _, _, ckv_s, kpet_s, qr_s, kr_s, rv_s) = _front_stage(
        x_sample.reshape(n_dec, D_MODEL), tabs_s, w, n_dec, 1)
    q_abs = jnp.transpose(_absorb_q(qt_s, w['w_absorb_t']), (2, 0, 1))
    kpe_s = jnp.transpose(kpet_s[0])
    k_new = jnp.concatenate([ckv_s, kpe_s,
                             jnp.zeros((n_dec, KEY_W - KV_LORA - MLA_ROPE), F32)], axis=1)
    o_lat = _paged_attention(page_table, q_abs, k_new.reshape(n_dec, 1, KEY_W), cache_ckv,
                             jnp.swapaxes(cache_kpe, 2, 3))
    o_att_s = _unabsorb(jnp.transpose(o_lat, (1, 0, 2)), w['w_unabsorb'])
    o_ret_s, ret_s = _sample_retention(qr_s, kr_s, rv_s, state_ret, w['ret_norm'])
    y_s = _back_stage(h_s, o_att_s, o_ret_s, p_sample.reshape(n_dec, PLE_DIM), w, n_dec)

    return (y_p.reshape(batch, seq, D_MODEL),
            y_s.reshape(n_dec, 1, D_MODEL),
            ckv_p.reshape(1, batch, seq, KV_LORA),
            jnp.transpose(kpet_p, (0, 2, 1))[None],
            ret_p[None],
            ckv_s.reshape(1, n_dec, 1, KV_LORA),
            kpe_s.reshape(1, n_dec, 1, MLA_ROPE),
            ret_s)
```

```python
import functools

import jax
import jax.numpy as jnp
import numpy as np
from jax import lax
from jax.experimental import pallas as pl
from jax.experimental.pallas import tpu as pltpu

F32 = jnp.float32
BF16 = jnp.bfloat16

D_MODEL = 1024
D_FF = 2816
PLE_DIM = 256
MLA_HEADS = 8
MLA_NOPE = 64
MLA_ROPE = 32
MLA_V = 64
Q_LORA = 256
KV_LORA = 128
RET_HEADS = 4
RET_DK = 128
RET_DV = 128
PAGE_SIZE = 128
ROPE_THETA = 10000.0
EPS = 1e-6

LANES = 128
HEAD_PAD = LANES
ROPE_OFF = MLA_NOPE
ROPE_HALF = MLA_ROPE // 2
SOFTMAX_SCALE = (MLA_NOPE + MLA_ROPE) ** -0.5
LOG2E = 1.4426950408889634
Q_SCALE = SOFTMAX_SCALE * LOG2E
NEG_BIG = -1e30

TOKEN_TILE = 512
ATTN_TILE = 512
RET_CHUNK = 512
PAGES_PER_STEP = 32
PAGE_GROUPS = 4
PAGE_SLOTS = 4
RET_SEQS_PER_STEP = 8
KEY_W = 2 * LANES
VMEM_LIMIT = 60 * 1024 * 1024

_OFF_CQ = 0
_OFF_CKV = _OFF_CQ + Q_LORA
_OFF_KPE = _OFF_CKV + KV_LORA
_OFF_RQ = _OFF_KPE + MLA_ROPE
_OFF_RK = _OFF_RQ + RET_HEADS * RET_DK
_OFF_RV = _OFF_RK + RET_HEADS * RET_DK
_OFF_RG = _OFF_RV + RET_HEADS * RET_DV
_RET_W = RET_HEADS * RET_DK
_ATT_W = MLA_HEADS * MLA_V
_GATE_W = _RET_W + 2 * D_MODEL
_FRONT_W = Q_LORA + KV_LORA + LANES + 3 * _RET_W
_FRONT_BLOCK = -(-_GATE_W // _FRONT_W)
BF16_TILE_ROWS = 16
V_ROWS = MLA_V + BF16_TILE_ROWS
_VT_ROWS = MLA_HEADS * V_ROWS


def _rms(x, g):
    return x * lax.rsqrt(jnp.mean(x * x, axis=-1, keepdims=True) + EPS) * g


def _dot(a, b):
    return jnp.dot(a, b, preferred_element_type=F32)


def _dot_nt(a, b):
    return lax.dot_general(a, b, (((1,), (1,)), ((), ())), preferred_element_type=F32)


def _swiglu_half(x, g, wg, wu, wd):
    xn = _rms(x, g).astype(BF16)
    gate = _dot(xn, wg)
    up = _dot(xn, wu)
    act = (gate * jax.nn.sigmoid(gate) * up).astype(BF16)
    return x + 0.5 * _dot(act, wd)


def _const_spec(shape, index=None):
    index = (0,) * len(shape) if index is None else index
    return pl.BlockSpec(shape, lambda *_: index, pipeline_mode=pl.Buffered(1))


def _front_kernel(x_ref, cm_ref, s1_ref, s2_ref, cr_ref, sr_ref, ct_ref, st_ref,
                  g1_ref, wg_ref, wu_ref, wd_ref, gmix_ref, win_ref,
                  gqa_ref, wqbt_ref, gkv_ref, wk_ref, wvt_ref,
                  h_ref, qt_ref, k_ref, vt_ref, ckv_ref, kpet_ref, qr_ref, kr_ref, rv_ref):
    x = x_ref[...]
    h = _swiglu_half(x, g1_ref[...], wg_ref[...], wu_ref[...], wd_ref[...])
    h_ref[...] = h
    un = _rms(h, gmix_ref[...]).astype(BF16)
    z = _dot(un, win_ref[...])

    cm, s1, s2 = cm_ref[...], s1_ref[...], s2_ref[...]

    def mla_rope(t):
        return (t * cm + pltpu.roll(t, LANES - ROPE_HALF, 1) * s1
                + pltpu.roll(t, ROPE_HALF, 1) * s2)

    cq = z[:, _OFF_CQ:_OFF_CQ + Q_LORA]
    qt = _dot_nt(wqbt_ref[...], _rms(cq, gqa_ref[...]).astype(BF16))
    ct, st = ct_ref[...], st_ref[...]
    c_kv = _rms(z[:, Q_LORA:Q_LORA + KV_LORA], gkv_ref[...])
    ckv_ref[...] = c_kv
    c_kv_b = c_kv.astype(BF16)
    k_nope = _dot(c_kv_b, wk_ref[...])
    vt = _dot_nt(wvt_ref[...], c_kv_b)
    sum_row = lax.broadcasted_iota(jnp.int32, vt.shape, 0) % V_ROWS >= MLA_V
    vt_ref[...] = jnp.where(sum_row, 1.0, vt).astype(BF16)
    k_pe = mla_rope(z[:, Q_LORA + KV_LORA:Q_LORA + KV_LORA + LANES])
    kpet_ref[0] = jnp.transpose(k_pe)[ROPE_OFF:ROPE_OFF + MLA_ROPE, :]
    for hd in range(MLA_HEADS):
        sl = slice(hd * HEAD_PAD, (hd + 1) * HEAD_PAD)
        k_ref[:, sl] = (k_nope[:, sl] + k_pe).astype(BF16)
        r0 = hd * HEAD_PAD + ROPE_OFF
        x1 = qt[r0:r0 + ROPE_HALF]
        x2 = qt[r0 + ROPE_HALF:r0 + MLA_ROPE]
        qt_ref[hd * HEAD_PAD:r0, :] = (qt[hd * HEAD_PAD:r0] * Q_SCALE).astype(BF16)
        qt_ref[r0:r0 + ROPE_HALF, :] = ((x1 * ct - x2 * st) * Q_SCALE).astype(BF16)
        qt_ref[r0 + ROPE_HALF:r0 + MLA_ROPE, :] = ((x1 * st + x2 * ct) * Q_SCALE).astype(BF16)
        qt_ref[r0 + MLA_ROPE:(hd + 1) * HEAD_PAD, :] = jnp.zeros(
            (HEAD_PAD - ROPE_OFF - MLA_ROPE, qt.shape[1]), BF16)

    cr, sr = cr_ref[...], sr_ref[...]
    base = Q_LORA + KV_LORA + LANES
    for hd in range(RET_HEADS):
        sl = slice(hd * RET_DK, (hd + 1) * RET_DK)
        rq = z[:, base + hd * RET_DK:base + (hd + 1) * RET_DK]
        rk = z[:, base + _RET_W + hd * RET_DK:base + _RET_W + (hd + 1) * RET_DK]
        qr_ref[:, sl] = rq * cr + pltpu.roll(rq, RET_DK // 2, 1) * sr
        kr_ref[:, sl] = (rk * cr + pltpu.roll(rk, RET_DK // 2, 1) * sr) * (RET_DK ** -0.5)
    rv_ref[...] = z[:, base + 2 * _RET_W:base + 3 * _RET_W]


def _front_stage(x, tables, w, tile, table_tiles):
    n = x.shape[0]
    seq_rows = tile * table_tiles
    row = lambda width: pl.BlockSpec((tile, width), lambda i: (i, 0))
    col = lambda height: pl.BlockSpec((height, tile), lambda i: (0, i))
    tab = pl.BlockSpec((tile, LANES), lambda i: (i % table_tiles, 0))
    tab_t = pl.BlockSpec((ROPE_HALF, tile), lambda i: (0, i % table_tiles))
    weights = [w['ffn1_norm'], w['ffn1_w_gate'], w['ffn1_w_up'], w['ffn1_w_down'], w['mix_norm'],
               w['w_in_all'], w['q_a_norm'], w['w_q_bt'], w['kv_a_norm'], w['w_k'], w['w_vt']]
    w_specs = [_const_spec((D_MODEL, _FRONT_W), (0, _FRONT_BLOCK)) if a is w['w_in_all']
               else _const_spec(a.shape) for a in weights]
    outs = [((n, D_MODEL), F32, row(D_MODEL)),
            ((MLA_HEADS * HEAD_PAD, n), BF16, col(MLA_HEADS * HEAD_PAD)),
            ((n, MLA_HEADS * HEAD_PAD), BF16, row(MLA_HEADS * HEAD_PAD)),
            ((_VT_ROWS, n), BF16, col(_VT_ROWS)),
            ((n, KV_LORA), F32, row(KV_LORA)),
            ((n // seq_rows, MLA_ROPE, seq_rows), F32,
             pl.BlockSpec((1, MLA_ROPE, tile), lambda i: (i // table_tiles, 0, i % table_tiles))),
            ((n, _RET_W), F32, row(_RET_W)),
            ((n, _RET_W), F32, row(_RET_W)),
            ((n, _RET_W), F32, row(_RET_W))]
    return pl.pallas_call(
        _front_kernel,
        grid=(n // tile,),
        in_specs=[row(D_MODEL)] + [tab] * 5 + [tab_t] * 2 + w_specs,
        out_specs=[spec for _, _, spec in outs],
        out_shape=[jax.ShapeDtypeStruct(shape, dt) for shape, dt, _ in outs],
        compiler_params=pltpu.CompilerParams(dimension_semantics=("arbitrary",),
                                             vmem_limit_bytes=VMEM_LIMIT),
        name="front_stage",
    )(x, *tables, *weights)


def _attn_kernel(batch, qi_ref, ki_ref, *refs):
    qt_refs, k_refs, vt_refs = refs[:batch], refs[batch:2 * batch], refs[2 * batch:3 * batch]
    o_ref, m_sc, acc_sc = refs[3 * batch:]
    t = pl.program_id(0)
    qi = qi_ref[t]
    ki = ki_ref[t]
    units = [(b, hd) for b in range(batch) for hd in range(MLA_HEADS)]

    @pl.when(ki == 0)
    def _():
        m_sc[...] = jnp.full(m_sc.shape, NEG_BIG, F32)
        acc_sc[...] = jnp.zeros(acc_sc.shape, F32)

    def scores(unit):
        b, hd = unit
        kh = k_refs[b][:, hd * HEAD_PAD:(hd + 1) * HEAD_PAD]
        return _dot(kh, qt_refs[b][hd * HEAD_PAD:(hd + 1) * HEAD_PAD, :])

    def value_update(unit, alpha, p):
        b, hd = unit
        i = b * MLA_HEADS + hd
        acc_sc[i] = alpha * acc_sc[i] + _dot(vt_refs[b][hd * V_ROWS:(hd + 1) * V_ROWS, :], p)

    def update(masked):
        st_next = scores(units[0])
        pending = None
        for n, unit in enumerate(units):
            i = unit[0] * MLA_HEADS + unit[1]
            st = st_next
            if n + 1 < len(units):
                st_next = scores(units[n + 1])
            if masked:
                key = lax.broadcasted_iota(jnp.int32, st.shape, 0)
                qry = lax.broadcasted_iota(jnp.int32, st.shape, 1)
                st = jnp.where(key <= qry, st, NEG_BIG)
            m_prev = m_sc[i]
            m_new = jnp.maximum(m_prev, jnp.max(st, axis=0, keepdims=True))
            alpha = jnp.exp2(m_prev - m_new)
            p = jnp.exp2(st - m_new).astype(BF16)
            m_sc[i] = m_new
            if pending is not None:
                value_update(*pending)
            pending = (unit, alpha, p)
        value_update(*pending)

    @pl.when(ki < qi)
    def _():
        update(False)

    @pl.when(ki == qi)
    def _():
        update(True)
        acc = acc_sc[...]
        o_t = (acc[:, :MLA_V, :] / acc[:, MLA_V:MLA_V + 1, :]).reshape(batch, _ATT_W, ATTN_TILE)
        for b in range(batch):
            o_ref[b] = jnp.transpose(o_t[b]).astype(o_ref.dtype)


def _prompt_attention(qt, k, vt, batch, seq):
    nt = seq // ATTN_TILE
    qi_list = np.array([i for i in range(nt) for _ in range(i + 1)], np.int32)
    ki_list = np.array([j for i in range(nt) for j in range(i + 1)], np.int32)
    qk_w = MLA_HEADS * HEAD_PAD
    seqs = range(batch)
    q_specs = [pl.BlockSpec((qk_w, ATTN_TILE), lambda s, qi, ki, b=b: (0, b * nt + qi[s])) for b in seqs]
    k_specs = [pl.BlockSpec((ATTN_TILE, qk_w), lambda s, qi, ki, b=b: (b * nt + ki[s], 0)) for b in seqs]
    v_specs = [pl.BlockSpec((_VT_ROWS, ATTN_TILE), lambda s, qi, ki, b=b: (0, b * nt + ki[s]))
               for b in seqs]
    grid_spec = pltpu.PrefetchScalarGridSpec(
        num_scalar_prefetch=2,
        grid=(len(qi_list),),
        in_specs=q_specs + k_specs + v_specs,
        out_specs=pl.BlockSpec((batch, ATTN_TILE, _ATT_W), lambda s, qi, ki: (0, qi[s], 0)),
        scratch_shapes=[pltpu.VMEM((batch * MLA_HEADS, 1, ATTN_TILE), F32),
                        pltpu.VMEM((batch * MLA_HEADS, V_ROWS, ATTN_TILE), F32)],
    )
    out = pl.pallas_call(
        functools.partial(_attn_kernel, batch),
        grid_spec=grid_spec,
        out_shape=jax.ShapeDtypeStruct((batch, seq, _ATT_W), BF16),
        compiler_params=pltpu.CompilerParams(dimension_semantics=("arbitrary",),
                                             vmem_limit_bytes=VMEM_LIMIT),
        name="prompt_attention",
    )(jnp.asarray(qi_list), jnp.asarray(ki_list), *([qt] * batch), *([k] * batch), *([vt] * batch))
    return out.reshape(batch * seq, _ATT_W)


def _head_layernorm(o, g):
    mu = jnp.mean(o, axis=-1, keepdims=True)
    d = o - mu
    var = jnp.mean(d * d, axis=-1, keepdims=True)
    return d * lax.rsqrt(var + EPS) * g


def _ret_prompt_kernel(q_ref, k_ref, v_ref, dec_ref, qd_ref, kd_ref, gc_ref, g_ref,
                       o_ref, s_out_ref, s_sc):
    c = pl.program_id(1)

    @pl.when(c == 0)
    def _():
        s_sc[...] = jnp.zeros(s_sc.shape, F32)

    for hd in range(RET_HEADS):
        sl = slice(hd * RET_DK, (hd + 1) * RET_DK)
        q = q_ref[0, :, sl]
        k = k_ref[0, :, sl]
        vb = v_ref[0, :, sl].astype(BF16)
        qb = q.astype(BF16)
        state = s_sc[hd]
        scores = _dot_nt(qb, k.astype(BF16)) * dec_ref[hd]
        inner = _dot(scores.astype(BF16), vb)
        cross = _dot(qb, state.astype(BF16)) * qd_ref[hd]
        o_ref[0, :, sl] = _head_layernorm(inner + cross, g_ref[hd:hd + 1, :])
        k_dec_t = jnp.transpose(k * kd_ref[hd]).astype(BF16)
        s_sc[hd] = gc_ref[hd:hd + 1, :] * state + _dot(k_dec_t, vb)

    @pl.when(c == pl.num_programs(1) - 1)
    def _():
        s_out_ref[0] = s_sc[...]


def _log_gamma():
    return np.log1p(-np.exp2(-5.0 - np.arange(RET_HEADS, dtype=np.float64)))


def _retention_consts(chunk):
    log_gamma = _log_gamma()
    idx = np.arange(chunk, dtype=np.float64)
    diff = idx[:, None] - idx[None, :]
    decay = np.where(diff[None] >= 0,
                     np.exp(log_gamma[:, None, None] * np.maximum(diff, 0.0)[None]), 0.0)
    q_decay = np.exp(log_gamma[:, None] * (idx + 1.0)[None, :])
    k_decay = np.exp(log_gamma[:, None] * (chunk - 1.0 - idx)[None, :])
    chunk_decay = np.exp(log_gamma * chunk)
    lane = lambda a: np.broadcast_to(a[..., None], a.shape + (LANES,))
    tables = (decay, lane(q_decay), lane(k_decay), lane(chunk_decay))
    return tuple(np.ascontiguousarray(t, dtype=np.float32) for t in tables)


def _prompt_retention(qr, kr, rv, ret_norm):
    b, t, _ = qr.shape
    decay, q_decay, k_decay, chunk_decay = _retention_consts(RET_CHUNK)
    seq = pl.BlockSpec((1, RET_CHUNK, _RET_W), lambda bb, c: (bb, c, 0))
    return pl.pallas_call(
        _ret_prompt_kernel,
        grid=(b, t // RET_CHUNK),
        in_specs=[seq, seq, seq, _const_spec(decay.shape), _const_spec(q_decay.shape),
                  _const_spec(k_decay.shape), _const_spec(chunk_decay.shape),
                  _const_spec(ret_norm.shape)],
        out_specs=[seq, pl.BlockSpec((1, RET_HEADS, RET_DK, RET_DV), lambda bb, c: (bb, 0, 0, 0))],
        out_shape=[jax.ShapeDtypeStruct((b, t, _RET_W), F32),
                   jax.ShapeDtypeStruct((b, RET_HEADS, RET_DK, RET_DV), F32)],
        scratch_shapes=[pltpu.VMEM((RET_HEADS, RET_DK, RET_DV), F32)],
        compiler_params=pltpu.CompilerParams(dimension_semantics=("arbitrary", "arbitrary")),
        name="prompt_retention",
    )(qr, kr, rv, decay, q_decay, k_decay, chunk_decay, ret_norm)


def _ret_sample_kernel(q_ref, k_ref, v_ref, s_ref, gam_ref, g_ref, o_ref, s_out_ref):
    rows = lax.broadcasted_iota(jnp.int32, (RET_DK, RET_DK), 0)
    cols = lax.broadcasted_iota(jnp.int32, (RET_DK, RET_DK), 1)
    eye = rows == cols

    def column(r):
        return jnp.sum(jnp.where(eye, jnp.broadcast_to(r, (RET_DK, RET_DK)), 0.0),
                       axis=1, keepdims=True)

    heads = range(RET_HEADS)
    lanes = [slice(hd * RET_DK, (hd + 1) * RET_DK) for hd in heads]
    gams = [gam_ref[hd:hd + 1, :] for hd in heads]
    for i in range(RET_SEQS_PER_STEP):
        qs = [q_ref[i, :, sl] for sl in lanes]
        ks = [k_ref[i, :, sl] for sl in lanes]
        vs = [v_ref[i, :, sl] for sl in lanes]
        q_cols = [column(q) for q in qs]
        k_cols = [column(k) for k in ks]
        qk = [jnp.sum(q * k, axis=-1, keepdims=True) for q, k in zip(qs, ks)]
        states = [s_ref[0, i, hd] for hd in heads]
        outs = [qk[hd] * vs[hd] + jnp.sum(q_cols[hd] * states[hd], axis=0, keepdims=True) * gams[hd]
                for hd in heads]
        for hd in heads:
            s_out_ref[0, i, hd] = gams[hd] * states[hd] + k_cols[hd] * vs[hd]
        mus = [jnp.mean(o, axis=-1, keepdims=True) for o in outs]
        devs = [o - mu for o, mu in zip(outs, mus)]
        variances = [jnp.mean(d * d, axis=-1, keepdims=True) for d in devs]
        for hd in heads:
            o_ref[i, :, lanes[hd]] = devs[hd] * lax.rsqrt(variances[hd] + EPS) * g_ref[hd:hd + 1, :]


def _sample_retention(qr, kr, rv, state, ret_norm):
    n = qr.shape[0]
    gam = np.ascontiguousarray(
        np.broadcast_to(np.exp(_log_gamma() * 1.0)[:, None], (RET_HEADS, LANES)), dtype=np.float32)
    g = RET_SEQS_PER_STEP
    tok = pl.BlockSpec((g, 1, _RET_W), lambda i: (i, 0, 0))
    st = pl.BlockSpec((1, g, RET_HEADS, RET_DK, RET_DV), lambda i: (0, i, 0, 0, 0))
    r3 = lambda a: a.reshape(n, 1, _RET_W)
    o, s_new = pl.pallas_call(
        _ret_sample_kernel,
        grid=(n // g,),
        in_specs=[tok, tok, tok, st, _const_spec(gam.shape), _const_spec(ret_norm.shape)],
        out_specs=[tok, st],
        out_shape=[jax.ShapeDtypeStruct((n, 1, _RET_W), F32),
                   jax.ShapeDtypeStruct(state.shape, F32)],
        compiler_params=pltpu.CompilerParams(dimension_semantics=("arbitrary",)),
        name="sample_retention",
    )(r3(qr), r3(kr), r3(rv), state, gam, ret_norm)
    return o.reshape(n, _RET_W), s_new


def _absorb_q_kernel(qt_ref, wt_ref, o_ref):
    for hd in range(MLA_HEADS):
        o_ref[hd] = _dot(wt_ref[hd], qt_ref[hd * HEAD_PAD:(hd + 1) * HEAD_PAD, :])


def _absorb_q(qt, w_absorb_t):
    n = qt.shape[1]
    return pl.pallas_call(
        _absorb_q_kernel,
        out_shape=jax.ShapeDtypeStruct((MLA_HEADS, KEY_W, n), F32),
        name="sample_absorb_q",
    )(qt, w_absorb_t)


def _unabsorb_kernel(o_ref, w_ref, out_ref):
    acc = _dot(o_ref[0].astype(BF16), w_ref[0])
    for hd in range(1, MLA_HEADS):
        acc = acc + _dot(o_ref[hd].astype(BF16), w_ref[hd])
    out_ref[...] = acc.astype(out_ref.dtype)


def _unabsorb(o_lat, w_unabsorb):
    n = o_lat.shape[1]
    return pl.pallas_call(
        _unabsorb_kernel,
        out_shape=jax.ShapeDtypeStruct((n, _ATT_W), BF16),
        name="sample_unabsorb",
    )(o_lat, w_unabsorb)


def _paged_kernel(pt_ref, q_ref, knew_ref, ckv_hbm, kpet_hbm, o_ref, buf, kpe_buf, sem):
    b = pl.program_id(0)
    n_seq = pl.num_programs(0)
    n_chunks = pt_ref.shape[1] // PAGES_PER_STEP
    assert n_chunks == PAGE_SLOTS
    ahead = PAGE_SLOTS - 1
    chunk_keys = PAGES_PER_STEP * PAGE_SIZE
    group_keys = chunk_keys // PAGE_GROUPS

    def page_copies(bb, cc, sl):
        out = []
        for j in range(PAGES_PER_STEP):
            pg = pt_ref[bb, cc * PAGES_PER_STEP + j]
            keys = pl.ds(j * PAGE_SIZE, PAGE_SIZE)
            out.append(pltpu.make_async_copy(ckv_hbm.at[0, pg], buf.at[sl, keys], sem.at[sl, 0]))
            out.append(pltpu.make_async_copy(kpet_hbm.at[0, pg], kpe_buf.at[sl, :, keys],
                                             sem.at[sl, 1]))
        return out

    @pl.when(b == 0)
    def _():
        for c in range(ahead):
            for cp in page_copies(0, c, c):
                cp.start()

    b_next = jnp.minimum(b + 1, n_seq - 1)

    q = q_ref[0]
    q_lat = q[:, 0:KV_LORA]
    q_pe = q[:, KV_LORA:KV_LORA + MLA_ROPE]

    def scores(sl, g):
        keys = slice(g * group_keys, (g + 1) * group_keys)
        return _dot_nt(q_lat, buf[sl, keys, :]) + _dot(q_pe, kpe_buf[sl, :, keys])

    m_run = jnp.full((MLA_HEADS, 1), NEG_BIG, F32)
    l_run = jnp.zeros((MLA_HEADS, 1), F32)
    acc_run = jnp.zeros((MLA_HEADS, KV_LORA), F32)
    def wait_chunk(c):
        pltpu.make_async_copy(buf.at[c], buf.at[c], sem.at[c, 0]).wait()
        pltpu.make_async_copy(kpe_buf.at[c], kpe_buf.at[c], sem.at[c, 1]).wait()

    wait_chunk(0)
    s_next = scores(0, 0)
    for c in range(n_chunks):
        sl = c
        nxt = c + ahead
        prefetch = (page_copies(b, nxt, nxt) if nxt < n_chunks
                    else page_copies(b_next, nxt - n_chunks, nxt - n_chunks))
        parts = []
        per_group = len(prefetch) // PAGE_GROUPS
        for g in range(PAGE_GROUPS):
            s = s_next
            if g + 1 < PAGE_GROUPS:
                s_next = scores(sl, g + 1)
            elif c + 1 < n_chunks:
                wait_chunk(c + 1)
                s_next = scores(c + 1, 0)
            for cp in prefetch[g * per_group:(g + 1) * per_group]:
                cp.start()
            m_g = jnp.max(s, axis=-1, keepdims=True)
            p = jnp.exp2(s - m_g)
            l_g = jnp.sum(p, axis=-1, keepdims=True)
            keys = slice(g * group_keys, (g + 1) * group_keys)
            parts.append((m_g, l_g, _dot(p, buf[sl, keys, :])))

        m_new = m_run
        for m_g, _, _ in parts:
            m_new = jnp.maximum(m_new, m_g)
        a_run = jnp.exp2(m_run - m_new)
        l_run = a_run * l_run
        acc_run = a_run * acc_run
        for m_g, l_g, o_g in parts:
            a_g = jnp.exp2(m_g - m_new)
            l_run = l_run + a_g * l_g
            acc_run = acc_run + a_g * o_g
        m_run = m_new

    kn = knew_ref[0]
    s_self = jnp.sum(q * kn, axis=-1, keepdims=True)
    m_fin = jnp.maximum(m_run, s_self)
    a = jnp.exp2(m_run - m_fin)
    p_self = jnp.exp2(s_self - m_fin)
    o_ref[0] = (a * acc_run + p_self * kn[:, 0:KV_LORA]) / (a * l_run + p_self)

    @pl.when(b == n_seq - 1)
    def _():
        for c in range(ahead):
            wait_chunk(c)


def _paged_attention(page_table, q_abs, k_new, cache_ckv, cache_kpe_t):
    n, n_pages = page_table.shape
    chunk_keys = PAGES_PER_STEP * PAGE_SIZE
    grid_spec = pltpu.PrefetchScalarGridSpec(
        num_scalar_prefetch=1,
        grid=(n,),
        in_specs=[
            pl.BlockSpec((1, MLA_HEADS, KEY_W), lambda b, pt: (b, 0, 0)),
            pl.BlockSpec((1, 1, KEY_W), lambda b, pt: (b, 0, 0)),
            pl.BlockSpec(memory_space=pl.ANY),
            pl.BlockSpec(memory_space=pl.ANY),
        ],
        out_specs=pl.BlockSpec((1, MLA_HEADS, KV_LORA), lambda b, pt: (b, 0, 0)),
        scratch_shapes=[pltpu.VMEM((PAGE_SLOTS, chunk_keys, KV_LORA), F32),
                        pltpu.VMEM((PAGE_SLOTS, MLA_ROPE, chunk_keys), F32),
                        pltpu.SemaphoreType.DMA((PAGE_SLOTS, 2))],
    )
    return pl.pallas_call(
        _paged_kernel,
        grid_spec=grid_spec,
        out_shape=jax.ShapeDtypeStruct((n, MLA_HEADS, KV_LORA), F32),
        compiler_params=pltpu.CompilerParams(dimension_semantics=("arbitrary",),
                                             vmem_limit_bytes=VMEM_LIMIT),
        name="sample_paged_attention",
    )(page_table, q_abs, k_new, cache_ckv, cache_kpe_t)


def _back_kernel(h_ref, oatt_ref, oret_ref, p_ref,
                 gmix_ref, wgate_ref, wba_ref, wbr_ref, wout_ref,
                 g2_ref, wg_ref, wu_ref, wd_ref, gple_ref, wpg_ref, wpp_ref, gfin_ref, y_ref):
    h = h_ref[...]
    un = _rms(h, gmix_ref[...]).astype(BF16)
    gates = _dot(un, wgate_ref[...])
    rg = gates[:, :_RET_W]
    ga = gates[:, _RET_W:_RET_W + D_MODEL]
    gr = gates[:, _RET_W + D_MODEL:]
    o_ret = (rg * jax.nn.sigmoid(rg) * oret_ref[...]).astype(BF16)
    merged = (jax.nn.sigmoid(ga) * _dot(oatt_ref[...], wba_ref[...])
              + jax.nn.sigmoid(gr) * _dot(o_ret, wbr_ref[...]))
    h = h + _dot(merged.astype(BF16), wout_ref[...])
    h = _swiglu_half(h, g2_ref[...], wg_ref[...], wu_ref[...], wd_ref[...])
    gate = jax.nn.sigmoid(_dot(_rms(h, gple_ref[...]).astype(BF16), wpg_ref[...]))
    h = h + gate * _dot(p_ref[...].astype(BF16), wpp_ref[...])
    y_ref[...] = _rms(h, gfin_ref[...])


def _back_stage(h, o_att, o_ret, p_emb, w, tile):
    n = h.shape[0]
    row = lambda width: pl.BlockSpec((tile, width), lambda i: (i, 0))
    weights = [w['mix_norm'], w['w_in_all'], w['w_branch_att'], w['w_branch_ret'], w['w_out'],
               w['ffn2_norm'], w['ffn2_w_gate'], w['ffn2_w_up'], w['ffn2_w_down'],
               w['ple_norm'], w['w_ple_gate'], w['w_ple_proj'], w['final_norm']]
    return pl.pallas_call(
        _back_kernel,
        grid=(n // tile,),
        in_specs=[row(D_MODEL), row(_ATT_W), row(_RET_W), row(PLE_DIM)]
        + [_const_spec((D_MODEL, _GATE_W)) if a is w['w_in_all'] else _const_spec(a.shape)
           for a in weights],
        out_specs=row(D_MODEL),
        out_shape=jax.ShapeDtypeStruct((n, D_MODEL), F32),
        compiler_params=pltpu.CompilerParams(dimension_semantics=("arbitrary",),
                                             vmem_limit_bytes=VMEM_LIMIT),
        name="back_stage",
    )(h, o_att, o_ret, p_emb, *weights)


def _rope_tables(pos):
    pos = np.asarray(pos, np.float64)[:, None]
    n = pos.shape[0]
    inv_m = ROPE_THETA ** (-np.arange(ROPE_HALF, dtype=np.float64) / ROPE_HALF)
    cos_m, sin_m = np.cos(pos * inv_m[None, :]), np.sin(pos * inv_m[None, :])
    z = lambda width: np.zeros((n, width))
    tail = HEAD_PAD - ROPE_OFF - MLA_ROPE
    cm = np.concatenate([np.ones((n, ROPE_OFF)), cos_m, cos_m, z(tail)], axis=1)
    s1 = np.concatenate([z(ROPE_OFF), -sin_m, z(ROPE_HALF), z(tail)], axis=1)
    s2 = np.concatenate([z(ROPE_OFF), z(ROPE_HALF), sin_m, z(tail)], axis=1)
    half = RET_DK // 2
    inv_r = ROPE_THETA ** (-np.arange(half, dtype=np.float64) / half)
    cos_r, sin_r = np.cos(pos * inv_r[None, :]), np.sin(pos * inv_r[None, :])
    cr = np.concatenate([cos_r, cos_r], axis=1)
    sr = np.concatenate([-sin_r, sin_r], axis=1)
    tables = (cm, s1, s2, cr, sr, cos_m.T, sin_m.T)
    return tuple(np.ascontiguousarray(t, dtype=np.float32) for t in tables)


def _layer_weights(i, ffn1_norm, ffn1_w_gate, ffn1_w_up, ffn1_w_down, mix_norm, w_in, q_a_norm, w_q_b,
                   kv_a_norm, w_kv_b, ret_norm, w_branch_att, w_branch_ret, w_out,
                   ffn2_norm, ffn2_w_gate, ffn2_w_up, ffn2_w_down, ple_norm, w_ple_gate, w_ple_proj,
                   final_norm):
    vec = lambda a: a.reshape(1, -1)
    bf = lambda a: a.astype(BF16)
    win = w_in[i]
    zcols = lambda width: jnp.zeros((D_MODEL, width), F32)
    w_in_all = jnp.concatenate(
        [win[:, _OFF_RG:], zcols(_FRONT_BLOCK * _FRONT_W - _GATE_W),
         win[:, :_OFF_KPE], zcols(ROPE_OFF), win[:, _OFF_KPE:_OFF_RQ],
         zcols(HEAD_PAD - ROPE_OFF - MLA_ROPE), win[:, _OFF_RQ:_OFF_RG]], axis=1)
    wqb = w_q_b[i].reshape(Q_LORA, MLA_HEADS, MLA_NOPE + MLA_ROPE)
    wqb = jnp.pad(wqb, ((0, 0), (0, 0), (0, HEAD_PAD - MLA_NOPE - MLA_ROPE)))
    wkv = w_kv_b[i].reshape(KV_LORA, MLA_HEADS, MLA_NOPE + MLA_V)
    w_uk, w_uv = wkv[..., :MLA_NOPE], wkv[..., MLA_NOPE:]
    w_k = jnp.pad(w_uk, ((0, 0), (0, 0), (0, HEAD_PAD - MLA_NOPE)))
    pass_rope = np.zeros((1, KEY_W, HEAD_PAD), np.float32)
    pass_rope[0, KV_LORA:KV_LORA + MLA_ROPE, MLA_NOPE:MLA_NOPE + MLA_ROPE] = np.eye(MLA_ROPE)
    w_absorb_t = jnp.pad(jnp.transpose(w_uk, (1, 0, 2)),
                         ((0, 0), (0, KEY_W - KV_LORA), (0, HEAD_PAD - MLA_NOPE))) + pass_rope
    own_cols = np.eye(MLA_HEADS, dtype=np.float32)[:, None, :, None]
    w_unabsorb = (own_cols * jnp.transpose(w_uv, (1, 0, 2))[:, :, None, :]).reshape(
        MLA_HEADS, KV_LORA, _ATT_W)
    return {
        'ffn1_norm': vec(ffn1_norm[i]), 'ffn1_w_gate': bf(ffn1_w_gate[i]), 'ffn1_w_up': bf(ffn1_w_up[i]),
        'ffn1_w_down': bf(ffn1_w_down[i]), 'mix_norm': vec(mix_norm[i]),
        'w_in_all': bf(w_in_all),
        'q_a_norm': vec(q_a_norm[i]),
        'w_q_bt': bf(jnp.transpose(wqb.reshape(Q_LORA, MLA_HEADS * HEAD_PAD))),
        'kv_a_norm': vec(kv_a_norm[i]), 'w_k': bf(w_k.reshape(KV_LORA, MLA_HEADS * HEAD_PAD)),
        'w_vt': bf(jnp.pad(jnp.transpose(w_uv, (1, 2, 0)), ((0, 0), (0, V_ROWS - MLA_V), (0, 0)))
                   .reshape(_VT_ROWS, KV_LORA)),
        'w_absorb_t': bf(w_absorb_t), 'w_unabsorb': bf(w_unabsorb),
        'ret_norm': ret_norm[i],
        'w_branch_att': bf(w_branch_att[i]), 'w_branch_ret': bf(w_branch_ret[i]), 'w_out': bf(w_out[i]),
        'ffn2_norm': vec(ffn2_norm[i]), 'ffn2_w_gate': bf(ffn2_w_gate[i]), 'ffn2_w_up': bf(ffn2_w_up[i]),
        'ffn2_w_down': bf(ffn2_w_down[i]), 'ple_norm': vec(ple_norm[i]),
        'w_ple_gate': bf(w_ple_gate[i]), 'w_ple_proj': bf(w_ple_proj[i]),
        'final_norm': vec(final_norm),
    }


def kernel(x_prompt, x_sample, cache_ckv, cache_kpe, state_ret, page_table, p_prompt, p_sample, ffn1_norm, ffn1_w_gate, ffn1_w_up, ffn1_w_down, mix_norm, w_in, q_a_norm, w_q_b, kv_a_norm, w_kv_b, ret_norm, w_branch_att, w_branch_ret, w_out, ffn2_norm, ffn2_w_gate, ffn2_w_up, ffn2_w_down, ple_norm, w_ple_gate, w_ple_proj, final_norm):
    batch, seq, _ = x_prompt.shape
    n_dec, dec_seq, _ = x_sample.shape
    depth = w_in.shape[0]
    assert dec_seq == 1 and depth == 1
    n_past = page_table.shape[1] * PAGE_SIZE

    w = _layer_weights(0, ffn1_norm, ffn1_w_gate, ffn1_w_up, ffn1_w_down, mix_norm, w_in, q_a_norm,
                       w_q_b, kv_a_norm, w_kv_b, ret_norm, w_branch_att, w_branch_ret, w_out,
                       ffn2_norm, ffn2_w_gate, ffn2_w_up, ffn2_w_down, ple_norm, w_ple_gate,
                       w_ple_proj, final_norm)

    tabs_p = _rope_tables(np.arange(seq))
    (h_p, qt_p, k_p, vt_p, ckv_p, kpet_p, qr_p, kr_p, rv_p) = _front_stage(
        x_prompt.reshape(batch * seq, D_MODEL), tabs_p, w, TOKEN_TILE, seq // TOKEN_TILE)
    bt = lambda a: a.reshape(batch, seq, a.shape[-1])
    o_att_p = _prompt_attention(qt_p, k_p, vt_p, batch, seq)
    o_ret_p, ret_p = _prompt_retention(bt(qr_p), bt(kr_p), bt(rv_p), w['ret_norm'])
    y_p = _back_stage(h_p, o_att_p, o_ret_p.reshape(batch * seq, -1),
                      p_prompt.reshape(batch * seq, PLE_DIM), w, TOKEN_TILE)

    tabs_1 = _rope_tables(np.full((1,), n_past))
    tabs_s = (tuple(np.ascontiguousarray(np.broadcast_to(t, (n_dec, LANES))) for t in tabs_1[:5])
              + tuple(np.ascontiguousarray(np.broadcast_to(t, (ROPE_HALF, n_dec))) for t in tabs_1[5:]))
    (h_s, qt_s, _, _, ckv_s, kpet_s, qr_s, kr_s, rv_s) = _front_stage(
        x_sample.reshape(n_dec, D_MODEL), tabs_s, w, n_dec, 1)
    q_abs = jnp.transpose(_absorb_q(qt_s, w['w_absorb_t']), (2, 0, 1))
    kpe_s = jnp.transpose(kpet_s[0])
    k_new = jnp.concatenate([ckv_s, kpe_s,
                             jnp.zeros((n_dec, KEY_W - KV_LORA - MLA_ROPE), F32)], axis=1)
    o_lat = _paged_attention(page_table, q_abs, k_new.reshape(n_dec, 1, KEY_W), cache_ckv,
                             jnp.swapaxes(cache_kpe, 2, 3))
    o_att_s = _unabsorb(jnp.transpose(o_lat, (1, 0, 2)), w['w_unabsorb'])
    o_ret_s, ret_s = _sample_retention(qr_s, kr_s, rv_s, state_ret, w['ret_norm'])
    y_s = _back_stage(h_s, o_att_s, o_ret_s, p_sample.reshape(n_dec, PLE_DIM), w, n_dec)

    return (y_p.reshape(batch, seq, D_MODEL),
            y_s.reshape(n_dec, 1, D_MODEL),
            ckv_p.reshape(1, batch, seq, KV_LORA),
            jnp.transpose(kpet_p, (0, 2, 1))[None],
            ret_p[None],
            ckv_s.reshape(1, n_dec, 1, KV_LORA),
            kpe_s.reshape(1, n_dec, 1, MLA_ROPE),
            ret_s)
```

```python
import functools

import jax
import jax.numpy as jnp
import numpy as np
from jax import lax
from jax.experimental import pallas as pl
from jax.experimental.pallas import tpu as pltpu

F32 = jnp.float32
BF16 = jnp.bfloat16

D_MODEL = 1024
D_FF = 2816
PLE_DIM = 256
MLA_HEADS = 8
MLA_NOPE = 64
MLA_ROPE = 32
MLA_V = 64
Q_LORA = 256
KV_LORA = 128
RET_HEADS = 4
RET_DK = 128
RET_DV = 128
PAGE_SIZE = 128
ROPE_THETA = 10000.0
EPS = 1e-6

LANES = 128
HEAD_PAD = LANES
ROPE_OFF = MLA_NOPE
ROPE_HALF = MLA_ROPE // 2
SOFTMAX_SCALE = (MLA_NOPE + MLA_ROPE) ** -0.5
LOG2E = 1.4426950408889634
Q_SCALE = SOFTMAX_SCALE * LOG2E
NEG_BIG = -1e30

TOKEN_TILE = 512
ATTN_TILE = 512
RET_CHUNK = 512
PAGES_PER_STEP = 32
PAGE_GROUPS = 4
PAGE_SLOTS = 4
RET_SEQS_PER_STEP = 8
KEY_W = 2 * LANES
VMEM_LIMIT = 60 * 1024 * 1024

_OFF_CQ = 0
_OFF_CKV = _OFF_CQ + Q_LORA
_OFF_KPE = _OFF_CKV + KV_LORA
_OFF_RQ = _OFF_KPE + MLA_ROPE
_OFF_RK = _OFF_RQ + RET_HEADS * RET_DK
_OFF_RV = _OFF_RK + RET_HEADS * RET_DK
_OFF_RG = _OFF_RV + RET_HEADS * RET_DV
_RET_W = RET_HEADS * RET_DK
_ATT_W = MLA_HEADS * MLA_V
BF16_TILE_ROWS = 16
V_ROWS = MLA_V + BF16_TILE_ROWS
_VT_ROWS = MLA_HEADS * V_ROWS


def _rms(x, g):
    return x * lax.rsqrt(jnp.mean(x * x, axis=-1, keepdims=True) + EPS) * g


def _dot(a, b):
    return jnp.dot(a, b, preferred_element_type=F32)


def _dot_nt(a, b):
    return lax.dot_general(a, b, (((1,), (1,)), ((), ())), preferred_element_type=F32)


def _swiglu_half(x, g, wg, wu, wd):
    xn = _rms(x, g).astype(BF16)
    gate = _dot(xn, wg)
    up = _dot(xn, wu)
    act = (gate * jax.nn.sigmoid(gate) * up).astype(BF16)
    return x + 0.5 * _dot(act, wd)


def _const_spec(shape, index=None):
    index = (0,) * len(shape) if index is None else index
    return pl.BlockSpec(shape, lambda *_: index, pipeline_mode=pl.Buffered(1))


def _front_kernel(x_ref, cm_ref, s1_ref, s2_ref, cr_ref, sr_ref, ct_ref, st_ref,
                  g1_ref, wg_ref, wu_ref, wd_ref, gmix_ref, win_ref,
                  gqa_ref, wqbt_ref, gkv_ref, wk_ref, wvt_ref,
                  h_ref, qt_ref, k_ref, vt_ref, ckv_ref, kpet_ref, qr_ref, kr_ref, rv_ref):
    x = x_ref[...]
    h = _swiglu_half(x, g1_ref[...], wg_ref[...], wu_ref[...], wd_ref[...])
    h_ref[...] = h
    un = _rms(h, gmix_ref[...]).astype(BF16)
    z = _dot_nt(un, win_ref[...])

    cm, s1, s2 = cm_ref[...], s1_ref[...], s2_ref[...]

    def mla_rope(t):
        return (t * cm + pltpu.roll(t, LANES - ROPE_HALF, 1) * s1
                + pltpu.roll(t, ROPE_HALF, 1) * s2)

    cq = z[:, _OFF_CQ:_OFF_CQ + Q_LORA]
    qt = _dot_nt(wqbt_ref[...], _rms(cq, gqa_ref[...]).astype(BF16))
    ct, st = ct_ref[...], st_ref[...]
    c_kv = _rms(z[:, Q_LORA:Q_LORA + KV_LORA], gkv_ref[...])
    ckv_ref[...] = c_kv
    c_kv_b = c_kv.astype(BF16)
    k_nope = _dot(c_kv_b, wk_ref[...])
    vt = _dot_nt(wvt_ref[...], c_kv_b)
    sum_row = lax.broadcasted_iota(jnp.int32, vt.shape, 0) % V_ROWS >= MLA_V
    vt_ref[...] = jnp.where(sum_row, 1.0, vt).astype(BF16)
    k_pe = mla_rope(z[:, Q_LORA + KV_LORA:Q_LORA + KV_LORA + LANES])
    kpet_ref[0] = jnp.transpose(k_pe)[ROPE_OFF:ROPE_OFF + MLA_ROPE, :]
    for hd in range(MLA_HEADS):
        sl = slice(hd * HEAD_PAD, (hd + 1) * HEAD_PAD)
        k_ref[:, sl] = (k_nope[:, sl] + k_pe).astype(BF16)
        r0 = hd * HEAD_PAD + ROPE_OFF
        x1 = qt[r0:r0 + ROPE_HALF]
        x2 = qt[r0 + ROPE_HALF:r0 + MLA_ROPE]
        qt_ref[hd * HEAD_PAD:r0, :] = (qt[hd * HEAD_PAD:r0] * Q_SCALE).astype(BF16)
        qt_ref[r0:r0 + ROPE_HALF, :] = ((x1 * ct - x2 * st) * Q_SCALE).astype(BF16)
        qt_ref[r0 + ROPE_HALF:r0 + MLA_ROPE, :] = ((x1 * st + x2 * ct) * Q_SCALE).astype(BF16)
        qt_ref[r0 + MLA_ROPE:(hd + 1) * HEAD_PAD, :] = jnp.zeros(
            (HEAD_PAD - ROPE_OFF - MLA_ROPE, qt.shape[1]), BF16)

    cr, sr = cr_ref[...], sr_ref[...]
    base = Q_LORA + KV_LORA + LANES
    for hd in range(RET_HEADS):
        sl = slice(hd * RET_DK, (hd + 1) * RET_DK)
        rq = z[:, base + hd * RET_DK:base + (hd + 1) * RET_DK]
        rk = z[:, base + _RET_W + hd * RET_DK:base + _RET_W + (hd + 1) * RET_DK]
        qr_ref[:, sl] = rq * cr + pltpu.roll(rq, RET_DK // 2, 1) * sr
        kr_ref[:, sl] = (rk * cr + pltpu.roll(rk, RET_DK // 2, 1) * sr) * (RET_DK ** -0.5)
    rv_ref[...] = z[:, base + 2 * _RET_W:base + 3 * _RET_W]


def _front_stage(x, tables, w, tile, table_tiles):
    n = x.shape[0]
    seq_rows = tile * table_tiles
    row = lambda width: pl.BlockSpec((tile, width), lambda i: (i, 0))
    col = lambda height: pl.BlockSpec((height, tile), lambda i: (0, i))
    tab = pl.BlockSpec((tile, LANES), lambda i: (i % table_tiles, 0))
    tab_t = pl.BlockSpec((ROPE_HALF, tile), lambda i: (0, i % table_tiles))
    weights = [w['ffn1_norm'], w['ffn1_w_gate'], w['ffn1_w_up'], w['ffn1_w_down'], w['mix_norm'],
               w['w_in_front_t'], w['q_a_norm'], w['w_q_bt'], w['kv_a_norm'], w['w_k'], w['w_vt']]
    outs = [((n, D_MODEL), F32, row(D_MODEL)),
            ((MLA_HEADS * HEAD_PAD, n), BF16, col(MLA_HEADS * HEAD_PAD)),
            ((n, MLA_HEADS * HEAD_PAD), BF16, row(MLA_HEADS * HEAD_PAD)),
            ((_VT_ROWS, n), BF16, col(_VT_ROWS)),
            ((n, KV_LORA), F32, row(KV_LORA)),
            ((n // seq_rows, MLA_ROPE, seq_rows), F32,
             pl.BlockSpec((1, MLA_ROPE, tile), lambda i: (i // table_tiles, 0, i % table_tiles))),
            ((n, _RET_W), F32, row(_RET_W)),
            ((n, _RET_W), F32, row(_RET_W)),
            ((n, _RET_W), F32, row(_RET_W))]
    return pl.pallas_call(
        _front_kernel,
        grid=(n // tile,),
        in_specs=[row(D_MODEL)] + [tab] * 5 + [tab_t] * 2 + [_const_spec(a.shape) for a in weights],
        out_specs=[spec for _, _, spec in outs],
        out_shape=[jax.ShapeDtypeStruct(shape, dt) for shape, dt, _ in outs],
        compiler_params=pltpu.CompilerParams(dimension_semantics=("arbitrary",),
                                             vmem_limit_bytes=VMEM_LIMIT),
        name="front_stage",
    )(x, *tables, *weights)


def _attn_kernel(batch, qi_ref, ki_ref, *refs):
    qt_refs, k_refs, vt_refs = refs[:batch], refs[batch:2 * batch], refs[2 * batch:3 * batch]
    o_ref, m_sc, acc_sc = refs[3 * batch:]
    t = pl.program_id(0)
    qi = qi_ref[t]
    ki = ki_ref[t]
    units = [(b, hd) for b in range(batch) for hd in range(MLA_HEADS)]

    @pl.when(ki == 0)
    def _():
        m_sc[...] = jnp.full(m_sc.shape, NEG_BIG, F32)
        acc_sc[...] = jnp.zeros(acc_sc.shape, F32)

    def scores(unit):
        b, hd = unit
        kh = k_refs[b][:, hd * HEAD_PAD:(hd + 1) * HEAD_PAD]
        return _dot(kh, qt_refs[b][hd * HEAD_PAD:(hd + 1) * HEAD_PAD, :])

    def value_update(unit, alpha, p):
        b, hd = unit
        i = b * MLA_HEADS + hd
        acc_sc[i] = alpha * acc_sc[i] + _dot(vt_refs[b][hd * V_ROWS:(hd + 1) * V_ROWS, :], p)

    def update(masked):
        st_next = scores(units[0])
        pending = None
        for n, unit in enumerate(units):
            i = unit[0] * MLA_HEADS + unit[1]
            st = st_next
            if n + 1 < len(units):
                st_next = scores(units[n + 1])
            if masked:
                key = lax.broadcasted_iota(jnp.int32, st.shape, 0)
                qry = lax.broadcasted_iota(jnp.int32, st.shape, 1)
                st = jnp.where(key <= qry, st, NEG_BIG)
            m_prev = m_sc[i]
            m_new = jnp.maximum(m_prev, jnp.max(st, axis=0, keepdims=True))
            alpha = jnp.exp2(m_prev - m_new)
            p = jnp.exp2(st - m_new).astype(BF16)
            m_sc[i] = m_new
            if pending is not None:
                value_update(*pending)
            pending = (unit, alpha, p)
        value_update(*pending)

    @pl.when(ki < qi)
    def _():
        update(False)

    @pl.when(ki == qi)
    def _():
        update(True)
        acc = acc_sc[...]
        o_t = (acc[:, :MLA_V, :] / acc[:, MLA_V:MLA_V + 1, :]).reshape(batch, _ATT_W, ATTN_TILE)
        for b in range(batch):
            o_ref[b] = jnp.transpose(o_t[b]).astype(o_ref.dtype)


def _prompt_attention(qt, k, vt, batch, seq):
    nt = seq // ATTN_TILE
    qi_list = np.array([i for i in range(nt) for _ in range(i + 1)], np.int32)
    ki_list = np.array([j for i in range(nt) for j in range(i + 1)], np.int32)
    qk_w = MLA_HEADS * HEAD_PAD
    seqs = range(batch)
    q_specs = [pl.BlockSpec((qk_w, ATTN_TILE), lambda s, qi, ki, b=b: (0, b * nt + qi[s])) for b in seqs]
    k_specs = [pl.BlockSpec((ATTN_TILE, qk_w), lambda s, qi, ki, b=b: (b * nt + ki[s], 0)) for b in seqs]
    v_specs = [pl.BlockSpec((_VT_ROWS, ATTN_TILE), lambda s, qi, ki, b=b: (0, b * nt + ki[s]))
               for b in seqs]
    grid_spec = pltpu.PrefetchScalarGridSpec(
        num_scalar_prefetch=2,
        grid=(len(qi_list),),
        in_specs=q_specs + k_specs + v_specs,
        out_specs=pl.BlockSpec((batch, ATTN_TILE, _ATT_W), lambda s, qi, ki: (0, qi[s], 0)),
        scratch_shapes=[pltpu.VMEM((batch * MLA_HEADS, 1, ATTN_TILE), F32),
                        pltpu.VMEM((batch * MLA_HEADS, V_ROWS, ATTN_TILE), F32)],
    )
    out = pl.pallas_call(
        functools.partial(_attn_kernel, batch),
        grid_spec=grid_spec,
        out_shape=jax.ShapeDtypeStruct((batch, seq, _ATT_W), BF16),
        compiler_params=pltpu.CompilerParams(dimension_semantics=("arbitrary",),
                                             vmem_limit_bytes=VMEM_LIMIT),
        name="prompt_attention",
    )(jnp.asarray(qi_list), jnp.asarray(ki_list), *([qt] * batch), *([k] * batch), *([vt] * batch))
    return out.reshape(batch * seq, _ATT_W)


def _head_layernorm(o, g):
    mu = jnp.mean(o, axis=-1, keepdims=True)
    d = o - mu
    var = jnp.mean(d * d, axis=-1, keepdims=True)
    return d * lax.rsqrt(var + EPS) * g


def _ret_prompt_kernel(q_ref, k_ref, v_ref, dec_ref, qd_ref, kd_ref, gc_ref, g_ref,
                       o_ref, s_out_ref, s_sc):
    c = pl.program_id(1)

    @pl.when(c == 0)
    def _():
        s_sc[...] = jnp.zeros(s_sc.shape, F32)

    for hd in range(RET_HEADS):
        sl = slice(hd * RET_DK, (hd + 1) * RET_DK)
        q = q_ref[0, :, sl]
        k = k_ref[0, :, sl]
        vb = v_ref[0, :, sl].astype(BF16)
        qb = q.astype(BF16)
        state = s_sc[hd]
        scores = _dot_nt(qb, k.astype(BF16)) * dec_ref[hd]
        inner = _dot(scores.astype(BF16), vb)
        cross = _dot(qb, state.astype(BF16)) * qd_ref[hd]
        o_ref[0, :, sl] = _head_layernorm(inner + cross, g_ref[hd:hd + 1, :])
        k_dec_t = jnp.transpose(k * kd_ref[hd]).astype(BF16)
        s_sc[hd] = gc_ref[hd:hd + 1, :] * state + _dot(k_dec_t, vb)

    @pl.when(c == pl.num_programs(1) - 1)
    def _():
        s_out_ref[0] = s_sc[...]


def _log_gamma():
    return np.log1p(-np.exp2(-5.0 - np.arange(RET_HEADS, dtype=np.float64)))


def _retention_consts(chunk):
    log_gamma = _log_gamma()
    idx = np.arange(chunk, dtype=np.float64)
    diff = idx[:, None] - idx[None, :]
    decay = np.where(diff[None] >= 0,
                     np.exp(log_gamma[:, None, None] * np.maximum(diff, 0.0)[None]), 0.0)
    q_decay = np.exp(log_gamma[:, None] * (idx + 1.0)[None, :])
    k_decay = np.exp(log_gamma[:, None] * (chunk - 1.0 - idx)[None, :])
    chunk_decay = np.exp(log_gamma * chunk)
    lane = lambda a: np.broadcast_to(a[..., None], a.shape + (LANES,))
    tables = (decay, lane(q_decay), lane(k_decay), lane(chunk_decay))
    return tuple(np.ascontiguousarray(t, dtype=np.float32) for t in tables)


def _prompt_retention(qr, kr, rv, ret_norm):
    b, t, _ = qr.shape
    decay, q_decay, k_decay, chunk_decay = _retention_consts(RET_CHUNK)
    seq = pl.BlockSpec((1, RET_CHUNK, _RET_W), lambda bb, c: (bb, c, 0))
    return pl.pallas_call(
        _ret_prompt_kernel,
        grid=(b, t // RET_CHUNK),
        in_specs=[seq, seq, seq, _const_spec(decay.shape), _const_spec(q_decay.shape),
                  _const_spec(k_decay.shape), _const_spec(chunk_decay.shape),
                  _const_spec(ret_norm.shape)],
        out_specs=[seq, pl.BlockSpec((1, RET_HEADS, RET_DK, RET_DV), lambda bb, c: (bb, 0, 0, 0))],
        out_shape=[jax.ShapeDtypeStruct((b, t, _RET_W), F32),
                   jax.ShapeDtypeStruct((b, RET_HEADS, RET_DK, RET_DV), F32)],
        scratch_shapes=[pltpu.VMEM((RET_HEADS, RET_DK, RET_DV), F32)],
        compiler_params=pltpu.CompilerParams(dimension_semantics=("arbitrary", "arbitrary")),
        name="prompt_retention",
    )(qr, kr, rv, decay, q_decay, k_decay, chunk_decay, ret_norm)


def _ret_sample_kernel(q_ref, k_ref, v_ref, s_ref, gam_ref, g_ref, o_ref, s_out_ref):
    rows = lax.broadcasted_iota(jnp.int32, (RET_DK, RET_DK), 0)
    cols = lax.broadcasted_iota(jnp.int32, (RET_DK, RET_DK), 1)
    eye = rows == cols

    def column(r):
        return jnp.sum(jnp.where(eye, jnp.broadcast_to(r, (RET_DK, RET_DK)), 0.0),
                       axis=1, keepdims=True)

    heads = range(RET_HEADS)
    lanes = [slice(hd * RET_DK, (hd + 1) * RET_DK) for hd in heads]
    gams = [gam_ref[hd:hd + 1, :] for hd in heads]
    for i in range(RET_SEQS_PER_STEP):
        qs = [q_ref[i, :, sl] for sl in lanes]
        ks = [k_ref[i, :, sl] for sl in lanes]
        vs = [v_ref[i, :, sl] for sl in lanes]
        q_cols = [column(q) for q in qs]
        k_cols = [column(k) for k in ks]
        qk = [jnp.sum(q * k, axis=-1, keepdims=True) for q, k in zip(qs, ks)]
        states = [s_ref[0, i, hd] for hd in heads]
        outs = [qk[hd] * vs[hd] + jnp.sum(q_cols[hd] * states[hd], axis=0, keepdims=True) * gams[hd]
                for hd in heads]
        for hd in heads:
            s_out_ref[0, i, hd] = gams[hd] * states[hd] + k_cols[hd] * vs[hd]
        mus = [jnp.mean(o, axis=-1, keepdims=True) for o in outs]
        devs = [o - mu for o, mu in zip(outs, mus)]
        variances = [jnp.mean(d * d, axis=-1, keepdims=True) for d in devs]
        for hd in heads:
            o_ref[i, :, lanes[hd]] = devs[hd] * lax.rsqrt(variances[hd] + EPS) * g_ref[hd:hd + 1, :]


def _sample_retention(qr, kr, rv, state, ret_norm):
    n = qr.shape[0]
    gam = np.ascontiguousarray(
        np.broadcast_to(np.exp(_log_gamma() * 1.0)[:, None], (RET_HEADS, LANES)), dtype=np.float32)
    g = RET_SEQS_PER_STEP
    tok = pl.BlockSpec((g, 1, _RET_W), lambda i: (i, 0, 0))
    st = pl.BlockSpec((1, g, RET_HEADS, RET_DK, RET_DV), lambda i: (0, i, 0, 0, 0))
    r3 = lambda a: a.reshape(n, 1, _RET_W)
    o, s_new = pl.pallas_call(
        _ret_sample_kernel,
        grid=(n // g,),
        in_specs=[tok, tok, tok, st, _const_spec(gam.shape), _const_spec(ret_norm.shape)],
        out_specs=[tok, st],
        out_shape=[jax.ShapeDtypeStruct((n, 1, _RET_W), F32),
                   jax.ShapeDtypeStruct(state.shape, F32)],
        compiler_params=pltpu.CompilerParams(dimension_semantics=("arbitrary",)),
        name="sample_retention",
    )(r3(qr), r3(kr), r3(rv), state, gam, ret_norm)
    return o.reshape(n, _RET_W), s_new


def _absorb_q_kernel(qt_ref, wt_ref, o_ref):
    for hd in range(MLA_HEADS):
        o_ref[hd] = _dot(wt_ref[hd], qt_ref[hd * HEAD_PAD:(hd + 1) * HEAD_PAD, :])


def _absorb_q(qt, w_absorb_t):
    n = qt.shape[1]
    return pl.pallas_call(
        _absorb_q_kernel,
        out_shape=jax.ShapeDtypeStruct((MLA_HEADS, KEY_W, n), F32),
        name="sample_absorb_q",
    )(qt, w_absorb_t)


def _unabsorb_kernel(o_ref, w_ref, out_ref):
    acc = _dot(o_ref[0].astype(BF16), w_ref[0])
    for hd in range(1, MLA_HEADS):
        acc = acc + _dot(o_ref[hd].astype(BF16), w_ref[hd])
    out_ref[...] = acc.astype(out_ref.dtype)


def _unabsorb(o_lat, w_unabsorb):
    n = o_lat.shape[1]
    return pl.pallas_call(
        _unabsorb_kernel,
        out_shape=jax.ShapeDtypeStruct((n, _ATT_W), BF16),
        name="sample_unabsorb",
    )(o_lat, w_unabsorb)


def _paged_kernel(pt_ref, q_ref, knew_ref, ckv_hbm, kpet_hbm, o_ref, buf, kpe_buf, sem):
    b = pl.program_id(0)
    n_seq = pl.num_programs(0)
    n_chunks = pt_ref.shape[1] // PAGES_PER_STEP
    assert n_chunks == PAGE_SLOTS
    ahead = PAGE_SLOTS - 1
    chunk_keys = PAGES_PER_STEP * PAGE_SIZE
    group_keys = chunk_keys // PAGE_GROUPS

    def page_copies(bb, cc, sl):
        out = []
        for j in range(PAGES_PER_STEP):
            pg = pt_ref[bb, cc * PAGES_PER_STEP + j]
            keys = pl.ds(j * PAGE_SIZE, PAGE_SIZE)
            out.append(pltpu.make_async_copy(ckv_hbm.at[0, pg], buf.at[sl, keys], sem.at[sl, 0]))
            out.append(pltpu.make_async_copy(kpet_hbm.at[0, pg], kpe_buf.at[sl, :, keys],
                                             sem.at[sl, 1]))
        return out

    @pl.when(b == 0)
    def _():
        for c in range(ahead):
            for cp in page_copies(0, c, c):
                cp.start()

    b_next = jnp.minimum(b + 1, n_seq - 1)

    q = q_ref[0]
    q_lat = q[:, 0:KV_LORA]
    q_pe = q[:, KV_LORA:KV_LORA + MLA_ROPE]

    def scores(sl, g):
        keys = slice(g * group_keys, (g + 1) * group_keys)
        return _dot_nt(q_lat, buf[sl, keys, :]) + _dot(q_pe, kpe_buf[sl, :, keys])

    m_run = jnp.full((MLA_HEADS, 1), NEG_BIG, F32)
    l_run = jnp.zeros((MLA_HEADS, 1), F32)
    acc_run = jnp.zeros((MLA_HEADS, KV_LORA), F32)
    def wait_chunk(c):
        pltpu.make_async_copy(buf.at[c], buf.at[c], sem.at[c, 0]).wait()
        pltpu.make_async_copy(kpe_buf.at[c], kpe_buf.at[c], sem.at[c, 1]).wait()

    wait_chunk(0)
    s_next = scores(0, 0)
    for c in range(n_chunks):
        sl = c
        nxt = c + ahead
        prefetch = (page_copies(b, nxt, nxt) if nxt < n_chunks
                    else page_copies(b_next, nxt - n_chunks, nxt - n_chunks))
        parts = []
        per_group = len(prefetch) // PAGE_GROUPS
        for g in range(PAGE_GROUPS):
            s = s_next
            if g + 1 < PAGE_GROUPS:
                s_next = scores(sl, g + 1)
            elif c + 1 < n_chunks:
                wait_chunk(c + 1)
                s_next = scores(c + 1, 0)
            for cp in prefetch[g * per_group:(g + 1) * per_group]:
                cp.start()
            m_g = jnp.max(s, axis=-1, keepdims=True)
            p = jnp.exp2(s - m_g)
            l_g = jnp.sum(p, axis=-1, keepdims=True)
            keys = slice(g * group_keys, (g + 1) * group_keys)
            parts.append((m_g, l_g, _dot(p, buf[sl, keys, :])))

        m_new = m_run
        for m_g, _, _ in parts:
            m_new = jnp.maximum(m_new, m_g)
        a_run = jnp.exp2(m_run - m_new)
        l_run = a_run * l_run
        acc_run = a_run * acc_run
        for m_g, l_g, o_g in parts:
            a_g = jnp.exp2(m_g - m_new)
            l_run = l_run + a_g * l_g
            acc_run = acc_run + a_g * o_g
        m_run = m_new

    kn = knew_ref[0]
    s_self = jnp.sum(q * kn, axis=-1, keepdims=True)
    m_fin = jnp.maximum(m_run, s_self)
    a = jnp.exp2(m_run - m_fin)
    p_self = jnp.exp2(s_self - m_fin)
    o_ref[0] = (a * acc_run + p_self * kn[:, 0:KV_LORA]) / (a * l_run + p_self)

    @pl.when(b == n_seq - 1)
    def _():
        for c in range(ahead):
            wait_chunk(c)


def _paged_attention(page_table, q_abs, k_new, cache_ckv, cache_kpe_t):
    n, n_pages = page_table.shape
    chunk_keys = PAGES_PER_STEP * PAGE_SIZE
    grid_spec = pltpu.PrefetchScalarGridSpec(
        num_scalar_prefetch=1,
        grid=(n,),
        in_specs=[
            pl.BlockSpec((1, MLA_HEADS, KEY_W), lambda b, pt: (b, 0, 0)),
            pl.BlockSpec((1, 1, KEY_W), lambda b, pt: (b, 0, 0)),
            pl.BlockSpec(memory_space=pl.ANY),
            pl.BlockSpec(memory_space=pl.ANY),
        ],
        out_specs=pl.BlockSpec((1, MLA_HEADS, KV_LORA), lambda b, pt: (b, 0, 0)),
        scratch_shapes=[pltpu.VMEM((PAGE_SLOTS, chunk_keys, KV_LORA), F32),
                        pltpu.VMEM((PAGE_SLOTS, MLA_ROPE, chunk_keys), F32),
                        pltpu.SemaphoreType.DMA((PAGE_SLOTS, 2))],
    )
    return pl.pallas_call(
        _paged_kernel,
        grid_spec=grid_spec,
        out_shape=jax.ShapeDtypeStruct((n, MLA_HEADS, KV_LORA), F32),
        compiler_params=pltpu.CompilerParams(dimension_semantics=("arbitrary",),
                                             vmem_limit_bytes=VMEM_LIMIT),
        name="sample_paged_attention",
    )(page_table, q_abs, k_new, cache_ckv, cache_kpe_t)


def _back_kernel(h_ref, oatt_ref, oret_ref, p_ref,
                 gmix_ref, wgate_ref, wba_ref, wbr_ref, wout_ref,
                 g2_ref, wg_ref, wu_ref, wd_ref, gple_ref, wpg_ref, wpp_ref, gfin_ref, y_ref):
    h = h_ref[...]
    un = _rms(h, gmix_ref[...]).astype(BF16)
    gates = _dot_nt(un, wgate_ref[...])
    rg = gates[:, :_RET_W]
    ga = gates[:, _RET_W:_RET_W + D_MODEL]
    gr = gates[:, _RET_W + D_MODEL:]
    o_ret = (rg * jax.nn.sigmoid(rg) * oret_ref[...]).astype(BF16)
    merged = (jax.nn.sigmoid(ga) * _dot(oatt_ref[...], wba_ref[...])
              + jax.nn.sigmoid(gr) * _dot(o_ret, wbr_ref[...]))
    h = h + _dot(merged.astype(BF16), wout_ref[...])
    h = _swiglu_half(h, g2_ref[...], wg_ref[...], wu_ref[...], wd_ref[...])
    gate = jax.nn.sigmoid(_dot(_rms(h, gple_ref[...]).astype(BF16), wpg_ref[...]))
    h = h + gate * _dot(p_ref[...].astype(BF16), wpp_ref[...])
    y_ref[...] = _rms(h, gfin_ref[...])


def _back_stage(h, o_att, o_ret, p_emb, w, tile):
    n = h.shape[0]
    row = lambda width: pl.BlockSpec((tile, width), lambda i: (i, 0))
    weights = [w['mix_norm'], w['w_in_gates_t'], w['w_branch_att'], w['w_branch_ret'], w['w_out'],
               w['ffn2_norm'], w['ffn2_w_gate'], w['ffn2_w_up'], w['ffn2_w_down'],
               w['ple_norm'], w['w_ple_gate'], w['w_ple_proj'], w['final_norm']]
    return pl.pallas_call(
        _back_kernel,
        grid=(n // tile,),
        in_specs=[row(D_MODEL), row(_ATT_W), row(_RET_W), row(PLE_DIM)]
        + [_const_spec(a.shape) for a in weights],
        out_specs=row(D_MODEL),
        out_shape=jax.ShapeDtypeStruct((n, D_MODEL), F32),
        compiler_params=pltpu.CompilerParams(dimension_semantics=("arbitrary",),
                                             vmem_limit_bytes=VMEM_LIMIT),
        name="back_stage",
    )(h, o_att, o_ret, p_emb, *weights)


def _rope_tables(pos):
    pos = np.asarray(pos, np.float64)[:, None]
    n = pos.shape[0]
    inv_m = ROPE_THETA ** (-np.arange(ROPE_HALF, dtype=np.float64) / ROPE_HALF)
    cos_m, sin_m = np.cos(pos * inv_m[None, :]), np.sin(pos * inv_m[None, :])
    z = lambda width: np.zeros((n, width))
    tail = HEAD_PAD - ROPE_OFF - MLA_ROPE
    cm = np.concatenate([np.ones((n, ROPE_OFF)), cos_m, cos_m, z(tail)], axis=1)
    s1 = np.concatenate([z(ROPE_OFF), -sin_m, z(ROPE_HALF), z(tail)], axis=1)
    s2 = np.concatenate([z(ROPE_OFF), z(ROPE_HALF), sin_m, z(tail)], axis=1)
    half = RET_DK // 2
    inv_r = ROPE_THETA ** (-np.arange(half, dtype=np.float64) / half)
    cos_r, sin_r = np.cos(pos * inv_r[None, :]), np.sin(pos * inv_r[None, :])
    cr = np.concatenate([cos_r, cos_r], axis=1)
    sr = np.concatenate([-sin_r, sin_r], axis=1)
    tables = (cm, s1, s2, cr, sr, cos_m.T, sin_m.T)
    return tuple(np.ascontiguousarray(t, dtype=np.float32) for t in tables)


def _layer_weights(i, ffn1_norm, ffn1_w_gate, ffn1_w_up, ffn1_w_down, mix_norm, w_in, q_a_norm, w_q_b,
                   kv_a_norm, w_kv_b, ret_norm, w_branch_att, w_branch_ret, w_out,
                   ffn2_norm, ffn2_w_gate, ffn2_w_up, ffn2_w_down, ple_norm, w_ple_gate, w_ple_proj,
                   final_norm):
    vec = lambda a: a.reshape(1, -1)
    bf = lambda a: a.astype(BF16)
    win_t = jnp.transpose(w_in[i])
    zrows = lambda height: jnp.zeros((height, D_MODEL), F32)
    w_in_front_t = jnp.concatenate(
        [win_t[:_OFF_KPE], zrows(ROPE_OFF), win_t[_OFF_KPE:_OFF_RQ],
         zrows(HEAD_PAD - ROPE_OFF - MLA_ROPE), win_t[_OFF_RQ:_OFF_RG]], axis=0)
    wqb = w_q_b[i].reshape(Q_LORA, MLA_HEADS, MLA_NOPE + MLA_ROPE)
    wqb = jnp.pad(wqb, ((0, 0), (0, 0), (0, HEAD_PAD - MLA_NOPE - MLA_ROPE)))
    wkv = w_kv_b[i].reshape(KV_LORA, MLA_HEADS, MLA_NOPE + MLA_V)
    w_uk, w_uv = wkv[..., :MLA_NOPE], wkv[..., MLA_NOPE:]
    w_k = jnp.pad(w_uk, ((0, 0), (0, 0), (0, HEAD_PAD - MLA_NOPE)))
    pass_rope = np.zeros((1, KEY_W, HEAD_PAD), np.float32)
    pass_rope[0, KV_LORA:KV_LORA + MLA_ROPE, MLA_NOPE:MLA_NOPE + MLA_ROPE] = np.eye(MLA_ROPE)
    w_absorb_t = jnp.pad(jnp.transpose(w_uk, (1, 0, 2)),
                         ((0, 0), (0, KEY_W - KV_LORA), (0, HEAD_PAD - MLA_NOPE))) + pass_rope
    own_cols = np.eye(MLA_HEADS, dtype=np.float32)[:, None, :, None]
    w_unabsorb = (own_cols * jnp.transpose(w_uv, (1, 0, 2))[:, :, None, :]).reshape(
        MLA_HEADS, KV_LORA, _ATT_W)
    return {
        'ffn1_norm': vec(ffn1_norm[i]), 'ffn1_w_gate': bf(ffn1_w_gate[i]), 'ffn1_w_up': bf(ffn1_w_up[i]),
        'ffn1_w_down': bf(ffn1_w_down[i]), 'mix_norm': vec(mix_norm[i]),
        'w_in_front_t': bf(w_in_front_t), 'w_in_gates_t': bf(win_t[_OFF_RG:]),
        'q_a_norm': vec(q_a_norm[i]),
        'w_q_bt': bf(jnp.transpose(wqb.reshape(Q_LORA, MLA_HEADS * HEAD_PAD))),
        'kv_a_norm': vec(kv_a_norm[i]), 'w_k': bf(w_k.reshape(KV_LORA, MLA_HEADS * HEAD_PAD)),
        'w_vt': bf(jnp.pad(jnp.transpose(w_uv, (1, 2, 0)), ((0, 0), (0, V_ROWS - MLA_V), (0, 0)))
                   .reshape(_VT_ROWS, KV_LORA)),
        'w_absorb_t': bf(w_absorb_t), 'w_unabsorb': bf(w_unabsorb),
        'ret_norm': ret_norm[i],
        'w_branch_att': bf(w_branch_att[i]), 'w_branch_ret': bf(w_branch_ret[i]), 'w_out': bf(w_out[i]),
        'ffn2_norm': vec(ffn2_norm[i]), 'ffn2_w_gate': bf(ffn2_w_gate[i]), 'ffn2_w_up': bf(ffn2_w_up[i]),
        'ffn2_w_down': bf(ffn2_w_down[i]), 'ple_norm': vec(ple_norm[i]),
        'w_ple_gate': bf(w_ple_gate[i]), 'w_ple_proj': bf(w_ple_proj[i]),
        'final_norm': vec(final_norm),
    }


def kernel(x_prompt, x_sample, cache_ckv, cache_kpe, state_ret, page_table, p_prompt, p_sample, ffn1_norm, ffn1_w_gate, ffn1_w_up, ffn1_w_down, mix_norm, w_in, q_a_norm, w_q_b, kv_a_norm, w_kv_b, ret_norm, w_branch_att, w_branch_ret, w_out, ffn2_norm, ffn2_w_gate, ffn2_w_up, ffn2_w_down, ple_norm, w_ple_gate, w_ple_proj, final_norm):
    batch, seq, _ = x_prompt.shape
    n_dec, dec_seq, _ = x_sample.shape
    depth = w_in.shape[0]
    assert dec_seq == 1 and depth == 1
    n_past = page_table.shape[1] * PAGE_SIZE

    w = _layer_weights(0, ffn1_norm, ffn1_w_gate, ffn1_w_up, ffn1_w_down, mix_norm, w_in, q_a_norm,
                       w_q_b, kv_a_norm, w_kv_b, ret_norm, w_branch_att, w_branch_ret, w_out,
                       ffn2_norm, ffn2_w_gate, ffn2_w_up, ffn2_w_down, ple_norm, w_ple_gate,
                       w_ple_proj, final_norm)

    tabs_p = _rope_tables(np.arange(seq))
    (h_p, qt_p, k_p, vt_p, ckv_p, kpet_p, qr_p, kr_p, rv_p) = _front_stage(
        x_prompt.reshape(batch * seq, D_MODEL), tabs_p, w, TOKEN_TILE, seq // TOKEN_TILE)
    bt = lambda a: a.reshape(batch, seq, a.shape[-1])
    o_att_p = _prompt_attention(qt_p, k_p, vt_p, batch, seq)
    o_ret_p, ret_p = _prompt_retention(bt(qr_p), bt(kr_p), bt(rv_p), w['ret_norm'])
    y_p = _back_stage(h_p, o_att_p, o_ret_p.reshape(batch * seq, -1),
                      p_prompt.reshape(batch * seq, PLE_DIM), w, TOKEN_TILE)

    tabs_1 = _rope_tables(np.full((1,), n_past))
    tabs_s = (tuple(np.ascontiguousarray(np.broadcast_to(t, (n_dec, LANES))) for t in tabs_1[:5])
              + tuple(np.ascontiguousarray(np.broadcast_to(t, (ROPE_HALF, n_dec))) for t in tabs_1[5:]))
    (h_s, qt_s, _, _, ckv_s, kpet_s, qr_s, kr_s, rv_s) = _front_stage(
        x_sample.reshape(n_dec, D_MODEL), tabs_s, w, n_dec, 1)
    q_abs = jnp.transpose(_absorb_q(qt_s, w['w_absorb_t']), (2, 0, 1))
    kpe_s = jnp.transpose(kpet_s[0])
    k_new = jnp.concatenate([ckv_s, kpe_s,
                             jnp.zeros((n_dec, KEY_W - KV_LORA - MLA_ROPE), F32)], axis=1)
    o_lat = _paged_attention(page_table, q_abs, k_new.reshape(n_dec, 1, KEY_W), cache_ckv,
                             jnp.swapaxes(cache_kpe, 2, 3))
    o_att_s = _unabsorb(jnp.transpose(o_lat, (1, 0, 2)), w['w_unabsorb'])
    o_ret_s, ret_s = _sample_retention(qr_s, kr_s, rv_s, state_ret, w['ret_norm'])
    y_s = _back_stage(h_s, o_att_s, o_ret_s, p_sample.reshape(n_dec, PLE_DIM), w, n_dec)

    return (y_p.reshape(batch, seq, D_MODEL),
            y_s.reshape(n_dec, 1, D_MODEL),
            ckv_p.reshape(1, batch, seq, KV_LORA),
            jnp.transpose(kpet_p, (0, 2, 1))[None],
            ret_p[None],
            ckv_s.reshape(1, n_dec, 1, KV_LORA),
            kpe_s.reshape(1, n_dec, 1, MLA_ROPE),
            ret_s)
```

```python
import functools

import jax
import jax.numpy as jnp
import numpy as np
from jax import lax
from jax.experimental import pallas as pl
from jax.experimental.pallas import tpu as pltpu

F32 = jnp.float32
BF16 = jnp.bfloat16

D_MODEL = 1024
D_FF = 2816
PLE_DIM = 256
MLA_HEADS = 8
MLA_NOPE = 64
MLA_ROPE = 32
MLA_V = 64
Q_LORA = 256
KV_LORA = 128
RET_HEADS = 4
RET_DK = 128
RET_DV = 128
PAGE_SIZE = 128
ROPE_THETA = 10000.0
EPS = 1e-6

LANES = 128
HEAD_PAD = LANES
ROPE_OFF = MLA_NOPE
ROPE_HALF = MLA_ROPE // 2
SOFTMAX_SCALE = (MLA_NOPE + MLA_ROPE) ** -0.5
LOG2E = 1.4426950408889634
Q_SCALE = SOFTMAX_SCALE * LOG2E
NEG_BIG = -1e30

TOKEN_TILE = 512
ATTN_TILE = 512
RET_CHUNK = 512
PAGES_PER_STEP = 32
PAGE_GROUPS = 2
PAGE_SLOTS = 4
RET_SEQS_PER_STEP = 8
KEY_W = 2 * LANES
VMEM_LIMIT = 60 * 1024 * 1024

_OFF_CQ = 0
_OFF_CKV = _OFF_CQ + Q_LORA
_OFF_KPE = _OFF_CKV + KV_LORA
_OFF_RQ = _OFF_KPE + MLA_ROPE
_OFF_RK = _OFF_RQ + RET_HEADS * RET_DK
_OFF_RV = _OFF_RK + RET_HEADS * RET_DK
_OFF_RG = _OFF_RV + RET_HEADS * RET_DV
_RET_W = RET_HEADS * RET_DK
_ATT_W = MLA_HEADS * MLA_V
BF16_TILE_ROWS = 16
V_ROWS = MLA_V + BF16_TILE_ROWS
_VT_ROWS = MLA_HEADS * V_ROWS


def _rms(x, g):
    return x * lax.rsqrt(jnp.mean(x * x, axis=-1, keepdims=True) + EPS) * g


def _dot(a, b):
    return jnp.dot(a, b, preferred_element_type=F32)


def _dot_nt(a, b):
    return lax.dot_general(a, b, (((1,), (1,)), ((), ())), preferred_element_type=F32)


def _swiglu_half(x, g, wg, wu, wd):
    xn = _rms(x, g).astype(BF16)
    gate = _dot(xn, wg)
    up = _dot(xn, wu)
    act = (gate * jax.nn.sigmoid(gate) * up).astype(BF16)
    return x + 0.5 * _dot(act, wd)


def _const_spec(shape, index=None):
    index = (0,) * len(shape) if index is None else index
    return pl.BlockSpec(shape, lambda *_: index, pipeline_mode=pl.Buffered(1))


def _front_kernel(x_ref, cm_ref, s1_ref, s2_ref, cr_ref, sr_ref, ct_ref, st_ref,
                  g1_ref, wg_ref, wu_ref, wd_ref, gmix_ref, win_ref,
                  gqa_ref, wqbt_ref, gkv_ref, wk_ref, wvt_ref,
                  h_ref, qt_ref, k_ref, vt_ref, ckv_ref, kpet_ref, qr_ref, kr_ref, rv_ref):
    x = x_ref[...]
    h = _swiglu_half(x, g1_ref[...], wg_ref[...], wu_ref[...], wd_ref[...])
    h_ref[...] = h
    un = _rms(h, gmix_ref[...]).astype(BF16)
    z = _dot_nt(un, win_ref[...])

    cm, s1, s2 = cm_ref[...], s1_ref[...], s2_ref[...]

    def mla_rope(t):
        return (t * cm + pltpu.roll(t, LANES - ROPE_HALF, 1) * s1
                + pltpu.roll(t, ROPE_HALF, 1) * s2)

    cq = z[:, _OFF_CQ:_OFF_CQ + Q_LORA]
    qt = _dot_nt(wqbt_ref[...], _rms(cq, gqa_ref[...]).astype(BF16))
    ct, st = ct_ref[...], st_ref[...]
    c_kv = _rms(z[:, Q_LORA:Q_LORA + KV_LORA], gkv_ref[...])
    ckv_ref[...] = c_kv
    c_kv_b = c_kv.astype(BF16)
    k_nope = _dot(c_kv_b, wk_ref[...])
    vt = _dot_nt(wvt_ref[...], c_kv_b)
    sum_row = lax.broadcasted_iota(jnp.int32, vt.shape, 0) % V_ROWS >= MLA_V
    vt_ref[...] = jnp.where(sum_row, 1.0, vt).astype(BF16)
    k_pe = mla_rope(z[:, Q_LORA + KV_LORA:Q_LORA + KV_LORA + LANES])
    kpet_ref[0] = jnp.transpose(k_pe)[ROPE_OFF:ROPE_OFF + MLA_ROPE, :]
    for hd in range(MLA_HEADS):
        sl = slice(hd * HEAD_PAD, (hd + 1) * HEAD_PAD)
        k_ref[:, sl] = (k_nope[:, sl] + k_pe).astype(BF16)
        r0 = hd * HEAD_PAD + ROPE_OFF
        x1 = qt[r0:r0 + ROPE_HALF]
        x2 = qt[r0 + ROPE_HALF:r0 + MLA_ROPE]
        qt_ref[hd * HEAD_PAD:r0, :] = (qt[hd * HEAD_PAD:r0] * Q_SCALE).astype(BF16)
        qt_ref[r0:r0 + ROPE_HALF, :] = ((x1 * ct - x2 * st) * Q_SCALE).astype(BF16)
        qt_ref[r0 + ROPE_HALF:r0 + MLA_ROPE, :] = ((x1 * st + x2 * ct) * Q_SCALE).astype(BF16)
        qt_ref[r0 + MLA_ROPE:(hd + 1) * HEAD_PAD, :] = jnp.zeros(
            (HEAD_PAD - ROPE_OFF - MLA_ROPE, qt.shape[1]), BF16)

    cr, sr = cr_ref[...], sr_ref[...]
    base = Q_LORA + KV_LORA + LANES
    for hd in range(RET_HEADS):
        sl = slice(hd * RET_DK, (hd + 1) * RET_DK)
        rq = z[:, base + hd * RET_DK:base + (hd + 1) * RET_DK]
        rk = z[:, base + _RET_W + hd * RET_DK:base + _RET_W + (hd + 1) * RET_DK]
        qr_ref[:, sl] = rq * cr + pltpu.roll(rq, RET_DK // 2, 1) * sr
        kr_ref[:, sl] = (rk * cr + pltpu.roll(rk, RET_DK // 2, 1) * sr) * (RET_DK ** -0.5)
    rv_ref[...] = z[:, base + 2 * _RET_W:base + 3 * _RET_W]


def _front_stage(x, tables, w, tile, table_tiles):
    n = x.shape[0]
    seq_rows = tile * table_tiles
    row = lambda width: pl.BlockSpec((tile, width), lambda i: (i, 0))
    col = lambda height: pl.BlockSpec((height, tile), lambda i: (0, i))
    tab = pl.BlockSpec((tile, LANES), lambda i: (i % table_tiles, 0))
    tab_t = pl.BlockSpec((ROPE_HALF, tile), lambda i: (0, i % table_tiles))
    weights = [w['ffn1_norm'], w['ffn1_w_gate'], w['ffn1_w_up'], w['ffn1_w_down'], w['mix_norm'],
               w['w_in_front_t'], w['q_a_norm'], w['w_q_bt'], w['kv_a_norm'], w['w_k'], w['w_vt']]
    outs = [((n, D_MODEL), F32, row(D_MODEL)),
            ((MLA_HEADS * HEAD_PAD, n), BF16, col(MLA_HEADS * HEAD_PAD)),
            ((n, MLA_HEADS * HEAD_PAD), BF16, row(MLA_HEADS * HEAD_PAD)),
            ((_VT_ROWS, n), BF16, col(_VT_ROWS)),
            ((n, KV_LORA), F32, row(KV_LORA)),
            ((n // seq_rows, MLA_ROPE, seq_rows), F32,
             pl.BlockSpec((1, MLA_ROPE, tile), lambda i: (i // table_tiles, 0, i % table_tiles))),
            ((n, _RET_W), F32, row(_RET_W)),
            ((n, _RET_W), F32, row(_RET_W)),
            ((n, _RET_W), F32, row(_RET_W))]
    return pl.pallas_call(
        _front_kernel,
        grid=(n // tile,),
        in_specs=[row(D_MODEL)] + [tab] * 5 + [tab_t] * 2 + [_const_spec(a.shape) for a in weights],
        out_specs=[spec for _, _, spec in outs],
        out_shape=[jax.ShapeDtypeStruct(shape, dt) for shape, dt, _ in outs],
        compiler_params=pltpu.CompilerParams(dimension_semantics=("arbitrary",),
                                             vmem_limit_bytes=VMEM_LIMIT),
        name="front_stage",
    )(x, *tables, *weights)


def _attn_kernel(batch, qi_ref, ki_ref, *refs):
    qt_refs, k_refs, vt_refs = refs[:batch], refs[batch:2 * batch], refs[2 * batch:3 * batch]
    o_ref, m_sc, acc_sc = refs[3 * batch:]
    t = pl.program_id(0)
    qi = qi_ref[t]
    ki = ki_ref[t]
    units = [(b, hd) for b in range(batch) for hd in range(MLA_HEADS)]

    @pl.when(ki == 0)
    def _():
        m_sc[...] = jnp.full(m_sc.shape, NEG_BIG, F32)
        acc_sc[...] = jnp.zeros(acc_sc.shape, F32)

    def scores(unit):
        b, hd = unit
        kh = k_refs[b][:, hd * HEAD_PAD:(hd + 1) * HEAD_PAD]
        return _dot(kh, qt_refs[b][hd * HEAD_PAD:(hd + 1) * HEAD_PAD, :])

    def value_update(unit, alpha, p):
        b, hd = unit
        i = b * MLA_HEADS + hd
        acc_sc[i] = alpha * acc_sc[i] + _dot(vt_refs[b][hd * V_ROWS:(hd + 1) * V_ROWS, :], p)

    def update(masked):
        st_next = scores(units[0])
        pending = None
        for n, unit in enumerate(units):
            i = unit[0] * MLA_HEADS + unit[1]
            st = st_next
            if n + 1 < len(units):
                st_next = scores(units[n + 1])
            if masked:
                key = lax.broadcasted_iota(jnp.int32, st.shape, 0)
                qry = lax.broadcasted_iota(jnp.int32, st.shape, 1)
                st = jnp.where(key <= qry, st, NEG_BIG)
            m_prev = m_sc[i]
            m_new = jnp.maximum(m_prev, jnp.max(st, axis=0, keepdims=True))
            alpha = jnp.exp2(m_prev - m_new)
            p = jnp.exp2(st - m_new).astype(BF16)
            m_sc[i] = m_new
            if pending is not None:
                value_update(*pending)
            pending = (unit, alpha, p)
        value_update(*pending)

    @pl.when(ki < qi)
    def _():
        update(False)

    @pl.when(ki == qi)
    def _():
        update(True)
        acc = acc_sc[...]
        o_t = (acc[:, :MLA_V, :] / acc[:, MLA_V:MLA_V + 1, :]).reshape(batch, _ATT_W, ATTN_TILE)
        for b in range(batch):
            o_ref[b] = jnp.transpose(o_t[b]).astype(o_ref.dtype)


def _prompt_attention(qt, k, vt, batch, seq):
    nt = seq // ATTN_TILE
    qi_list = np.array([i for i in range(nt) for _ in range(i + 1)], np.int32)
    ki_list = np.array([j for i in range(nt) for j in range(i + 1)], np.int32)
    qk_w = MLA_HEADS * HEAD_PAD
    seqs = range(batch)
    q_specs = [pl.BlockSpec((qk_w, ATTN_TILE), lambda s, qi, ki, b=b: (0, b * nt + qi[s])) for b in seqs]
    k_specs = [pl.BlockSpec((ATTN_TILE, qk_w), lambda s, qi, ki, b=b: (b * nt + ki[s], 0)) for b in seqs]
    v_specs = [pl.BlockSpec((_VT_ROWS, ATTN_TILE), lambda s, qi, ki, b=b: (0, b * nt + ki[s]))
               for b in seqs]
    grid_spec = pltpu.PrefetchScalarGridSpec(
        num_scalar_prefetch=2,
        grid=(len(qi_list),),
        in_specs=q_specs + k_specs + v_specs,
        out_specs=pl.BlockSpec((batch, ATTN_TILE, _ATT_W), lambda s, qi, ki: (0, qi[s], 0)),
        scratch_shapes=[pltpu.VMEM((batch * MLA_HEADS, 1, ATTN_TILE), F32),
                        pltpu.VMEM((batch * MLA_HEADS, V_ROWS, ATTN_TILE), F32)],
    )
    out = pl.pallas_call(
        functools.partial(_attn_kernel, batch),
        grid_spec=grid_spec,
        out_shape=jax.ShapeDtypeStruct((batch, seq, _ATT_W), BF16),
        compiler_params=pltpu.CompilerParams(dimension_semantics=("arbitrary",),
                                             vmem_limit_bytes=VMEM_LIMIT),
        name="prompt_attention",
    )(jnp.asarray(qi_list), jnp.asarray(ki_list), *([qt] * batch), *([k] * batch), *([vt] * batch))
    return out.reshape(batch * seq, _ATT_W)


def _head_layernorm(o, g):
    mu = jnp.mean(o, axis=-1, keepdims=True)
    d = o - mu
    var = jnp.mean(d * d, axis=-1, keepdims=True)
    return d * lax.rsqrt(var + EPS) * g


def _ret_prompt_kernel(q_ref, k_ref, v_ref, dec_ref, qd_ref, kd_ref, gc_ref, g_ref,
                       o_ref, s_out_ref, s_sc):
    c = pl.program_id(1)

    @pl.when(c == 0)
    def _():
        s_sc[...] = jnp.zeros(s_sc.shape, F32)

    for hd in range(RET_HEADS):
        sl = slice(hd * RET_DK, (hd + 1) * RET_DK)
        q = q_ref[0, :, sl]
        k = k_ref[0, :, sl]
        vb = v_ref[0, :, sl].astype(BF16)
        qb = q.astype(BF16)
        state = s_sc[hd]
        scores = _dot_nt(qb, k.astype(BF16)) * dec_ref[hd]
        inner = _dot(scores.astype(BF16), vb)
        cross = _dot(qb, state.astype(BF16)) * qd_ref[hd]
        o_ref[0, :, sl] = _head_layernorm(inner + cross, g_ref[hd:hd + 1, :])
        k_dec_t = jnp.transpose(k * kd_ref[hd]).astype(BF16)
        s_sc[hd] = gc_ref[hd:hd + 1, :] * state + _dot(k_dec_t, vb)

    @pl.when(c == pl.num_programs(1) - 1)
    def _():
        s_out_ref[0] = s_sc[...]


def _log_gamma():
    return np.log1p(-np.exp2(-5.0 - np.arange(RET_HEADS, dtype=np.float64)))


def _retention_consts(chunk):
    log_gamma = _log_gamma()
    idx = np.arange(chunk, dtype=np.float64)
    diff = idx[:, None] - idx[None, :]
    decay = np.where(diff[None] >= 0,
                     np.exp(log_gamma[:, None, None] * np.maximum(diff, 0.0)[None]), 0.0)
    q_decay = np.exp(log_gamma[:, None] * (idx + 1.0)[None, :])
    k_decay = np.exp(log_gamma[:, None] * (chunk - 1.0 - idx)[None, :])
    chunk_decay = np.exp(log_gamma * chunk)
    lane = lambda a: np.broadcast_to(a[..., None], a.shape + (LANES,))
    tables = (decay, lane(q_decay), lane(k_decay), lane(chunk_decay))
    return tuple(np.ascontiguousarray(t, dtype=np.float32) for t in tables)


def _prompt_retention(qr, kr, rv, ret_norm):
    b, t, _ = qr.shape
    decay, q_decay, k_decay, chunk_decay = _retention_consts(RET_CHUNK)
    seq = pl.BlockSpec((1, RET_CHUNK, _RET_W), lambda bb, c: (bb, c, 0))
    return pl.pallas_call(
        _ret_prompt_kernel,
        grid=(b, t // RET_CHUNK),
        in_specs=[seq, seq, seq, _const_spec(decay.shape), _const_spec(q_decay.shape),
                  _const_spec(k_decay.shape), _const_spec(chunk_decay.shape),
                  _const_spec(ret_norm.shape)],
        out_specs=[seq, pl.BlockSpec((1, RET_HEADS, RET_DK, RET_DV), lambda bb, c: (bb, 0, 0, 0))],
        out_shape=[jax.ShapeDtypeStruct((b, t, _RET_W), F32),
                   jax.ShapeDtypeStruct((b, RET_HEADS, RET_DK, RET_DV), F32)],
        scratch_shapes=[pltpu.VMEM((RET_HEADS, RET_DK, RET_DV), F32)],
        compiler_params=pltpu.CompilerParams(dimension_semantics=("arbitrary", "arbitrary")),
        name="prompt_retention",
    )(qr, kr, rv, decay, q_decay, k_decay, chunk_decay, ret_norm)


def _ret_sample_kernel(q_ref, k_ref, v_ref, s_ref, gam_ref, g_ref, o_ref, s_out_ref):
    rows = lax.broadcasted_iota(jnp.int32, (RET_DK, RET_DK), 0)
    cols = lax.broadcasted_iota(jnp.int32, (RET_DK, RET_DK), 1)
    eye = rows == cols

    def column(r):
        return jnp.sum(jnp.where(eye, jnp.broadcast_to(r, (RET_DK, RET_DK)), 0.0),
                       axis=1, keepdims=True)

    heads = range(RET_HEADS)
    lanes = [slice(hd * RET_DK, (hd + 1) * RET_DK) for hd in heads]
    gams = [gam_ref[hd:hd + 1, :] for hd in heads]
    for i in range(RET_SEQS_PER_STEP):
        qs = [q_ref[i, :, sl] for sl in lanes]
        ks = [k_ref[i, :, sl] for sl in lanes]
        vs = [v_ref[i, :, sl] for sl in lanes]
        q_cols = [column(q) for q in qs]
        k_cols = [column(k) for k in ks]
        qk = [jnp.sum(q * k, axis=-1, keepdims=True) for q, k in zip(qs, ks)]
        states = [s_ref[0, i, hd] for hd in heads]
        outs = [qk[hd] * vs[hd] + jnp.sum(q_cols[hd] * states[hd], axis=0, keepdims=True) * gams[hd]
                for hd in heads]
        for hd in heads:
            s_out_ref[0, i, hd] = gams[hd] * states[hd] + k_cols[hd] * vs[hd]
        mus = [jnp.mean(o, axis=-1, keepdims=True) for o in outs]
        devs = [o - mu for o, mu in zip(outs, mus)]
        variances = [jnp.mean(d * d, axis=-1, keepdims=True) for d in devs]
        for hd in heads:
            o_ref[i, :, lanes[hd]] = devs[hd] * lax.rsqrt(variances[hd] + EPS) * g_ref[hd:hd + 1, :]


def _sample_retention(qr, kr, rv, state, ret_norm):
    n = qr.shape[0]
    gam = np.ascontiguousarray(
        np.broadcast_to(np.exp(_log_gamma() * 1.0)[:, None], (RET_HEADS, LANES)), dtype=np.float32)
    g = RET_SEQS_PER_STEP
    tok = pl.BlockSpec((g, 1, _RET_W), lambda i: (i, 0, 0))
    st = pl.BlockSpec((1, g, RET_HEADS, RET_DK, RET_DV), lambda i: (0, i, 0, 0, 0))
    r3 = lambda a: a.reshape(n, 1, _RET_W)
    o, s_new = pl.pallas_call(
        _ret_sample_kernel,
        grid=(n // g,),
        in_specs=[tok, tok, tok, st, _const_spec(gam.shape), _const_spec(ret_norm.shape)],
        out_specs=[tok, st],
        out_shape=[jax.ShapeDtypeStruct((n, 1, _RET_W), F32),
                   jax.ShapeDtypeStruct(state.shape, F32)],
        compiler_params=pltpu.CompilerParams(dimension_semantics=("arbitrary",)),
        name="sample_retention",
    )(r3(qr), r3(kr), r3(rv), state, gam, ret_norm)
    return o.reshape(n, _RET_W), s_new


def _absorb_q_kernel(qt_ref, wt_ref, o_ref):
    for hd in range(MLA_HEADS):
        o_ref[hd] = _dot(wt_ref[hd], qt_ref[hd * HEAD_PAD:(hd + 1) * HEAD_PAD, :])


def _absorb_q(qt, w_absorb_t):
    n = qt.shape[1]
    return pl.pallas_call(
        _absorb_q_kernel,
        out_shape=jax.ShapeDtypeStruct((MLA_HEADS, KEY_W, n), F32),
        name="sample_absorb_q",
    )(qt, w_absorb_t)


def _unabsorb_kernel(o_ref, w_ref, out_ref):
    acc = _dot(o_ref[0].astype(BF16), w_ref[0])
    for hd in range(1, MLA_HEADS):
        acc = acc + _dot(o_ref[hd].astype(BF16), w_ref[hd])
    out_ref[...] = acc.astype(out_ref.dtype)


def _unabsorb(o_lat, w_unabsorb):
    n = o_lat.shape[1]
    return pl.pallas_call(
        _unabsorb_kernel,
        out_shape=jax.ShapeDtypeStruct((n, _ATT_W), BF16),
        name="sample_unabsorb",
    )(o_lat, w_unabsorb)


def _paged_kernel(pt_ref, q_ref, knew_ref, ckv_hbm, kpet_hbm, o_ref, buf, kpe_buf, sem):
    b = pl.program_id(0)
    n_seq = pl.num_programs(0)
    n_chunks = pt_ref.shape[1] // PAGES_PER_STEP
    assert n_chunks == PAGE_SLOTS
    ahead = PAGE_SLOTS - 1
    chunk_keys = PAGES_PER_STEP * PAGE_SIZE
    group_keys = chunk_keys // PAGE_GROUPS

    def page_copies(bb, cc, sl):
        out = []
        for j in range(PAGES_PER_STEP):
            pg = pt_ref[bb, cc * PAGES_PER_STEP + j]
            keys = pl.ds(j * PAGE_SIZE, PAGE_SIZE)
            out.append(pltpu.make_async_copy(ckv_hbm.at[0, pg], buf.at[sl, keys], sem.at[sl, 0]))
            out.append(pltpu.make_async_copy(kpet_hbm.at[0, pg], kpe_buf.at[sl, :, keys],
                                             sem.at[sl, 1]))
        return out

    @pl.when(b == 0)
    def _():
        for c in range(ahead):
            for cp in page_copies(0, c, c):
                cp.start()

    b_next = jnp.minimum(b + 1, n_seq - 1)

    q = q_ref[0]
    q_lat = q[:, 0:KV_LORA]
    q_pe = q[:, KV_LORA:KV_LORA + MLA_ROPE]

    def scores(sl, g):
        keys = slice(g * group_keys, (g + 1) * group_keys)
        return _dot_nt(q_lat, buf[sl, keys, :]) + _dot(q_pe, kpe_buf[sl, :, keys])

    m_run = jnp.full((MLA_HEADS, 1), NEG_BIG, F32)
    l_run = jnp.zeros((MLA_HEADS, 1), F32)
    acc_run = jnp.zeros((MLA_HEADS, KV_LORA), F32)
    def wait_chunk(c):
        pltpu.make_async_copy(buf.at[c], buf.at[c], sem.at[c, 0]).wait()
        pltpu.make_async_copy(kpe_buf.at[c], kpe_buf.at[c], sem.at[c, 1]).wait()

    wait_chunk(0)
    s_next = scores(0, 0)
    for c in range(n_chunks):
        sl = c
        nxt = c + ahead
        prefetch = (page_copies(b, nxt, nxt) if nxt < n_chunks
                    else page_copies(b_next, nxt - n_chunks, nxt - n_chunks))
        parts = []
        per_group = len(prefetch) // PAGE_GROUPS
        for g in range(PAGE_GROUPS):
            s = s_next
            if g + 1 < PAGE_GROUPS:
                s_next = scores(sl, g + 1)
            elif c + 1 < n_chunks:
                wait_chunk(c + 1)
                s_next = scores(c + 1, 0)
            for cp in prefetch[g * per_group:(g + 1) * per_group]:
                cp.start()
            m_g = jnp.max(s, axis=-1, keepdims=True)
            p = jnp.exp2(s - m_g)
            l_g = jnp.sum(p, axis=-1, keepdims=True)
            keys = slice(g * group_keys, (g + 1) * group_keys)
            parts.append((m_g, l_g, _dot(p, buf[sl, keys, :])))

        m_new = m_run
        for m_g, _, _ in parts:
            m_new = jnp.maximum(m_new, m_g)
        a_run = jnp.exp2(m_run - m_new)
        l_run = a_run * l_run
        acc_run = a_run * acc_run
        for m_g, l_g, o_g in parts:
            a_g = jnp.exp2(m_g - m_new)
            l_run = l_run + a_g * l_g
            acc_run = acc_run + a_g * o_g
        m_run = m_new

    kn = knew_ref[0]
    s_self = jnp.sum(q * kn, axis=-1, keepdims=True)
    m_fin = jnp.maximum(m_run, s_self)
    a = jnp.exp2(m_run - m_fin)
    p_self = jnp.exp2(s_self - m_fin)
    o_ref[0] = (a * acc_run + p_self * kn[:, 0:KV_LORA]) / (a * l_run + p_self)

    @pl.when(b == n_seq - 1)
    def _():
        for c in range(ahead):
            wait_chunk(c)


def _paged_attention(page_table, q_abs, k_new, cache_ckv, cache_kpe_t):
    n, n_pages = page_table.shape
    chunk_keys = PAGES_PER_STEP * PAGE_SIZE
    grid_spec = pltpu.PrefetchScalarGridSpec(
        num_scalar_prefetch=1,
        grid=(n,),
        in_specs=[
            pl.BlockSpec((1, MLA_HEADS, KEY_W), lambda b, pt: (b, 0, 0)),
            pl.BlockSpec((1, 1, KEY_W), lambda b, pt: (b, 0, 0)),
            pl.BlockSpec(memory_space=pl.ANY),
            pl.BlockSpec(memory_space=pl.ANY),
        ],
        out_specs=pl.BlockSpec((1, MLA_HEADS, KV_LORA), lambda b, pt: (b, 0, 0)),
        scratch_shapes=[pltpu.VMEM((PAGE_SLOTS, chunk_keys, KV_LORA), F32),
                        pltpu.VMEM((PAGE_SLOTS, MLA_ROPE, chunk_keys), F32),
                        pltpu.SemaphoreType.DMA((PAGE_SLOTS, 2))],
    )
    return pl.pallas_call(
        _paged_kernel,
        grid_spec=grid_spec,
        out_shape=jax.ShapeDtypeStruct((n, MLA_HEADS, KV_LORA), F32),
        compiler_params=pltpu.CompilerParams(dimension_semantics=("arbitrary",),
                                             vmem_limit_bytes=VMEM_LIMIT),
        name="sample_paged_attention",
    )(page_table, q_abs, k_new, cache_ckv, cache_kpe_t)


def _back_kernel(h_ref, oatt_ref, oret_ref, p_ref,
                 gmix_ref, wgate_ref, wba_ref, wbr_ref, wout_ref,
                 g2_ref, wg_ref, wu_ref, wd_ref, gple_ref, wpg_ref, wpp_ref, gfin_ref, y_ref):
    h = h_ref[...]
    un = _rms(h, gmix_ref[...]).astype(BF16)
    gates = _dot_nt(un, wgate_ref[...])
    rg = gates[:, :_RET_W]
    ga = gates[:, _RET_W:_RET_W + D_MODEL]
    gr = gates[:, _RET_W + D_MODEL:]
    o_ret = (rg * jax.nn.sigmoid(rg) * oret_ref[...]).astype(BF16)
    merged = (jax.nn.sigmoid(ga) * _dot(oatt_ref[...], wba_ref[...])
              + jax.nn.sigmoid(gr) * _dot(o_ret, wbr_ref[...]))
    h = h + _dot(merged.astype(BF16), wout_ref[...])
    h = _swiglu_half(h, g2_ref[...], wg_ref[...], wu_ref[...], wd_ref[...])
    gate = jax.nn.sigmoid(_dot(_rms(h, gple_ref[...]).astype(BF16), wpg_ref[...]))
    h = h + gate * _dot(p_ref[...].astype(BF16), wpp_ref[...])
    y_ref[...] = _rms(h, gfin_ref[...])


def _back_stage(h, o_att, o_ret, p_emb, w, tile):
    n = h.shape[0]
    row = lambda width: pl.BlockSpec((tile, width), lambda i: (i, 0))
    weights = [w['mix_norm'], w['w_in_gates_t'], w['w_branch_att'], w['w_branch_ret'], w['w_out'],
               w['ffn2_norm'], w['ffn2_w_gate'], w['ffn2_w_up'], w['ffn2_w_down'],
               w['ple_norm'], w['w_ple_gate'], w['w_ple_proj'], w['final_norm']]
    return pl.pallas_call(
        _back_kernel,
        grid=(n // tile,),
        in_specs=[row(D_MODEL), row(_ATT_W), row(_RET_W), row(PLE_DIM)]
        + [_const_spec(a.shape) for a in weights],
        out_specs=row(D_MODEL),
        out_shape=jax.ShapeDtypeStruct((n, D_MODEL), F32),
        compiler_params=pltpu.CompilerParams(dimension_semantics=("arbitrary",),
                                             vmem_limit_bytes=VMEM_LIMIT),
        name="back_stage",
    )(h, o_att, o_ret, p_emb, *weights)


def _rope_tables(pos):
    pos = np.asarray(pos, np.float64)[:, None]
    n = pos.shape[0]
    inv_m = ROPE_THETA ** (-np.arange(ROPE_HALF, dtype=np.float64) / ROPE_HALF)
    cos_m, sin_m = np.cos(pos * inv_m[None, :]), np.sin(pos * inv_m[None, :])
    z = lambda width: np.zeros((n, width))
    tail = HEAD_PAD - ROPE_OFF - MLA_ROPE
    cm = np.concatenate([np.ones((n, ROPE_OFF)), cos_m, cos_m, z(tail)], axis=1)
    s1 = np.concatenate([z(ROPE_OFF), -sin_m, z(ROPE_HALF), z(tail)], axis=1)
    s2 = np.concatenate([z(ROPE_OFF), z(ROPE_HALF), sin_m, z(tail)], axis=1)
    half = RET_DK // 2
    inv_r = ROPE_THETA ** (-np.arange(half, dtype=np.float64) / half)
    cos_r, sin_r = np.cos(pos * inv_r[None, :]), np.sin(pos * inv_r[None, :])
    cr = np.concatenate([cos_r, cos_r], axis=1)
    sr = np.concatenate([-sin_r, sin_r], axis=1)
    tables = (cm, s1, s2, cr, sr, cos_m.T, sin_m.T)
    return tuple(np.ascontiguousarray(t, dtype=np.float32) for t in tables)


def _layer_weights(i, ffn1_norm, ffn1_w_gate, ffn1_w_up, ffn1_w_down, mix_norm, w_in, q_a_norm, w_q_b,
                   kv_a_norm, w_kv_b, ret_norm, w_branch_att, w_branch_ret, w_out,
                   ffn2_norm, ffn2_w_gate, ffn2_w_up, ffn2_w_down, ple_norm, w_ple_gate, w_ple_proj,
                   final_norm):
    vec = lambda a: a.reshape(1, -1)
    bf = lambda a: a.astype(BF16)
    win_t = jnp.transpose(w_in[i])
    zrows = lambda height: jnp.zeros((height, D_MODEL), F32)
    w_in_front_t = jnp.concatenate(
        [win_t[:_OFF_KPE], zrows(ROPE_OFF), win_t[_OFF_KPE:_OFF_RQ],
         zrows(HEAD_PAD - ROPE_OFF - MLA_ROPE), win_t[_OFF_RQ:_OFF_RG]], axis=0)
    wqb = w_q_b[i].reshape(Q_LORA, MLA_HEADS, MLA_NOPE + MLA_ROPE)
    wqb = jnp.pad(wqb, ((0, 0), (0, 0), (0, HEAD_PAD - MLA_NOPE - MLA_ROPE)))
    wkv = w_kv_b[i].reshape(KV_LORA, MLA_HEADS, MLA_NOPE + MLA_V)
    w_uk, w_uv = wkv[..., :MLA_NOPE], wkv[..., MLA_NOPE:]
    w_k = jnp.pad(w_uk, ((0, 0), (0, 0), (0, HEAD_PAD - MLA_NOPE)))
    pass_rope = np.zeros((1, KEY_W, HEAD_PAD), np.float32)
    pass_rope[0, KV_LORA:KV_LORA + MLA_ROPE, MLA_NOPE:MLA_NOPE + MLA_ROPE] = np.eye(MLA_ROPE)
    w_absorb_t = jnp.pad(jnp.transpose(w_uk, (1, 0, 2)),
                         ((0, 0), (0, KEY_W - KV_LORA), (0, HEAD_PAD - MLA_NOPE))) + pass_rope
    own_cols = np.eye(MLA_HEADS, dtype=np.float32)[:, None, :, None]
    w_unabsorb = (own_cols * jnp.transpose(w_uv, (1, 0, 2))[:, :, None, :]).reshape(
        MLA_HEADS, KV_LORA, _ATT_W)
    return {
        'ffn1_norm': vec(ffn1_norm[i]), 'ffn1_w_gate': bf(ffn1_w_gate[i]), 'ffn1_w_up': bf(ffn1_w_up[i]),
        'ffn1_w_down': bf(ffn1_w_down[i]), 'mix_norm': vec(mix_norm[i]),
        'w_in_front_t': bf(w_in_front_t), 'w_in_gates_t': bf(win_t[_OFF_RG:]),
        'q_a_norm': vec(q_a_norm[i]),
        'w_q_bt': bf(jnp.transpose(wqb.reshape(Q_LORA, MLA_HEADS * HEAD_PAD))),
        'kv_a_norm': vec(kv_a_norm[i]), 'w_k': bf(w_k.reshape(KV_LORA, MLA_HEADS * HEAD_PAD)),
        'w_vt': bf(jnp.pad(jnp.transpose(w_uv, (1, 2, 0)), ((0, 0), (0, V_ROWS - MLA_V), (0, 0)))
                   .reshape(_VT_ROWS, KV_LORA)),
        'w_absorb_t': bf(w_absorb_t), 'w_unabsorb': bf(w_unabsorb),
        'ret_norm': ret_norm[i],
        'w_branch_att': bf(w_branch_att[i]), 'w_branch_ret': bf(w_branch_ret[i]), 'w_out': bf(w_out[i]),
        'ffn2_norm': vec(ffn2_norm[i]), 'ffn2_w_gate': bf(ffn2_w_gate[i]), 'ffn2_w_up': bf(ffn2_w_up[i]),
        'ffn2_w_down': bf(ffn2_w_down[i]), 'ple_norm': vec(ple_norm[i]),
        'w_ple_gate': bf(w_ple_gate[i]), 'w_ple_proj': bf(w_ple_proj[i]),
        'final_norm': vec(final_norm),
    }


def kernel(x_prompt, x_sample, cache_ckv, cache_kpe, state_ret, page_table, p_prompt, p_sample, ffn1_norm, ffn1_w_gate, ffn1_w_up, ffn1_w_down, mix_norm, w_in, q_a_norm, w_q_b, kv_a_norm, w_kv_b, ret_norm, w_branch_att, w_branch_ret, w_out, ffn2_norm, ffn2_w_gate, ffn2_w_up, ffn2_w_down, ple_norm, w_ple_gate, w_ple_proj, final_norm):
    batch, seq, _ = x_prompt.shape
    n_dec, dec_seq, _ = x_sample.shape
    depth = w_in.shape[0]
    assert dec_seq == 1 and depth == 1
    n_past = page_table.shape[1] * PAGE_SIZE

    w = _layer_weights(0, ffn1_norm, ffn1_w_gate, ffn1_w_up, ffn1_w_down, mix_norm, w_in, q_a_norm,
                       w_q_b, kv_a_norm, w_kv_b, ret_norm, w_branch_att, w_branch_ret, w_out,
                       ffn2_norm, ffn2_w_gate, ffn2_w_up, ffn2_w_down, ple_norm, w_ple_gate,
                       w_ple_proj, final_norm)

    tabs_p = _rope_tables(np.arange(seq))
    (h_p, qt_p, k_p, vt_p, ckv_p, kpet_p, qr_p, kr_p, rv_p) = _front_stage(
        x_prompt.reshape(batch * seq, D_MODEL), tabs_p, w, TOKEN_TILE, seq // TOKEN_TILE)
    bt = lambda a: a.reshape(batch, seq, a.shape[-1])
    o_att_p = _prompt_attention(qt_p, k_p, vt_p, batch, seq)
    o_ret_p, ret_p = _prompt_retention(bt(qr_p), bt(kr_p), bt(rv_p), w['ret_norm'])
    y_p = _back_stage(h_p, o_att_p, o_ret_p.reshape(batch * seq, -1),
                      p_prompt.reshape(batch * seq, PLE_DIM), w, TOKEN_TILE)

    tabs_1 = _rope_tables(np.full((1,), n_past))
    tabs_s = (tuple(np.ascontiguousarray(np.broadcast_to(t, (n_dec, LANES))) for t in tabs_1[:5])
              + tuple(np.ascontiguousarray(np.broadcast_to(t, (ROPE_HALF, n_dec))) for t in tabs_1[5:]))
    (h_s, qt_s, _, _, ckv_s, kpet_s, qr_s, kr_s, rv_s) = _front_stage(
        x_sample.reshape(n_dec, D_MODEL), tabs_s, w, n_dec, 1)
    q_abs = jnp.transpose(_absorb_q(qt_s, w['w_absorb_t']), (2, 0, 1))
    kpe_s = jnp.transpose(kpet_s[0])
    k_new = jnp.concatenate([ckv_s, kpe_s,
                             jnp.zeros((n_dec, KEY_W - KV_LORA - MLA_ROPE), F32)], axis=1)
    o_lat = _paged_attention(page_table, q_abs, k_new.reshape(n_dec, 1, KEY_W), cache_ckv,
                             jnp.swapaxes(cache_kpe, 2, 3))
    o_att_s = _unabsorb(jnp.transpose(o_lat, (1, 0, 2)), w['w_unabsorb'])
    o_ret_s, ret_s = _sample_retention(qr_s, kr_s, rv_s, state_ret, w['ret_norm'])
    y_s = _back_stage(h_s, o_att_s, o_ret_s, p_sample.reshape(n_dec, PLE_DIM), w, n_dec)

    return (y_p.reshape(batch, seq, D_MODEL),
            y_s.reshape(n_dec, 1, D_MODEL),
            ckv_p.reshape(1, batch, seq, KV_LORA),
            jnp.transpose(kpet_p, (0, 2, 1))[None],
            ret_p[None],
            ckv_s.reshape(1, n_dec, 1, KV_LORA),
            kpe_s.reshape(1, n_dec, 1, MLA_ROPE),
            ret_s)
```

```python
import functools

import jax
import jax.numpy as jnp
import numpy as np
from jax import lax
from jax.experimental import pallas as pl
from jax.experimental.pallas import tpu as pltpu

F32 = jnp.float32
BF16 = jnp.bfloat16

D_MODEL = 1024
D_FF = 2816
PLE_DIM = 256
MLA_HEADS = 8
MLA_NOPE = 64
MLA_ROPE = 32
MLA_V = 64
Q_LORA = 256
KV_LORA = 128
RET_HEADS = 4
RET_DK = 128
RET_DV = 128
PAGE_SIZE = 128
ROPE_THETA = 10000.0
EPS = 1e-6

LANES = 128
HEAD_PAD = LANES
ROPE_OFF = MLA_NOPE
ROPE_HALF = MLA_ROPE // 2
SOFTMAX_SCALE = (MLA_NOPE + MLA_ROPE) ** -0.5
LOG2E = 1.4426950408889634
Q_SCALE = SOFTMAX_SCALE * LOG2E
NEG_BIG = -1e30

TOKEN_TILE = 512
ATTN_TILE = 512
RET_CHUNK = 512
PAGES_PER_STEP = 32
PAGE_GROUPS = 4
PAGE_SLOTS = 4
RET_SEQS_PER_STEP = 8
KEY_W = 2 * LANES
VMEM_LIMIT = 60 * 1024 * 1024

_OFF_CQ = 0
_OFF_CKV = _OFF_CQ + Q_LORA
_OFF_KPE = _OFF_CKV + KV_LORA
_OFF_RQ = _OFF_KPE + MLA_ROPE
_OFF_RK = _OFF_RQ + RET_HEADS * RET_DK
_OFF_RV = _OFF_RK + RET_HEADS * RET_DK
_OFF_RG = _OFF_RV + RET_HEADS * RET_DV
_RET_W = RET_HEADS * RET_DK
_ATT_W = MLA_HEADS * MLA_V
BF16_TILE_ROWS = 16
V_ROWS = MLA_V + BF16_TILE_ROWS
_VT_ROWS = MLA_HEADS * V_ROWS


def _rms(x, g):
    return x * lax.rsqrt(jnp.mean(x * x, axis=-1, keepdims=True) + EPS) * g


def _dot(a, b):
    return jnp.dot(a, b, preferred_element_type=F32)


def _dot_nt(a, b):
    return lax.dot_general(a, b, (((1,), (1,)), ((), ())), preferred_element_type=F32)


def _swiglu_half(x, g, wg, wu, wd):
    xn = _rms(x, g).astype(BF16)
    gate = _dot(xn, wg)
    up = _dot(xn, wu)
    act = (gate * jax.nn.sigmoid(gate) * up).astype(BF16)
    return x + 0.5 * _dot(act, wd)


def _const_spec(shape, index=None):
    index = (0,) * len(shape) if index is None else index
    return pl.BlockSpec(shape, lambda *_: index, pipeline_mode=pl.Buffered(1))


def _front_kernel(x_ref, cm_ref, s1_ref, s2_ref, cr_ref, sr_ref, ct_ref, st_ref,
                  g1_ref, wg_ref, wu_ref, wd_ref, gmix_ref, win_ref,
                  gqa_ref, wqbt_ref, gkv_ref, wk_ref, wvt_ref,
                  h_ref, qt_ref, k_ref, vt_ref, ckv_ref, kpet_ref, qr_ref, kr_ref, rv_ref):
    x = x_ref[...]
    h = _swiglu_half(x, g1_ref[...], wg_ref[...], wu_ref[...], wd_ref[...])
    h_ref[...] = h
    un = _rms(h, gmix_ref[...]).astype(BF16)
    z = _dot_nt(un, win_ref[...])

    cm, s1, s2 = cm_ref[...], s1_ref[...], s2_ref[...]

    def mla_rope(t):
        return (t * cm + pltpu.roll(t, LANES - ROPE_HALF, 1) * s1
                + pltpu.roll(t, ROPE_HALF, 1) * s2)

    cq = z[:, _OFF_CQ:_OFF_CQ + Q_LORA]
    qt = _dot_nt(wqbt_ref[...], _rms(cq, gqa_ref[...]).astype(BF16))
    ct, st = ct_ref[...], st_ref[...]
    c_kv = _rms(z[:, Q_LORA:Q_LORA + KV_LORA], gkv_ref[...])
    ckv_ref[...] = c_kv
    c_kv_b = c_kv.astype(BF16)
    k_nope = _dot(c_kv_b, wk_ref[...])
    vt = _dot_nt(wvt_ref[...], c_kv_b)
    sum_row = lax.broadcasted_iota(jnp.int32, vt.shape, 0) % V_ROWS >= MLA_V
    vt_ref[...] = jnp.where(sum_row, 1.0, vt).astype(BF16)
    k_pe = mla_rope(z[:, Q_LORA + KV_LORA:Q_LORA + KV_LORA + LANES])
    kpet_ref[0] = jnp.transpose(k_pe)[ROPE_OFF:ROPE_OFF + MLA_ROPE, :]
    for hd in range(MLA_HEADS):
        sl = slice(hd * HEAD_PAD, (hd + 1) * HEAD_PAD)
        k_ref[:, sl] = (k_nope[:, sl] + k_pe).astype(BF16)
        r0 = hd * HEAD_PAD + ROPE_OFF
        x1 = qt[r0:r0 + ROPE_HALF]
        x2 = qt[r0 + ROPE_HALF:r0 + MLA_ROPE]
        qt_ref[hd * HEAD_PAD:r0, :] = (qt[hd * HEAD_PAD:r0] * Q_SCALE).astype(BF16)
        qt_ref[r0:r0 + ROPE_HALF, :] = ((x1 * ct - x2 * st) * Q_SCALE).astype(BF16)
        qt_ref[r0 + ROPE_HALF:r0 + MLA_ROPE, :] = ((x1 * st + x2 * ct) * Q_SCALE).astype(BF16)
        qt_ref[r0 + MLA_ROPE:(hd + 1) * HEAD_PAD, :] = jnp.zeros(
            (HEAD_PAD - ROPE_OFF - MLA_ROPE, qt.shape[1]), BF16)

    cr, sr = cr_ref[...], sr_ref[...]
    base = Q_LORA + KV_LORA + LANES
    for hd in range(RET_HEADS):
        sl = slice(hd * RET_DK, (hd + 1) * RET_DK)
        rq = z[:, base + hd * RET_DK:base + (hd + 1) * RET_DK]
        rk = z[:, base + _RET_W + hd * RET_DK:base + _RET_W + (hd + 1) * RET_DK]
        qr_ref[:, sl] = rq * cr + pltpu.roll(rq, RET_DK // 2, 1) * sr
        kr_ref[:, sl] = (rk * cr + pltpu.roll(rk, RET_DK // 2, 1) * sr) * (RET_DK ** -0.5)
    rv_ref[...] = z[:, base + 2 * _RET_W:base + 3 * _RET_W]


def _front_stage(x, tables, w, tile, table_tiles):
    n = x.shape[0]
    seq_rows = tile * table_tiles
    row = lambda width: pl.BlockSpec((tile, width), lambda i: (i, 0))
    col = lambda height: pl.BlockSpec((height, tile), lambda i: (0, i))
    tab = pl.BlockSpec((tile, LANES), lambda i: (i % table_tiles, 0))
    tab_t = pl.BlockSpec((ROPE_HALF, tile), lambda i: (0, i % table_tiles))
    weights = [w['ffn1_norm'], w['ffn1_w_gate'], w['ffn1_w_up'], w['ffn1_w_down'], w['mix_norm'],
               w['w_in_front_t'], w['q_a_norm'], w['w_q_bt'], w['kv_a_norm'], w['w_k'], w['w_vt']]
    outs = [((n, D_MODEL), F32, row(D_MODEL)),
            ((MLA_HEADS * HEAD_PAD, n), BF16, col(MLA_HEADS * HEAD_PAD)),
            ((n, MLA_HEADS * HEAD_PAD), BF16, row(MLA_HEADS * HEAD_PAD)),
            ((_VT_ROWS, n), BF16, col(_VT_ROWS)),
            ((n, KV_LORA), F32, row(KV_LORA)),
            ((n // seq_rows, MLA_ROPE, seq_rows), F32,
             pl.BlockSpec((1, MLA_ROPE, tile), lambda i: (i // table_tiles, 0, i % table_tiles))),
            ((n, _RET_W), F32, row(_RET_W)),
            ((n, _RET_W), F32, row(_RET_W)),
            ((n, _RET_W), F32, row(_RET_W))]
    return pl.pallas_call(
        _front_kernel,
        grid=(n // tile,),
        in_specs=[row(D_MODEL)] + [tab] * 5 + [tab_t] * 2 + [_const_spec(a.shape) for a in weights],
        out_specs=[spec for _, _, spec in outs],
        out_shape=[jax.ShapeDtypeStruct(shape, dt) for shape, dt, _ in outs],
        compiler_params=pltpu.CompilerParams(dimension_semantics=("arbitrary",),
                                             vmem_limit_bytes=VMEM_LIMIT),
        name="front_stage",
    )(x, *tables, *weights)


def _attn_kernel(batch, qi_ref, ki_ref, *refs):
    qt_refs, k_refs, vt_refs = refs[:batch], refs[batch:2 * batch], refs[2 * batch:3 * batch]
    o_ref, m_sc, acc_sc = refs[3 * batch:]
    t = pl.program_id(0)
    qi = qi_ref[t]
    ki = ki_ref[t]
    units = [(b, hd) for b in range(batch) for hd in range(MLA_HEADS)]

    @pl.when(ki == 0)
    def _():
        m_sc[...] = jnp.full(m_sc.shape, NEG_BIG, F32)
        acc_sc[...] = jnp.zeros(acc_sc.shape, F32)

    def scores(unit):
        b, hd = unit
        kh = k_refs[b][:, hd * HEAD_PAD:(hd + 1) * HEAD_PAD]
        return _dot(kh, qt_refs[b][hd * HEAD_PAD:(hd + 1) * HEAD_PAD, :])

    def value_update(unit, alpha, p):
        b, hd = unit
        i = b * MLA_HEADS + hd
        acc_sc[i] = alpha * acc_sc[i] + _dot(vt_refs[b][hd * V_ROWS:(hd + 1) * V_ROWS, :], p)

    def update(masked):
        st_next = scores(units[0])
        pending = None
        for n, unit in enumerate(units):
            i = unit[0] * MLA_HEADS + unit[1]
            st = st_next
            if n + 1 < len(units):
                st_next = scores(units[n + 1])
            if masked:
                key = lax.broadcasted_iota(jnp.int32, st.shape, 0)
                qry = lax.broadcasted_iota(jnp.int32, st.shape, 1)
                st = jnp.where(key <= qry, st, NEG_BIG)
            m_prev = m_sc[i]
            m_new = jnp.maximum(m_prev, jnp.max(st, axis=0, keepdims=True))
            alpha = jnp.exp2(m_prev - m_new)
            p = jnp.exp2(st - m_new).astype(BF16)
            m_sc[i] = m_new
            if pending is not None:
                value_update(*pending)
            pending = (unit, alpha, p)
        value_update(*pending)

    @pl.when(ki < qi)
    def _():
        update(False)

    @pl.when(ki == qi)
    def _():
        update(True)
        acc = acc_sc[...]
        o_t = (acc[:, :MLA_V, :] / acc[:, MLA_V:MLA_V + 1, :]).reshape(batch, _ATT_W, ATTN_TILE)
        for b in range(batch):
            o_ref[b] = jnp.transpose(o_t[b]).astype(o_ref.dtype)


def _prompt_attention(qt, k, vt, batch, seq):
    nt = seq // ATTN_TILE
    qi_list = np.array([i for i in range(nt) for _ in range(i + 1)], np.int32)
    ki_list = np.array([j for i in range(nt) for j in range(i + 1)], np.int32)
    qk_w = MLA_HEADS * HEAD_PAD
    seqs = range(batch)
    q_specs = [pl.BlockSpec((qk_w, ATTN_TILE), lambda s, qi, ki, b=b: (0, b * nt + qi[s])) for b in seqs]
    k_specs = [pl.BlockSpec((ATTN_TILE, qk_w), lambda s, qi, ki, b=b: (b * nt + ki[s], 0)) for b in seqs]
    v_specs = [pl.BlockSpec((_VT_ROWS, ATTN_TILE), lambda s, qi, ki, b=b: (0, b * nt + ki[s]))
               for b in seqs]
    grid_spec = pltpu.PrefetchScalarGridSpec(
        num_scalar_prefetch=2,
        grid=(len(qi_list),),
        in_specs=q_specs + k_specs + v_specs,
        out_specs=pl.BlockSpec((batch, ATTN_TILE, _ATT_W), lambda s, qi, ki: (0, qi[s], 0)),
        scratch_shapes=[pltpu.VMEM((batch * MLA_HEADS, 1, ATTN_TILE), F32),
                        pltpu.VMEM((batch * MLA_HEADS, V_ROWS, ATTN_TILE), F32)],
    )
    out = pl.pallas_call(
        functools.partial(_attn_kernel, batch),
        grid_spec=grid_spec,
        out_shape=jax.ShapeDtypeStruct((batch, seq, _ATT_W), BF16),
        compiler_params=pltpu.CompilerParams(dimension_semantics=("arbitrary",),
                                             vmem_limit_bytes=VMEM_LIMIT),
        name="prompt_attention",
    )(jnp.asarray(qi_list), jnp.asarray(ki_list), *([qt] * batch), *([k] * batch), *([vt] * batch))
    return out.reshape(batch * seq, _ATT_W)


def _head_layernorm(o, g):
    mu = jnp.mean(o, axis=-1, keepdims=True)
    d = o - mu
    var = jnp.mean(d * d, axis=-1, keepdims=True)
    return d * lax.rsqrt(var + EPS) * g


def _ret_prompt_kernel(q_ref, k_ref, v_ref, dec_ref, qd_ref, kd_ref, gc_ref, g_ref,
                       o_ref, s_out_ref, s_sc):
    c = pl.program_id(1)

    @pl.when(c == 0)
    def _():
        s_sc[...] = jnp.zeros(s_sc.shape, F32)

    for hd in range(RET_HEADS):
        sl = slice(hd * RET_DK, (hd + 1) * RET_DK)
        q = q_ref[0, :, sl]
        k = k_ref[0, :, sl]
        vb = v_ref[0, :, sl].astype(BF16)
        qb = q.astype(BF16)
        state = s_sc[hd]
        scores = _dot_nt(qb, k.astype(BF16)) * dec_ref[hd]
        inner = _dot(scores.astype(BF16), vb)
        cross = _dot(qb, state.astype(BF16)) * qd_ref[hd]
        o_ref[0, :, sl] = _head_layernorm(inner + cross, g_ref[hd:hd + 1, :])
        k_dec_t = jnp.transpose(k * kd_ref[hd]).astype(BF16)
        s_sc[hd] = gc_ref[hd:hd + 1, :] * state + _dot(k_dec_t, vb)

    @pl.when(c == pl.num_programs(1) - 1)
    def _():
        s_out_ref[0] = s_sc[...]


def _log_gamma():
    return np.log1p(-np.exp2(-5.0 - np.arange(RET_HEADS, dtype=np.float64)))


def _retention_consts(chunk):
    log_gamma = _log_gamma()
    idx = np.arange(chunk, dtype=np.float64)
    diff = idx[:, None] - idx[None, :]
    decay = np.where(diff[None] >= 0,
                     np.exp(log_gamma[:, None, None] * np.maximum(diff, 0.0)[None]), 0.0)
    q_decay = np.exp(log_gamma[:, None] * (idx + 1.0)[None, :])
    k_decay = np.exp(log_gamma[:, None] * (chunk - 1.0 - idx)[None, :])
    chunk_decay = np.exp(log_gamma * chunk)
    lane = lambda a: np.broadcast_to(a[..., None], a.shape + (LANES,))
    tables = (decay, lane(q_decay), lane(k_decay), lane(chunk_decay))
    return tuple(np.ascontiguousarray(t, dtype=np.float32) for t in tables)


def _prompt_retention(qr, kr, rv, ret_norm):
    b, t, _ = qr.shape
    decay, q_decay, k_decay, chunk_decay = _retention_consts(RET_CHUNK)
    seq = pl.BlockSpec((1, RET_CHUNK, _RET_W), lambda bb, c: (bb, c, 0))
    return pl.pallas_call(
        _ret_prompt_kernel,
        grid=(b, t // RET_CHUNK),
        in_specs=[seq, seq, seq, _const_spec(decay.shape), _const_spec(q_decay.shape),
                  _const_spec(k_decay.shape), _const_spec(chunk_decay.shape),
                  _const_spec(ret_norm.shape)],
        out_specs=[seq, pl.BlockSpec((1, RET_HEADS, RET_DK, RET_DV), lambda bb, c: (bb, 0, 0, 0))],
        out_shape=[jax.ShapeDtypeStruct((b, t, _RET_W), F32),
                   jax.ShapeDtypeStruct((b, RET_HEADS, RET_DK, RET_DV), F32)],
        scratch_shapes=[pltpu.VMEM((RET_HEADS, RET_DK, RET_DV), F32)],
        compiler_params=pltpu.CompilerParams(dimension_semantics=("arbitrary", "arbitrary")),
        name="prompt_retention",
    )(qr, kr, rv, decay, q_decay, k_decay, chunk_decay, ret_norm)


def _ret_sample_kernel(q_ref, k_ref, v_ref, s_ref, gam_ref, g_ref, o_ref, s_out_ref):
    rows = lax.broadcasted_iota(jnp.int32, (RET_DK, RET_DK), 0)
    cols = lax.broadcasted_iota(jnp.int32, (RET_DK, RET_DK), 1)
    eye = rows == cols

    def column(r):
        return jnp.sum(jnp.where(eye, jnp.broadcast_to(r, (RET_DK, RET_DK)), 0.0),
                       axis=1, keepdims=True)

    heads = range(RET_HEADS)
    lanes = [slice(hd * RET_DK, (hd + 1) * RET_DK) for hd in heads]
    gams = [gam_ref[hd:hd + 1, :] for hd in heads]
    for i in range(RET_SEQS_PER_STEP):
        qs = [q_ref[i, :, sl] for sl in lanes]
        ks = [k_ref[i, :, sl] for sl in lanes]
        vs = [v_ref[i, :, sl] for sl in lanes]
        k_cols = [column(k) for k in ks]
        qk = [jnp.sum(q * k, axis=-1, keepdims=True) for q, k in zip(qs, ks)]
        states = [s_ref[0, i, hd] for hd in heads]
        cross = [_dot(jnp.broadcast_to(qs[hd], (8, RET_DK)), states[hd])[0:1] for hd in heads]
        outs = [qk[hd] * vs[hd] + cross[hd] * gams[hd] for hd in heads]
        for hd in heads:
            s_out_ref[0, i, hd] = gams[hd] * states[hd] + k_cols[hd] * vs[hd]
        mus = [jnp.mean(o, axis=-1, keepdims=True) for o in outs]
        devs = [o - mu for o, mu in zip(outs, mus)]
        variances = [jnp.mean(d * d, axis=-1, keepdims=True) for d in devs]
        for hd in heads:
            o_ref[i, :, lanes[hd]] = devs[hd] * lax.rsqrt(variances[hd] + EPS) * g_ref[hd:hd + 1, :]


def _sample_retention(qr, kr, rv, state, ret_norm):
    n = qr.shape[0]
    gam = np.ascontiguousarray(
        np.broadcast_to(np.exp(_log_gamma() * 1.0)[:, None], (RET_HEADS, LANES)), dtype=np.float32)
    g = RET_SEQS_PER_STEP
    tok = pl.BlockSpec((g, 1, _RET_W), lambda i: (i, 0, 0))
    st = pl.BlockSpec((1, g, RET_HEADS, RET_DK, RET_DV), lambda i: (0, i, 0, 0, 0))
    r3 = lambda a: a.reshape(n, 1, _RET_W)
    o, s_new = pl.pallas_call(
        _ret_sample_kernel,
        grid=(n // g,),
        in_specs=[tok, tok, tok, st, _const_spec(gam.shape), _const_spec(ret_norm.shape)],
        out_specs=[tok, st],
        out_shape=[jax.ShapeDtypeStruct((n, 1, _RET_W), F32),
                   jax.ShapeDtypeStruct(state.shape, F32)],
        compiler_params=pltpu.CompilerParams(dimension_semantics=("arbitrary",)),
        name="sample_retention",
    )(r3(qr), r3(kr), r3(rv), state, gam, ret_norm)
    return o.reshape(n, _RET_W), s_new


def _absorb_q_kernel(qt_ref, wt_ref, o_ref):
    for hd in range(MLA_HEADS):
        o_ref[hd] = _dot(wt_ref[hd], qt_ref[hd * HEAD_PAD:(hd + 1) * HEAD_PAD, :])


def _absorb_q(qt, w_absorb_t):
    n = qt.shape[1]
    return pl.pallas_call(
        _absorb_q_kernel,
        out_shape=jax.ShapeDtypeStruct((MLA_HEADS, KEY_W, n), F32),
        name="sample_absorb_q",
    )(qt, w_absorb_t)


def _unabsorb_kernel(o_ref, w_ref, out_ref):
    acc = _dot(o_ref[0].astype(BF16), w_ref[0])
    for hd in range(1, MLA_HEADS):
        acc = acc + _dot(o_ref[hd].astype(BF16), w_ref[hd])
    out_ref[...] = acc.astype(out_ref.dtype)


def _unabsorb(o_lat, w_unabsorb):
    n = o_lat.shape[1]
    return pl.pallas_call(
        _unabsorb_kernel,
        out_shape=jax.ShapeDtypeStruct((n, _ATT_W), BF16),
        name="sample_unabsorb",
    )(o_lat, w_unabsorb)


def _paged_kernel(pt_ref, q_ref, knew_ref, ckv_hbm, kpet_hbm, o_ref, buf, kpe_buf, sem):
    b = pl.program_id(0)
    n_seq = pl.num_programs(0)
    n_chunks = pt_ref.shape[1] // PAGES_PER_STEP
    assert n_chunks == PAGE_SLOTS
    ahead = PAGE_SLOTS - 1
    chunk_keys = PAGES_PER_STEP * PAGE_SIZE
    group_keys = chunk_keys // PAGE_GROUPS

    def page_copies(bb, cc, sl):
        out = []
        for j in range(PAGES_PER_STEP):
            pg = pt_ref[bb, cc * PAGES_PER_STEP + j]
            keys = pl.ds(j * PAGE_SIZE, PAGE_SIZE)
            out.append(pltpu.make_async_copy(ckv_hbm.at[0, pg], buf.at[sl, keys], sem.at[sl, 0]))
            out.append(pltpu.make_async_copy(kpet_hbm.at[0, pg], kpe_buf.at[sl, :, keys],
                                             sem.at[sl, 1]))
        return out

    @pl.when(b == 0)
    def _():
        for c in range(ahead):
            for cp in page_copies(0, c, c):
                cp.start()

    b_next = jnp.minimum(b + 1, n_seq - 1)

    q = q_ref[0]
    q_lat = q[:, 0:KV_LORA]
    q_pe = q[:, KV_LORA:KV_LORA + MLA_ROPE]

    def scores(sl, g):
        keys = slice(g * group_keys, (g + 1) * group_keys)
        return _dot_nt(q_lat, buf[sl, keys, :]) + _dot(q_pe, kpe_buf[sl, :, keys])

    m_run = jnp.full((MLA_HEADS, 1), NEG_BIG, F32)
    l_run = jnp.zeros((MLA_HEADS, 1), F32)
    acc_run = jnp.zeros((MLA_HEADS, KV_LORA), F32)
    def wait_chunk(c):
        pltpu.make_async_copy(buf.at[c], buf.at[c], sem.at[c, 0]).wait()
        pltpu.make_async_copy(kpe_buf.at[c], kpe_buf.at[c], sem.at[c, 1]).wait()

    wait_chunk(0)
    s_next = scores(0, 0)
    for c in range(n_chunks):
        sl = c
        nxt = c + ahead
        prefetch = (page_copies(b, nxt, nxt) if nxt < n_chunks
                    else page_copies(b_next, nxt - n_chunks, nxt - n_chunks))
        parts = []
        per_group = len(prefetch) // PAGE_GROUPS
        for g in range(PAGE_GROUPS):
            s = s_next
            if g + 1 < PAGE_GROUPS:
                s_next = scores(sl, g + 1)
            elif c + 1 < n_chunks:
                wait_chunk(c + 1)
                s_next = scores(c + 1, 0)
            for cp in prefetch[g * per_group:(g + 1) * per_group]:
                cp.start()
            m_g = jnp.max(s, axis=-1, keepdims=True)
            p = jnp.exp2(s - m_g)
            l_g = jnp.sum(p, axis=-1, keepdims=True)
            keys = slice(g * group_keys, (g + 1) * group_keys)
            parts.append((m_g, l_g, _dot(p, buf[sl, keys, :])))

        m_new = m_run
        for m_g, _, _ in parts:
            m_new = jnp.maximum(m_new, m_g)
        a_run = jnp.exp2(m_run - m_new)
        l_run = a_run * l_run
        acc_run = a_run * acc_run
        for m_g, l_g, o_g in parts:
            a_g = jnp.exp2(m_g - m_new)
            l_run = l_run + a_g * l_g
            acc_run = acc_run + a_g * o_g
        m_run = m_new

    kn = knew_ref[0]
    s_self = jnp.sum(q * kn, axis=-1, keepdims=True)
    m_fin = jnp.maximum(m_run, s_self)
    a = jnp.exp2(m_run - m_fin)
    p_self = jnp.exp2(s_self - m_fin)
    o_ref[0] = (a * acc_run + p_self * kn[:, 0:KV_LORA]) / (a * l_run + p_self)

    @pl.when(b == n_seq - 1)
    def _():
        for c in range(ahead):
            wait_chunk(c)


def _paged_attention(page_table, q_abs, k_new, cache_ckv, cache_kpe_t):
    n, n_pages = page_table.shape
    chunk_keys = PAGES_PER_STEP * PAGE_SIZE
    grid_spec = pltpu.PrefetchScalarGridSpec(
        num_scalar_prefetch=1,
        grid=(n,),
        in_specs=[
            pl.BlockSpec((1, MLA_HEADS, KEY_W), lambda b, pt: (b, 0, 0)),
            pl.BlockSpec((1, 1, KEY_W), lambda b, pt: (b, 0, 0)),
            pl.BlockSpec(memory_space=pl.ANY),
            pl.BlockSpec(memory_space=pl.ANY),
        ],
        out_specs=pl.BlockSpec((1, MLA_HEADS, KV_LORA), lambda b, pt: (b, 0, 0)),
        scratch_shapes=[pltpu.VMEM((PAGE_SLOTS, chunk_keys, KV_LORA), F32),
                        pltpu.VMEM((PAGE_SLOTS, MLA_ROPE, chunk_keys), F32),
                        pltpu.SemaphoreType.DMA((PAGE_SLOTS, 2))],
    )
    return pl.pallas_call(
        _paged_kernel,
        grid_spec=grid_spec,
        out_shape=jax.ShapeDtypeStruct((n, MLA_HEADS, KV_LORA), F32),
        compiler_params=pltpu.CompilerParams(dimension_semantics=("arbitrary",),
                                             vmem_limit_bytes=VMEM_LIMIT),
        name="sample_paged_attention",
    )(page_table, q_abs, k_new, cache_ckv, cache_kpe_t)


def _back_kernel(h_ref, oatt_ref, oret_ref, p_ref,
                 gmix_ref, wgate_ref, wba_ref, wbr_ref, wout_ref,
                 g2_ref, wg_ref, wu_ref, wd_ref, gple_ref, wpg_ref, wpp_ref, gfin_ref, y_ref):
    h = h_ref[...]
    un = _rms(h, gmix_ref[...]).astype(BF16)
    gates = _dot_nt(un, wgate_ref[...])
    rg = gates[:, :_RET_W]
    ga = gates[:, _RET_W:_RET_W + D_MODEL]
    gr = gates[:, _RET_W + D_MODEL:]
    o_ret = (rg * jax.nn.sigmoid(rg) * oret_ref[...]).astype(BF16)
    merged = (jax.nn.sigmoid(ga) * _dot(oatt_ref[...], wba_ref[...])
              + jax.nn.sigmoid(gr) * _dot(o_ret, wbr_ref[...]))
    h = h + _dot(merged.astype(BF16), wout_ref[...])
    h = _swiglu_half(h, g2_ref[...], wg_ref[...], wu_ref[...], wd_ref[...])
    gate = jax.nn.sigmoid(_dot(_rms(h, gple_ref[...]).astype(BF16), wpg_ref[...]))
    h = h + gate * _dot(p_ref[...].astype(BF16), wpp_ref[...])
    y_ref[...] = _rms(h, gfin_ref[...])


def _back_stage(h, o_att, o_ret, p_emb, w, tile):
    n = h.shape[0]
    row = lambda width: pl.BlockSpec((tile, width), lambda i: (i, 0))
    weights = [w['mix_norm'], w['w_in_gates_t'], w['w_branch_att'], w['w_branch_ret'], w['w_out'],
               w['ffn2_norm'], w['ffn2_w_gate'], w['ffn2_w_up'], w['ffn2_w_down'],
               w['ple_norm'], w['w_ple_gate'], w['w_ple_proj'], w['final_norm']]
    return pl.pallas_call(
        _back_kernel,
        grid=(n // tile,),
        in_specs=[row(D_MODEL), row(_ATT_W), row(_RET_W), row(PLE_DIM)]
        + [_const_spec(a.shape) for a in weights],
        out_specs=row(D_MODEL),
        out_shape=jax.ShapeDtypeStruct((n, D_MODEL), F32),
        compiler_params=pltpu.CompilerParams(dimension_semantics=("arbitrary",),
                                             vmem_limit_bytes=VMEM_LIMIT),
        name="back_stage",
    )(h, o_att, o_ret, p_emb, *weights)


def _rope_tables(pos):
    pos = np.asarray(pos, np.float64)[:, None]
    n = pos.shape[0]
    inv_m = ROPE_THETA ** (-np.arange(ROPE_HALF, dtype=np.float64) / ROPE_HALF)
    cos_m, sin_m = np.cos(pos * inv_m[None, :]), np.sin(pos * inv_m[None, :])
    z = lambda width: np.zeros((n, width))
    tail = HEAD_PAD - ROPE_OFF - MLA_ROPE
    cm = np.concatenate([np.ones((n, ROPE_OFF)), cos_m, cos_m, z(tail)], axis=1)
    s1 = np.concatenate([z(ROPE_OFF), -sin_m, z(ROPE_HALF), z(tail)], axis=1)
    s2 = np.concatenate([z(ROPE_OFF), z(ROPE_HALF), sin_m, z(tail)], axis=1)
    half = RET_DK // 2
    inv_r = ROPE_THETA ** (-np.arange(half, dtype=np.float64) / half)
    cos_r, sin_r = np.cos(pos * inv_r[None, :]), np.sin(pos * inv_r[None, :])
    cr = np.concatenate([cos_r, cos_r], axis=1)
    sr = np.concatenate([-sin_r, sin_r], axis=1)
    tables = (cm, s1, s2, cr, sr, cos_m.T, sin_m.T)
    return tuple(np.ascontiguousarray(t, dtype=np.float32) for t in tables)


def _layer_weights(i, ffn1_norm, ffn1_w_gate, ffn1_w_up, ffn1_w_down, mix_norm, w_in, q_a_norm, w_q_b,
                   kv_a_norm, w_kv_b, ret_norm, w_branch_att, w_branch_ret, w_out,
                   ffn2_norm, ffn2_w_gate, ffn2_w_up, ffn2_w_down, ple_norm, w_ple_gate, w_ple_proj,
                   final_norm):
    vec = lambda a: a.reshape(1, -1)
    bf = lambda a: a.astype(BF16)
    win_t = jnp.transpose(w_in[i])
    zrows = lambda height: jnp.zeros((height, D_MODEL), F32)
    w_in_front_t = jnp.concatenate(
        [win_t[:_OFF_KPE], zrows(ROPE_OFF), win_t[_OFF_KPE:_OFF_RQ],
         zrows(HEAD_PAD - ROPE_OFF - MLA_ROPE), win_t[_OFF_RQ:_OFF_RG]], axis=0)
    wqb = w_q_b[i].reshape(Q_LORA, MLA_HEADS, MLA_NOPE + MLA_ROPE)
    wqb = jnp.pad(wqb, ((0, 0), (0, 0), (0, HEAD_PAD - MLA_NOPE - MLA_ROPE)))
    wkv = w_kv_b[i].reshape(KV_LORA, MLA_HEADS, MLA_NOPE + MLA_V)
    w_uk, w_uv = wkv[..., :MLA_NOPE], wkv[..., MLA_NOPE:]
    w_k = jnp.pad(w_uk, ((0, 0), (0, 0), (0, HEAD_PAD - MLA_NOPE)))
    pass_rope = np.zeros((1, KEY_W, HEAD_PAD), np.float32)
    pass_rope[0, KV_LORA:KV_LORA + MLA_ROPE, MLA_NOPE:MLA_NOPE + MLA_ROPE] = np.eye(MLA_ROPE)
    w_absorb_t = jnp.pad(jnp.transpose(w_uk, (1, 0, 2)),
                         ((0, 0), (0, KEY_W - KV_LORA), (0, HEAD_PAD - MLA_NOPE))) + pass_rope
    own_cols = np.eye(MLA_HEADS, dtype=np.float32)[:, None, :, None]
    w_unabsorb = (own_cols * jnp.transpose(w_uv, (1, 0, 2))[:, :, None, :]).reshape(
        MLA_HEADS, KV_LORA, _ATT_W)
    return {
        'ffn1_norm': vec(ffn1_norm[i]), 'ffn1_w_gate': bf(ffn1_w_gate[i]), 'ffn1_w_up': bf(ffn1_w_up[i]),
        'ffn1_w_down': bf(ffn1_w_down[i]), 'mix_norm': vec(mix_norm[i]),
        'w_in_front_t': bf(w_in_front_t), 'w_in_gates_t': bf(win_t[_OFF_RG:]),
        'q_a_norm': vec(q_a_norm[i]),
        'w_q_bt': bf(jnp.transpose(wqb.reshape(Q_LORA, MLA_HEADS * HEAD_PAD))),
        'kv_a_norm': vec(kv_a_norm[i]), 'w_k': bf(w_k.reshape(KV_LORA, MLA_HEADS * HEAD_PAD)),
        'w_vt': bf(jnp.pad(jnp.transpose(w_uv, (1, 2, 0)), ((0, 0), (0, V_ROWS - MLA_V), (0, 0)))
                   .reshape(_VT_ROWS, KV_LORA)),
        'w_absorb_t': bf(w_absorb_t), 'w_unabsorb': bf(w_unabsorb),
        'ret_norm': ret_norm[i],
        'w_branch_att': bf(w_branch_att[i]), 'w_branch_ret': bf(w_branch_ret[i]), 'w_out': bf(w_out[i]),
        'ffn2_norm': vec(ffn2_norm[i]), 'ffn2_w_gate': bf(ffn2_w_gate[i]), 'ffn2_w_up': bf(ffn2_w_up[i]),
        'ffn2_w_down': bf(ffn2_w_down[i]), 'ple_norm': vec(ple_norm[i]),
        'w_ple_gate': bf(w_ple_gate[i]), 'w_ple_proj': bf(w_ple_proj[i]),
        'final_norm': vec(final_norm),
    }


def kernel(x_prompt, x_sample, cache_ckv, cache_kpe, state_ret, page_table, p_prompt, p_sample, ffn1_norm, ffn1_w_gate, ffn1_w_up, ffn1_w_down, mix_norm, w_in, q_a_norm, w_q_b, kv_a_norm, w_kv_b, ret_norm, w_branch_att, w_branch_ret, w_out, ffn2_norm, ffn2_w_gate, ffn2_w_up, ffn2_w_down, ple_norm, w_ple_gate, w_ple_proj, final_norm):
    batch, seq, _ = x_prompt.shape
    n_dec, dec_seq, _ = x_sample.shape
    depth = w_in.shape[0]
    assert dec_seq == 1 and depth == 1
    n_past = page_table.shape[1] * PAGE_SIZE

    w = _layer_weights(0, ffn1_norm, ffn1_w_gate, ffn1_w_up, ffn1_w_down, mix_norm, w_in, q_a_norm,
                       w_q_b, kv_a_norm, w_kv_b, ret_norm, w_branch_att, w_branch_ret, w_out,
                       ffn2_norm, ffn2_w_gate, ffn2_w_up, ffn2_w_down, ple_norm, w_ple_gate,
                       w_ple_proj, final_norm)

    tabs_p = _rope_tables(np.arange(seq))
    (h_p, qt_p, k_p, vt_p, ckv_p, kpet_p, qr_p, kr_p, rv_p) = _front_stage(
        x_prompt.reshape(batch * seq, D_MODEL), tabs_p, w, TOKEN_TILE, seq // TOKEN_TILE)
    bt = lambda a: a.reshape(batch, seq, a.shape[-1])
    o_att_p = _prompt_attention(qt_p, k_p, vt_p, batch, seq)
    o_ret_p, ret_p = _prompt_retention(bt(qr_p), bt(kr_p), bt(rv_p), w['ret_norm'])
    y_p = _back_stage(h_p, o_att_p, o_ret_p.reshape(batch * seq, -1),
                      p_prompt.reshape(batch * seq, PLE_DIM), w, TOKEN_TILE)

    tabs_1 = _rope_tables(np.full((1,), n_past))
    tabs_s = (tuple(np.ascontiguousarray(np.broadcast_to(t, (n_dec, LANES))) for t in tabs_1[:5])
              + tuple(np.ascontiguousarray(np.broadcast_to(t, (ROPE_HALF, n_dec))) for t in tabs_1[5:]))
    (h_s, qt_s, _, _, ckv_s, kpet_s, qr_s, kr_s, rv_s) = _front_stage(
        x_sample.reshape(n_dec, D_MODEL), tabs_s, w, n_dec, 1)
    q_abs = jnp.transpose(_absorb_q(qt_s, w['w_absorb_t']), (2, 0, 1))
    kpe_s = jnp.transpose(kpet_s[0])
    k_new = jnp.concatenate([ckv_s, kpe_s,
                             jnp.zeros((n_dec, KEY_W - KV_LORA - MLA_ROPE), F32)], axis=1)
    o_lat = _paged_attention(page_table, q_abs, k_new.reshape(n_dec, 1, KEY_W), cache_ckv,
                             jnp.swapaxes(cache_kpe, 2, 3))
    o_att_s = _unabsorb(jnp.transpose(o_lat, (1, 0, 2)), w['w_unabsorb'])
    o_ret_s, ret_s = _sample_retention(qr_s, kr_s, rv_s, state_ret, w['ret_norm'])
    y_s = _back_stage(h_s, o_att_s, o_ret_s, p_sample.reshape(n_dec, PLE_DIM), w, n_dec)

    return (y_p.reshape(batch, seq, D_MODEL),
            y_s.reshape(n_dec, 1, D_MODEL),
            ckv_p.reshape(1, batch, seq, KV_LORA),
            jnp.transpose(kpet_p, (0, 2, 1))[None],
            ret_p[None],
            ckv_s.reshape(1, n_dec, 1, KV_LORA),
            kpe_s.reshape(1, n_dec, 1, MLA_ROPE),
            ret_s)
```

```python
import functools

import jax
import jax.numpy as jnp
import numpy as np
from jax import lax
from jax.experimental import pallas as pl
from jax.experimental.pallas import tpu as pltpu

F32 = jnp.float32
BF16 = jnp.bfloat16

D_MODEL = 1024
D_FF = 2816
PLE_DIM = 256
MLA_HEADS = 8
MLA_NOPE = 64
MLA_ROPE = 32
MLA_V = 64
Q_LORA = 256
KV_LORA = 128
RET_HEADS = 4
RET_DK = 128
RET_DV = 128
PAGE_SIZE = 128
ROPE_THETA = 10000.0
EPS = 1e-6

LANES = 128
HEAD_PAD = LANES
ROPE_OFF = MLA_NOPE
ROPE_HALF = MLA_ROPE // 2
SOFTMAX_SCALE = (MLA_NOPE + MLA_ROPE) ** -0.5
LOG2E = 1.4426950408889634
Q_SCALE = SOFTMAX_SCALE * LOG2E
NEG_BIG = -1e30

TOKEN_TILE = 512
ATTN_TILE = 512
RET_CHUNK = 512
PAGES_PER_STEP = 32
PAGE_GROUPS = 4
PAGE_SLOTS = 4
RET_SEQS_PER_STEP = 8
KEY_W = 2 * LANES
VMEM_LIMIT = 60 * 1024 * 1024

_OFF_CQ = 0
_OFF_CKV = _OFF_CQ + Q_LORA
_OFF_KPE = _OFF_CKV + KV_LORA
_OFF_RQ = _OFF_KPE + MLA_ROPE
_OFF_RK = _OFF_RQ + RET_HEADS * RET_DK
_OFF_RV = _OFF_RK + RET_HEADS * RET_DK
_OFF_RG = _OFF_RV + RET_HEADS * RET_DV
_RET_W = RET_HEADS * RET_DK
_ATT_W = MLA_HEADS * MLA_V
BF16_TILE_ROWS = 16
V_ROWS = MLA_V + BF16_TILE_ROWS
_VT_ROWS = MLA_HEADS * V_ROWS


def _rms(x, g):
    return x * lax.rsqrt(jnp.mean(x * x, axis=-1, keepdims=True) + EPS) * g


def _dot(a, b):
    return jnp.dot(a, b, preferred_element_type=F32)


def _dot_nt(a, b):
    return lax.dot_general(a, b, (((1,), (1,)), ((), ())), preferred_element_type=F32)


def _swiglu_half(x, g, wg, wu, wd):
    xn = _rms(x, g).astype(BF16)
    gate = _dot(xn, wg)
    up = _dot(xn, wu)
    act = (gate * jax.nn.sigmoid(gate) * up).astype(BF16)
    return x + 0.5 * _dot(act, wd)


def _const_spec(shape, index=None):
    index = (0,) * len(shape) if index is None else index
    return pl.BlockSpec(shape, lambda *_: index, pipeline_mode=pl.Buffered(1))


def _front_kernel(x_ref, cm_ref, s1_ref, s2_ref, cr_ref, sr_ref, ct_ref, st_ref,
                  g1_ref, wg_ref, wu_ref, wd_ref, gmix_ref, win_ref,
                  gqa_ref, wqbt_ref, gkv_ref, wk_ref, wvt_ref,
                  h_ref, qt_ref, k_ref, vt_ref, ckv_ref, kpet_ref, qr_ref, kr_ref, rv_ref):
    x = x_ref[...]
    h = _swiglu_half(x, g1_ref[...], wg_ref[...], wu_ref[...], wd_ref[...])
    h_ref[...] = h
    un = _rms(h, gmix_ref[...]).astype(BF16)
    z = _dot_nt(un, win_ref[...])

    cm, s1, s2 = cm_ref[...], s1_ref[...], s2_ref[...]

    def mla_rope(t):
        return (t * cm + pltpu.roll(t, LANES - ROPE_HALF, 1) * s1
                + pltpu.roll(t, ROPE_HALF, 1) * s2)

    cq = z[:, _OFF_CQ:_OFF_CQ + Q_LORA]
    qt = _dot_nt(wqbt_ref[...], _rms(cq, gqa_ref[...]).astype(BF16))
    ct, st = ct_ref[...], st_ref[...]
    c_kv = _rms(z[:, Q_LORA:Q_LORA + KV_LORA], gkv_ref[...])
    ckv_ref[...] = c_kv
    c_kv_b = c_kv.astype(BF16)
    k_nope = _dot(c_kv_b, wk_ref[...])
    vt = _dot_nt(wvt_ref[...], c_kv_b)
    sum_row = lax.broadcasted_iota(jnp.int32, vt.shape, 0) % V_ROWS >= MLA_V
    vt_ref[...] = jnp.where(sum_row, 1.0, vt).astype(BF16)
    k_pe = mla_rope(z[:, Q_LORA + KV_LORA:Q_LORA + KV_LORA + LANES])
    kpet_ref[0] = jnp.transpose(k_pe)[ROPE_OFF:ROPE_OFF + MLA_ROPE, :]
    for hd in range(MLA_HEADS):
        sl = slice(hd * HEAD_PAD, (hd + 1) * HEAD_PAD)
        k_ref[:, sl] = (k_nope[:, sl] + k_pe).astype(BF16)
        r0 = hd * HEAD_PAD + ROPE_OFF
        x1 = qt[r0:r0 + ROPE_HALF]
        x2 = qt[r0 + ROPE_HALF:r0 + MLA_ROPE]
        qt_ref[hd * HEAD_PAD:r0, :] = (qt[hd * HEAD_PAD:r0] * Q_SCALE).astype(BF16)
        qt_ref[r0:r0 + ROPE_HALF, :] = ((x1 * ct - x2 * st) * Q_SCALE).astype(BF16)
        qt_ref[r0 + ROPE_HALF:r0 + MLA_ROPE, :] = ((x1 * st + x2 * ct) * Q_SCALE).astype(BF16)
        qt_ref[r0 + MLA_ROPE:(hd + 1) * HEAD_PAD, :] = jnp.zeros(
            (HEAD_PAD - ROPE_OFF - MLA_ROPE, qt.shape[1]), BF16)

    cr, sr = cr_ref[...], sr_ref[...]
    base = Q_LORA + KV_LORA + LANES
    for hd in range(RET_HEADS):
        sl = slice(hd * RET_DK, (hd + 1) * RET_DK)
        rq = z[:, base + hd * RET_DK:base + (hd + 1) * RET_DK]
        rk = z[:, base + _RET_W + hd * RET_DK:base + _RET_W + (hd + 1) * RET_DK]
        qr_ref[:, sl] = rq * cr + pltpu.roll(rq, RET_DK // 2, 1) * sr
        kr_ref[:, sl] = (rk * cr + pltpu.roll(rk, RET_DK // 2, 1) * sr) * (RET_DK ** -0.5)
    rv_ref[...] = z[:, base + 2 * _RET_W:base + 3 * _RET_W]


def _front_stage(x, tables, w, tile, table_tiles):
    n = x.shape[0]
    seq_rows = tile * table_tiles
    row = lambda width: pl.BlockSpec((tile, width), lambda i: (i, 0))
    col = lambda height: pl.BlockSpec((height, tile), lambda i: (0, i))
    tab = pl.BlockSpec((tile, LANES), lambda i: (i % table_tiles, 0))
    tab_t = pl.BlockSpec((ROPE_HALF, tile), lambda i: (0, i % table_tiles))
    weights = [w['ffn1_norm'], w['ffn1_w_gate'], w['ffn1_w_up'], w['ffn1_w_down'], w['mix_norm'],
               w['w_in_front_t'], w['q_a_norm'], w['w_q_bt'], w['kv_a_norm'], w['w_k'], w['w_vt']]
    outs = [((n, D_MODEL), F32, row(D_MODEL)),
            ((MLA_HEADS * HEAD_PAD, n), BF16, col(MLA_HEADS * HEAD_PAD)),
            ((n, MLA_HEADS * HEAD_PAD), BF16, row(MLA_HEADS * HEAD_PAD)),
            ((_VT_ROWS, n), BF16, col(_VT_ROWS)),
            ((n, KV_LORA), F32, row(KV_LORA)),
            ((n // seq_rows, MLA_ROPE, seq_rows), F32,
             pl.BlockSpec((1, MLA_ROPE, tile), lambda i: (i // table_tiles, 0, i % table_tiles))),
            ((n, _RET_W), F32, row(_RET_W)),
            ((n, _RET_W), F32, row(_RET_W)),
            ((n, _RET_W), F32, row(_RET_W))]
    return pl.pallas_call(
        _front_kernel,
        grid=(n // tile,),
        in_specs=[row(D_MODEL)] + [tab] * 5 + [tab_t] * 2 + [_const_spec(a.shape) for a in weights],
        out_specs=[spec for _, _, spec in outs],
        out_shape=[jax.ShapeDtypeStruct(shape, dt) for shape, dt, _ in outs],
        compiler_params=pltpu.CompilerParams(dimension_semantics=("arbitrary",),
                                             vmem_limit_bytes=VMEM_LIMIT),
        name="front_stage",
    )(x, *tables, *weights)


def _attn_kernel(batch, qi_ref, ki_ref, *refs):
    qt_refs, k_refs, vt_refs = refs[:batch], refs[batch:2 * batch], refs[2 * batch:3 * batch]
    o_ref, m_sc, acc_sc = refs[3 * batch:]
    t = pl.program_id(0)
    qi = qi_ref[t]
    ki = ki_ref[t]
    units = [(b, hd) for b in range(batch) for hd in range(MLA_HEADS)]

    @pl.when(ki == 0)
    def _():
        m_sc[...] = jnp.full(m_sc.shape, NEG_BIG, F32)
        acc_sc[...] = jnp.zeros(acc_sc.shape, F32)

    def scores(unit):
        b, hd = unit
        kh = k_refs[b][:, hd * HEAD_PAD:(hd + 1) * HEAD_PAD]
        return _dot(kh, qt_refs[b][hd * HEAD_PAD:(hd + 1) * HEAD_PAD, :])

    def value_update(unit, alpha, p):
        b, hd = unit
        i = b * MLA_HEADS + hd
        acc_sc[i] = alpha * acc_sc[i] + _dot(vt_refs[b][hd * V_ROWS:(hd + 1) * V_ROWS, :], p)

    def update(masked):
        st_next = scores(units[0])
        pending = None
        for n, unit in enumerate(units):
            i = unit[0] * MLA_HEADS + unit[1]
            st = st_next
            if n + 1 < len(units):
                st_next = scores(units[n + 1])
            if masked:
                key = lax.broadcasted_iota(jnp.int32, st.shape, 0)
                qry = lax.broadcasted_iota(jnp.int32, st.shape, 1)
                st = jnp.where(key <= qry, st, NEG_BIG)
            m_prev = m_sc[i]
            m_new = jnp.maximum(m_prev, jnp.max(st, axis=0, keepdims=True))
            alpha = jnp.exp2(m_prev - m_new)
            p = jnp.exp2(st - m_new).astype(BF16)
            m_sc[i] = m_new
            if pending is not None:
                value_update(*pending)
            pending = (unit, alpha, p)
        value_update(*pending)

    @pl.when(ki < qi)
    def _():
        update(False)

    @pl.when(ki == qi)
    def _():
        update(True)
        acc = acc_sc[...]
        o_t = (acc[:, :MLA_V, :] / acc[:, MLA_V:MLA_V + 1, :]).reshape(batch, _ATT_W, ATTN_TILE)
        for b in range(batch):
            o_ref[b] = jnp.transpose(o_t[b]).astype(o_ref.dtype)


def _prompt_attention(qt, k, vt, batch, seq):
    nt = seq // ATTN_TILE
    qi_list = np.array([i for i in range(nt) for _ in range(i + 1)], np.int32)
    ki_list = np.array([j for i in range(nt) for j in range(i + 1)], np.int32)
    qk_w = MLA_HEADS * HEAD_PAD
    seqs = range(batch)
    q_specs = [pl.BlockSpec((qk_w, ATTN_TILE), lambda s, qi, ki, b=b: (0, b * nt + qi[s])) for b in seqs]
    k_specs = [pl.BlockSpec((ATTN_TILE, qk_w), lambda s, qi, ki, b=b: (b * nt + ki[s], 0)) for b in seqs]
    v_specs = [pl.BlockSpec((_VT_ROWS, ATTN_TILE), lambda s, qi, ki, b=b: (0, b * nt + ki[s]))
               for b in seqs]
    grid_spec = pltpu.PrefetchScalarGridSpec(
        num_scalar_prefetch=2,
        grid=(len(qi_list),),
        in_specs=q_specs + k_specs + v_specs,
        out_specs=pl.BlockSpec((batch, ATTN_TILE, _ATT_W), lambda s, qi, ki: (0, qi[s], 0)),
        scratch_shapes=[pltpu.VMEM((batch * MLA_HEADS, 1, ATTN_TILE), F32),
                        pltpu.VMEM((batch * MLA_HEADS, V_ROWS, ATTN_TILE), F32)],
    )
    out = pl.pallas_call(
        functools.partial(_attn_kernel, batch),
        grid_spec=grid_spec,
        out_shape=jax.ShapeDtypeStruct((batch, seq, _ATT_W), BF16),
        compiler_params=pltpu.CompilerParams(dimension_semantics=("arbitrary",),
                                             vmem_limit_bytes=VMEM_LIMIT),
        name="prompt_attention",
    )(jnp.asarray(qi_list), jnp.asarray(ki_list), *([qt] * batch), *([k] * batch), *([vt] * batch))
    return out.reshape(batch * seq, _ATT_W)


def _head_layernorm(o, g):
    mu = jnp.mean(o, axis=-1, keepdims=True)
    d = o - mu
    var = jnp.mean(d * d, axis=-1, keepdims=True)
    return d * lax.rsqrt(var + EPS) * g


def _ret_prompt_kernel(q_ref, k_ref, v_ref, dec_ref, qd_ref, kd_ref, gc_ref, g_ref,
                       o_ref, s_out_ref, s_sc):
    c = pl.program_id(1)

    @pl.when(c == 0)
    def _():
        s_sc[...] = jnp.zeros(s_sc.shape, F32)

    for hd in range(RET_HEADS):
        sl = slice(hd * RET_DK, (hd + 1) * RET_DK)
        q = q_ref[0, :, sl]
        k = k_ref[0, :, sl]
        vb = v_ref[0, :, sl].astype(BF16)
        qb = q.astype(BF16)
        state = s_sc[hd]
        scores = _dot_nt(qb, k.astype(BF16)) * dec_ref[hd]
        inner = _dot(scores.astype(BF16), vb)
        cross = _dot(qb, state.astype(BF16)) * qd_ref[hd]
        o_ref[0, :, sl] = _head_layernorm(inner + cross, g_ref[hd:hd + 1, :])
        k_dec_t = jnp.transpose(k * kd_ref[hd]).astype(BF16)
        s_sc[hd] = gc_ref[hd:hd + 1, :] * state + _dot(k_dec_t, vb)

    @pl.when(c == pl.num_programs(1) - 1)
    def _():
        s_out_ref[0] = s_sc[...]


def _log_gamma():
    return np.log1p(-np.exp2(-5.0 - np.arange(RET_HEADS, dtype=np.float64)))


def _retention_consts(chunk):
    log_gamma = _log_gamma()
    idx = np.arange(chunk, dtype=np.float64)
    diff = idx[:, None] - idx[None, :]
    decay = np.where(diff[None] >= 0,
                     np.exp(log_gamma[:, None, None] * np.maximum(diff, 0.0)[None]), 0.0)
    q_decay = np.exp(log_gamma[:, None] * (idx + 1.0)[None, :])
    k_decay = np.exp(log_gamma[:, None] * (chunk - 1.0 - idx)[None, :])
    chunk_decay = np.exp(log_gamma * chunk)
    lane = lambda a: np.broadcast_to(a[..., None], a.shape + (LANES,))
    tables = (decay, lane(q_decay), lane(k_decay), lane(chunk_decay))
    return tuple(np.ascontiguousarray(t, dtype=np.float32) for t in tables)


def _prompt_retention(qr, kr, rv, ret_norm):
    b, t, _ = qr.shape
    decay, q_decay, k_decay, chunk_decay = _retention_consts(RET_CHUNK)
    seq = pl.BlockSpec((1, RET_CHUNK, _RET_W), lambda bb, c: (bb, c, 0))
    return pl.pallas_call(
        _ret_prompt_kernel,
        grid=(b, t // RET_CHUNK),
        in_specs=[seq, seq, seq, _const_spec(decay.shape), _const_spec(q_decay.shape),
                  _const_spec(k_decay.shape), _const_spec(chunk_decay.shape),
                  _const_spec(ret_norm.shape)],
        out_specs=[seq, pl.BlockSpec((1, RET_HEADS, RET_DK, RET_DV), lambda bb, c: (bb, 0, 0, 0))],
        out_shape=[jax.ShapeDtypeStruct((b, t, _RET_W), F32),
                   jax.ShapeDtypeStruct((b, RET_HEADS, RET_DK, RET_DV), F32)],
        scratch_shapes=[pltpu.VMEM((RET_HEADS, RET_DK, RET_DV), F32)],
        compiler_params=pltpu.CompilerParams(dimension_semantics=("arbitrary", "arbitrary")),
        name="prompt_retention",
    )(qr, kr, rv, decay, q_decay, k_decay, chunk_decay, ret_norm)


def _ret_sample_kernel(q_ref, k_ref, v_ref, s_ref, gam_ref, g_ref, o_ref, s_out_ref):
    rows = lax.broadcasted_iota(jnp.int32, (RET_DK, RET_DK), 0)
    cols = lax.broadcasted_iota(jnp.int32, (RET_DK, RET_DK), 1)
    eye = rows == cols

    def column(r):
        return jnp.sum(jnp.where(eye, jnp.broadcast_to(r, (RET_DK, RET_DK)), 0.0),
                       axis=1, keepdims=True)

    heads = range(RET_HEADS)
    lanes = [slice(hd * RET_DK, (hd + 1) * RET_DK) for hd in heads]
    gams = [gam_ref[hd:hd + 1, :] for hd in heads]
    for i in range(RET_SEQS_PER_STEP):
        qs = [q_ref[i, :, sl] for sl in lanes]
        ks = [k_ref[i, :, sl] for sl in lanes]
        vs = [v_ref[i, :, sl] for sl in lanes]
        k_cols = [column(k) for k in ks]
        qk = [jnp.sum(q * k, axis=-1, keepdims=True) for q, k in zip(qs, ks)]
        states = [s_ref[0, i, hd] for hd in heads]
        cross = [_dot(jnp.broadcast_to(qs[hd], (8, RET_DK)), states[hd])[0:1] for hd in heads]
        outs = [qk[hd] * vs[hd] + cross[hd] * gams[hd] for hd in heads]
        for hd in heads:
            s_out_ref[0, i, hd] = gams[hd] * states[hd] + k_cols[hd] * vs[hd]
        mus = [jnp.mean(o, axis=-1, keepdims=True) for o in outs]
        devs = [o - mu for o, mu in zip(outs, mus)]
        variances = [jnp.mean(d * d, axis=-1, keepdims=True) for d in devs]
        for hd in heads:
            o_ref[i, :, lanes[hd]] = devs[hd] * lax.rsqrt(variances[hd] + EPS) * g_ref[hd:hd + 1, :]


def _sample_retention(qr, kr, rv, state, ret_norm):
    n = qr.shape[0]
    gam = np.ascontiguousarray(
        np.broadcast_to(np.exp(_log_gamma() * 1.0)[:, None], (RET_HEADS, LANES)), dtype=np.float32)
    g = RET_SEQS_PER_STEP
    tok = pl.BlockSpec((g, 1, _RET_W), lambda i: (i, 0, 0))
    st = pl.BlockSpec((1, g, RET_HEADS, RET_DK, RET_DV), lambda i: (0, i, 0, 0, 0))
    r3 = lambda a: a.reshape(n, 1, _RET_W)
    o, s_new = pl.pallas_call(
        _ret_sample_kernel,
        grid=(n // g,),
        in_specs=[tok, tok, tok, st, _const_spec(gam.shape), _const_spec(ret_norm.shape)],
        out_specs=[tok, st],
        out_shape=[jax.ShapeDtypeStruct((n, 1, _RET_W), F32),
                   jax.ShapeDtypeStruct(state.shape, F32)],
        compiler_params=pltpu.CompilerParams(dimension_semantics=("arbitrary",)),
        name="sample_retention",
    )(r3(qr), r3(kr), r3(rv), state, gam, ret_norm)
    return o.reshape(n, _RET_W), s_new


def _absorb_q_kernel(qt_ref, wt_ref, o_ref):
    for hd in range(MLA_HEADS):
        o_ref[hd] = _dot(wt_ref[hd], qt_ref[hd * HEAD_PAD:(hd + 1) * HEAD_PAD, :])


def _absorb_q(qt, w_absorb_t):
    n = qt.shape[1]
    return pl.pallas_call(
        _absorb_q_kernel,
        out_shape=jax.ShapeDtypeStruct((MLA_HEADS, KEY_W, n), F32),
        name="sample_absorb_q",
    )(qt, w_absorb_t)


def _unabsorb_kernel(o_ref, w_ref, out_ref):
    acc = _dot(o_ref[0].astype(BF16), w_ref[0])
    for hd in range(1, MLA_HEADS):
        acc = acc + _dot(o_ref[hd].astype(BF16), w_ref[hd])
    out_ref[...] = acc.astype(out_ref.dtype)


def _unabsorb(o_lat, w_unabsorb):
    n = o_lat.shape[1]
    return pl.pallas_call(
        _unabsorb_kernel,
        out_shape=jax.ShapeDtypeStruct((n, _ATT_W), BF16),
        name="sample_unabsorb",
    )(o_lat, w_unabsorb)


def _paged_kernel(pt_ref, q_ref, knew_ref, ckv_hbm, kpet_hbm, o_ref, buf, kpe_buf, sem):
    b = pl.program_id(0)
    n_seq = pl.num_programs(0)
    n_chunks = pt_ref.shape[1] // PAGES_PER_STEP
    assert n_chunks == PAGE_SLOTS
    ahead = PAGE_SLOTS - 1
    chunk_keys = PAGES_PER_STEP * PAGE_SIZE
    group_keys = chunk_keys // PAGE_GROUPS

    def page_copies(bb, cc, sl):
        out = []
        for j in range(PAGES_PER_STEP):
            pg = pt_ref[bb, cc * PAGES_PER_STEP + j]
            keys = pl.ds(j * PAGE_SIZE, PAGE_SIZE)
            out.append((pltpu.make_async_copy(ckv_hbm.at[0, pg], buf.at[sl, keys], sem.at[sl, 0]), 0))
            out.append((pltpu.make_async_copy(kpet_hbm.at[0, pg], kpe_buf.at[sl, :, keys],
                                              sem.at[sl, 1]), 1))
        return out

    def start_page_copy(item):
        copy, thread = item
        copy.start(priority=thread)

    @pl.when(b == 0)
    def _():
        for c in range(ahead):
            for cp in page_copies(0, c, c):
                start_page_copy(cp)

    b_next = jnp.minimum(b + 1, n_seq - 1)

    q = q_ref[0]
    q_lat = q[:, 0:KV_LORA]
    q_pe = q[:, KV_LORA:KV_LORA + MLA_ROPE]

    def scores(sl, g):
        keys = slice(g * group_keys, (g + 1) * group_keys)
        return _dot_nt(q_lat, buf[sl, keys, :]) + _dot(q_pe, kpe_buf[sl, :, keys])

    m_run = jnp.full((MLA_HEADS, 1), NEG_BIG, F32)
    l_run = jnp.zeros((MLA_HEADS, 1), F32)
    acc_run = jnp.zeros((MLA_HEADS, KV_LORA), F32)
    def wait_chunk(c):
        pltpu.make_async_copy(buf.at[c], buf.at[c], sem.at[c, 0]).wait()
        pltpu.make_async_copy(kpe_buf.at[c], kpe_buf.at[c], sem.at[c, 1]).wait()

    wait_chunk(0)
    s_next = scores(0, 0)
    for c in range(n_chunks):
        sl = c
        nxt = c + ahead
        prefetch = (page_copies(b, nxt, nxt) if nxt < n_chunks
                    else page_copies(b_next, nxt - n_chunks, nxt - n_chunks))
        parts = []
        per_group = len(prefetch) // PAGE_GROUPS
        for g in range(PAGE_GROUPS):
            s = s_next
            if g + 1 < PAGE_GROUPS:
                s_next = scores(sl, g + 1)
            elif c + 1 < n_chunks:
                wait_chunk(c + 1)
                s_next = scores(c + 1, 0)
            for cp in prefetch[g * per_group:(g + 1) * per_group]:
                start_page_copy(cp)
            m_g = jnp.max(s, axis=-1, keepdims=True)
            p = jnp.exp2(s - m_g)
            l_g = jnp.sum(p, axis=-1, keepdims=True)
            keys = slice(g * group_keys, (g + 1) * group_keys)
            parts.append((m_g, l_g, _dot(p, buf[sl, keys, :])))

        m_new = m_run
        for m_g, _, _ in parts:
            m_new = jnp.maximum(m_new, m_g)
        a_run = jnp.exp2(m_run - m_new)
        l_run = a_run * l_run
        acc_run = a_run * acc_run
        for m_g, l_g, o_g in parts:
            a_g = jnp.exp2(m_g - m_new)
            l_run = l_run + a_g * l_g
            acc_run = acc_run + a_g * o_g
        m_run = m_new

    kn = knew_ref[0]
    s_self = jnp.sum(q * kn, axis=-1, keepdims=True)
    m_fin = jnp.maximum(m_run, s_self)
    a = jnp.exp2(m_run - m_fin)
    p_self = jnp.exp2(s_self - m_fin)
    o_ref[0] = (a * acc_run + p_self * kn[:, 0:KV_LORA]) / (a * l_run + p_self)

    @pl.when(b == n_seq - 1)
    def _():
        for c in range(ahead):
            wait_chunk(c)


def _paged_attention(page_table, q_abs, k_new, cache_ckv, cache_kpe_t):
    n, n_pages = page_table.shape
    chunk_keys = PAGES_PER_STEP * PAGE_SIZE
    grid_spec = pltpu.PrefetchScalarGridSpec(
        num_scalar_prefetch=1,
        grid=(n,),
        in_specs=[
            pl.BlockSpec((1, MLA_HEADS, KEY_W), lambda b, pt: (b, 0, 0)),
            pl.BlockSpec((1, 1, KEY_W), lambda b, pt: (b, 0, 0)),
            pl.BlockSpec(memory_space=pl.ANY),
            pl.BlockSpec(memory_space=pl.ANY),
        ],
        out_specs=pl.BlockSpec((1, MLA_HEADS, KV_LORA), lambda b, pt: (b, 0, 0)),
        scratch_shapes=[pltpu.VMEM((PAGE_SLOTS, chunk_keys, KV_LORA), F32),
                        pltpu.VMEM((PAGE_SLOTS, MLA_ROPE, chunk_keys), F32),
                        pltpu.SemaphoreType.DMA((PAGE_SLOTS, 2))],
    )
    return pl.pallas_call(
        _paged_kernel,
        grid_spec=grid_spec,
        out_shape=jax.ShapeDtypeStruct((n, MLA_HEADS, KV_LORA), F32),
        compiler_params=pltpu.CompilerParams(dimension_semantics=("arbitrary",),
                                             vmem_limit_bytes=VMEM_LIMIT),
        name="sample_paged_attention",
    )(page_table, q_abs, k_new, cache_ckv, cache_kpe_t)


def _back_kernel(h_ref, oatt_ref, oret_ref, p_ref,
                 gmix_ref, wgate_ref, wba_ref, wbr_ref, wout_ref,
                 g2_ref, wg_ref, wu_ref, wd_ref, gple_ref, wpg_ref, wpp_ref, gfin_ref, y_ref):
    h = h_ref[...]
    un = _rms(h, gmix_ref[...]).astype(BF16)
    gates = _dot_nt(un, wgate_ref[...])
    rg = gates[:, :_RET_W]
    ga = gates[:, _RET_W:_RET_W + D_MODEL]
    gr = gates[:, _RET_W + D_MODEL:]
    o_ret = (rg * jax.nn.sigmoid(rg) * oret_ref[...]).astype(BF16)
    merged = (jax.nn.sigmoid(ga) * _dot(oatt_ref[...], wba_ref[...])
              + jax.nn.sigmoid(gr) * _dot(o_ret, wbr_ref[...]))
    h = h + _dot(merged.astype(BF16), wout_ref[...])
    h = _swiglu_half(h, g2_ref[...], wg_ref[...], wu_ref[...], wd_ref[...])
    gate = jax.nn.sigmoid(_dot(_rms(h, gple_ref[...]).astype(BF16), wpg_ref[...]))
    h = h + gate * _dot(p_ref[...].astype(BF16), wpp_ref[...])
    y_ref[...] = _rms(h, gfin_ref[...])


def _back_stage(h, o_att, o_ret, p_emb, w, tile):
    n = h.shape[0]
    row = lambda width: pl.BlockSpec((tile, width), lambda i: (i, 0))
    weights = [w['mix_norm'], w['w_in_gates_t'], w['w_branch_att'], w['w_branch_ret'], w['w_out'],
               w['ffn2_norm'], w['ffn2_w_gate'], w['ffn2_w_up'], w['ffn2_w_down'],
               w['ple_norm'], w['w_ple_gate'], w['w_ple_proj'], w['final_norm']]
    return pl.pallas_call(
        _back_kernel,
        grid=(n // tile,),
        in_specs=[row(D_MODEL), row(_ATT_W), row(_RET_W), row(PLE_DIM)]
        + [_const_spec(a.shape) for a in weights],
        out_specs=row(D_MODEL),
        out_shape=jax.ShapeDtypeStruct((n, D_MODEL), F32),
        compiler_params=pltpu.CompilerParams(dimension_semantics=("arbitrary",),
                                             vmem_limit_bytes=VMEM_LIMIT),
        name="back_stage",
    )(h, o_att, o_ret, p_emb, *weights)


def _rope_tables(pos):
    pos = np.asarray(pos, np.float64)[:, None]
    n = pos.shape[0]
    inv_m = ROPE_THETA ** (-np.arange(ROPE_HALF, dtype=np.float64) / ROPE_HALF)
    cos_m, sin_m = np.cos(pos * inv_m[None, :]), np.sin(pos * inv_m[None, :])
    z = lambda width: np.zeros((n, width))
    tail = HEAD_PAD - ROPE_OFF - MLA_ROPE
    cm = np.concatenate([np.ones((n, ROPE_OFF)), cos_m, cos_m, z(tail)], axis=1)
    s1 = np.concatenate([z(ROPE_OFF), -sin_m, z(ROPE_HALF), z(tail)], axis=1)
    s2 = np.concatenate([z(ROPE_OFF), z(ROPE_HALF), sin_m, z(tail)], axis=1)
    half = RET_DK // 2
    inv_r = ROPE_THETA ** (-np.arange(half, dtype=np.float64) / half)
    cos_r, sin_r = np.cos(pos * inv_r[None, :]), np.sin(pos * inv_r[None, :])
    cr = np.concatenate([cos_r, cos_r], axis=1)
    sr = np.concatenate([-sin_r, sin_r], axis=1)
    tables = (cm, s1, s2, cr, sr, cos_m.T, sin_m.T)
    return tuple(np.ascontiguousarray(t, dtype=np.float32) for t in tables)


def _layer_weights(i, ffn1_norm, ffn1_w_gate, ffn1_w_up, ffn1_w_down, mix_norm, w_in, q_a_norm, w_q_b,
                   kv_a_norm, w_kv_b, ret_norm, w_branch_att, w_branch_ret, w_out,
                   ffn2_norm, ffn2_w_gate, ffn2_w_up, ffn2_w_down, ple_norm, w_ple_gate, w_ple_proj,
                   final_norm):
    vec = lambda a: a.reshape(1, -1)
    bf = lambda a: a.astype(BF16)
    win_t = jnp.transpose(w_in[i])
    zrows = lambda height: jnp.zeros((height, D_MODEL), F32)
    w_in_front_t = jnp.concatenate(
        [win_t[:_OFF_KPE], zrows(ROPE_OFF), win_t[_OFF_KPE:_OFF_RQ],
         zrows(HEAD_PAD - ROPE_OFF - MLA_ROPE), win_t[_OFF_RQ:_OFF_RG]], axis=0)
    wqb = w_q_b[i].reshape(Q_LORA, MLA_HEADS, MLA_NOPE + MLA_ROPE)
    wqb = jnp.pad(wqb, ((0, 0), (0, 0), (0, HEAD_PAD - MLA_NOPE - MLA_ROPE)))
    wkv = w_kv_b[i].reshape(KV_LORA, MLA_HEADS, MLA_NOPE + MLA_V)
    w_uk, w_uv = wkv[..., :MLA_NOPE], wkv[..., MLA_NOPE:]
    w_k = jnp.pad(w_uk, ((0, 0), (0, 0), (0, HEAD_PAD - MLA_NOPE)))
    pass_rope = np.zeros((1, KEY_W, HEAD_PAD), np.float32)
    pass_rope[0, KV_LORA:KV_LORA + MLA_ROPE, MLA_NOPE:MLA_NOPE + MLA_ROPE] = np.eye(MLA_ROPE)
    w_absorb_t = jnp.pad(jnp.transpose(w_uk, (1, 0, 2)),
                         ((0, 0), (0, KEY_W - KV_LORA), (0, HEAD_PAD - MLA_NOPE))) + pass_rope
    own_cols = np.eye(MLA_HEADS, dtype=np.float32)[:, None, :, None]
    w_unabsorb = (own_cols * jnp.transpose(w_uv, (1, 0, 2))[:, :, None, :]).reshape(
        MLA_HEADS, KV_LORA, _ATT_W)
    return {
        'ffn1_norm': vec(ffn1_norm[i]), 'ffn1_w_gate': bf(ffn1_w_gate[i]), 'ffn1_w_up': bf(ffn1_w_up[i]),
        'ffn1_w_down': bf(ffn1_w_down[i]), 'mix_norm': vec(mix_norm[i]),
        'w_in_front_t': bf(w_in_front_t), 'w_in_gates_t': bf(win_t[_OFF_RG:]),
        'q_a_norm': vec(q_a_norm[i]),
        'w_q_bt': bf(jnp.transpose(wqb.reshape(Q_LORA, MLA_HEADS * HEAD_PAD))),
        'kv_a_norm': vec(kv_a_norm[i]), 'w_k': bf(w_k.reshape(KV_LORA, MLA_HEADS * HEAD_PAD)),
        'w_vt': bf(jnp.pad(jnp.transpose(w_uv, (1, 2, 0)), ((0, 0), (0, V_ROWS - MLA_V), (0, 0)))
                   .reshape(_VT_ROWS, KV_LORA)),
        'w_absorb_t': bf(w_absorb_t), 'w_unabsorb': bf(w_unabsorb),
        'ret_norm': ret_norm[i],
        'w_branch_att': bf(w_branch_att[i]), 'w_branch_ret': bf(w_branch_ret[i]), 'w_out': bf(w_out[i]),
        'ffn2_norm': vec(ffn2_norm[i]), 'ffn2_w_gate': bf(ffn2_w_gate[i]), 'ffn2_w_up': bf(ffn2_w_up[i]),
        'ffn2_w_down': bf(ffn2_w_down[i]), 'ple_norm': vec(ple_norm[i]),
        'w_ple_gate': bf(w_ple_gate[i]), 'w_ple_proj': bf(w_ple_proj[i]),
        'final_norm': vec(final_norm),
    }


def kernel(x_prompt, x_sample, cache_ckv, cache_kpe, state_ret, page_table, p_prompt, p_sample, ffn1_norm, ffn1_w_gate, ffn1_w_up, ffn1_w_down, mix_norm, w_in, q_a_norm, w_q_b, kv_a_norm, w_kv_b, ret_norm, w_branch_att, w_branch_ret, w_out, ffn2_norm, ffn2_w_gate, ffn2_w_up, ffn2_w_down, ple_norm, w_ple_gate, w_ple_proj, final_norm):
    batch, seq, _ = x_prompt.shape
    n_dec, dec_seq, _ = x_sample.shape
    depth = w_in.shape[0]
    assert dec_seq == 1 and depth == 1
    n_past = page_table.shape[1] * PAGE_SIZE

    w = _layer_weights(0, ffn1_norm, ffn1_w_gate, ffn1_w_up, ffn1_w_down, mix_norm, w_in, q_a_norm,
                       w_q_b, kv_a_norm, w_kv_b, ret_norm, w_branch_att, w_branch_ret, w_out,
                       ffn2_norm, ffn2_w_gate, ffn2_w_up, ffn2_w_down, ple_norm, w_ple_gate,
                       w_ple_proj, final_norm)

    tabs_p = _rope_tables(np.arange(seq))
    (h_p, qt_p, k_p, vt_p, ckv_p, kpet_p, qr_p, kr_p, rv_p) = _front_stage(
        x_prompt.reshape(batch * seq, D_MODEL), tabs_p, w, TOKEN_TILE, seq // TOKEN_TILE)
    bt = lambda a: a.reshape(batch, seq, a.shape[-1])
    o_att_p = _prompt_attention(qt_p, k_p, vt_p, batch, seq)
    o_ret_p, ret_p = _prompt_retention(bt(qr_p), bt(kr_p), bt(rv_p), w['ret_norm'])
    y_p = _back_stage(h_p, o_att_p, o_ret_p.reshape(batch * seq, -1),
                      p_prompt.reshape(batch * seq, PLE_DIM), w, TOKEN_TILE)

    tabs_1 = _rope_tables(np.full((1,), n_past))
    tabs_s = (tuple(np.ascontiguousarray(np.broadcast_to(t, (n_dec, LANES))) for t in tabs_1[:5])
              + tuple(np.ascontiguousarray(np.broadcast_to(t, (ROPE_HALF, n_dec))) for t in tabs_1[5:]))
    (h_s, qt_s, _, _, ckv_s, kpet_s, qr_s, kr_s, rv_s) = _front_stage(
        x_sample.reshape(n_dec, D_MODEL), tabs_s, w, n_dec, 1)
    q_abs = jnp.transpose(_absorb_q(qt_s, w['w_absorb_t']), (2, 0, 1))
    kpe_s = jnp.transpose(kpet_s[0])
    k_new = jnp.concatenate([ckv_s, kpe_s,
                             jnp.zeros((n_dec, KEY_W - KV_LORA - MLA_ROPE), F32)], axis=1)
    o_lat = _paged_attention(page_table, q_abs, k_new.reshape(n_dec, 1, KEY_W), cache_ckv,
                             jnp.swapaxes(cache_kpe, 2, 3))
    o_att_s = _unabsorb(jnp.transpose(o_lat, (1, 0, 2)), w['w_unabsorb'])
    o_ret_s, ret_s = _sample_retention(qr_s, kr_s, rv_s, state_ret, w['ret_norm'])
    y_s = _back_stage(h_s, o_att_s, o_ret_s, p_sample.reshape(n_dec, PLE_DIM), w, n_dec)

    return (y_p.reshape(batch, seq, D_MODEL),
            y_s.reshape(n_dec, 1, D_MODEL),
            ckv_p.reshape(1, batch, seq, KV_LORA),
            jnp.transpose(kpet_p, (0, 2, 1))[None],
            ret_p[None],
            ckv_s.reshape(1, n_dec, 1, KV_LORA),
            kpe_s.reshape(1, n_dec, 1, MLA_ROPE),
            ret_s)
```
